```python
import jax, jax.numpy as jnp
from jax import lax
import numpy as np

D_MODEL = 1024
BATCH = 4
SEQ = 4096
DEPTH = 1

HEAD_DIM = 64
A_HEADS = 8
A_KV_HEADS = 2
A_GROUP = A_HEADS // A_KV_HEADS
A_HALF_WINDOW = 128
A_BLOCK = 128
B_HEADS = 8
DILATED_PATTERNS = ((128, 1), (512, 4), (2048, 16))
B_BLOCK = 64
A_Q = A_HEADS * HEAD_DIM
A_KV = A_KV_HEADS * HEAD_DIM
B_W = B_HEADS * HEAD_DIM
MIX_WIDTH = A_Q + B_W
IN_WIDTH = A_Q + 2 * A_KV + 3 * B_W
ROPE_THETA = 500000.0
ROT_DIM = HEAD_DIM // 4
N_EXPERTS = 16
CAPACITY_FACTOR = 2
EXPERT_FF = 2048
NORM_EPS = 1e-6
NEG_INF = -1e30

kernel_name = "hybrid_gqa_sink_dilated_ec_moe"


def rms_norm(x, g):
    xf = x.astype(jnp.float32)
    y = xf * lax.rsqrt(jnp.mean(xf * xf, axis=-1, keepdims=True) + NORM_EPS)
    return (y * g.astype(jnp.float32)).astype(x.dtype)


def rope_tables(seq_len):
    inv_freq = ROPE_THETA ** (-jnp.arange(0, ROT_DIM, 2, dtype=jnp.float32) / ROT_DIM)
    ang = jnp.arange(seq_len, dtype=jnp.float32)[:, None] * inv_freq[None, :]
    return jnp.cos(ang), jnp.sin(ang)


def partial_rope(x, cos, sin):
    half = ROT_DIM // 2
    shape = (1, x.shape[1]) + (1,) * (x.ndim - 3) + (half,)
    c = cos.reshape(shape)
    s = sin.reshape(shape)
    xf = x.astype(jnp.float32)
    x1, x2, rest = xf[..., :half], xf[..., half:ROT_DIM], xf[..., ROT_DIM:]
    out = jnp.concatenate([x1 * c - x2 * s, x2 * c + x1 * s, rest], axis=-1)
    return out.astype(x.dtype)


def banded_attention(q, k, v, half_window, block, sink=None):
    n, L, hk, g, dh = q.shape
    nb = -(-L // block)
    lp = nb * block
    pad = lp - L
    qp = jnp.pad(q, ((0, 0), (0, pad), (0, 0), (0, 0), (0, 0)))
    kp = jnp.pad(k, ((0, 0), (block, block + pad), (0, 0), (0, 0)))
    vp = jnp.pad(v, ((0, 0), (block, block + pad), (0, 0), (0, 0)))

    def windows(t):
        return jnp.concatenate(
            [t[:, i * block:i * block + lp].reshape(n, nb, block, hk, dh) for i in range(3)], axis=2)

    qb = qp.reshape(n, nb, block, hk, g, dh)
    kw, vw = windows(kp), windows(vp)
    qpos = jnp.arange(nb)[:, None] * block + jnp.arange(block)[None, :]
    kpos = (jnp.arange(nb)[:, None] - 1) * block + jnp.arange(3 * block)[None]
    rel = kpos[:, None, :] - qpos[:, :, None]
    valid = (jnp.abs(rel) <= half_window) & (kpos[:, None, :] >= 0) & (kpos[:, None, :] < L)

    s = jnp.einsum('nbqhgd,nbkhd->nbhgqk', qb.astype(jnp.float32), kw.astype(jnp.float32)) * (dh ** -0.5)
    s = jnp.where(valid[None, :, None, None], s, NEG_INF)
    m = jnp.max(s, axis=-1, keepdims=True)
    if sink is not None:
        sk = sink.astype(jnp.float32)[None, None, :, :, None, None]
        m = jnp.maximum(m, sk)
    p = jnp.exp(s - m)
    denom = jnp.sum(p, axis=-1, keepdims=True)
    if sink is not None:
        denom = denom + jnp.exp(sk - m)
    o = jnp.einsum('nbhgqk,nbkhd->nbhgqd', p, vw.astype(jnp.float32)) / denom
    lse = (m + jnp.log(denom))[..., 0]
    o = o.transpose(0, 1, 4, 2, 3, 5).reshape(n, lp, hk, g, dh)[:, :L]
    lse = lse.transpose(0, 1, 4, 2, 3).reshape(n, lp, hk, g)[:, :L]
    return o.astype(q.dtype), lse


def dilated_attention(q, k, v):
    b, s, h, dh = q.shape
    outs, lses = [], []
    for (w, d) in DILATED_PATTERNS:
        hw = w // (2 * d)

        def fold(t):
            return t.reshape(b, s // d, d, h, dh).transpose(0, 2, 1, 3, 4).reshape(b * d, s // d, h, dh)

        o, lse = banded_attention(fold(q)[:, :, :, None], fold(k), fold(v), hw, B_BLOCK)
        o = o[:, :, :, 0].reshape(b, d, s // d, h, dh).transpose(0, 2, 1, 3, 4).reshape(b, s, h, dh)
        lse = lse[..., 0].reshape(b, d, s // d, h).transpose(0, 2, 1, 3).reshape(b, s, h)
        outs.append(o.astype(jnp.float32))
        lses.append(lse)
    alpha = jax.nn.softmax(jnp.stack(lses, axis=0), axis=0)
    out = jnp.sum(alpha[..., None] * jnp.stack(outs, axis=0), axis=0)
    return out.astype(q.dtype)


def expert_choice_moe(h, w_router, w_gate, w_up, w_down):
    b, s, dm = h.shape
    cap = CAPACITY_FACTOR * s // N_EXPERTS
    logits = jnp.einsum('bsd,de->bse', h.astype(jnp.float32), w_router.astype(jnp.float32))
    affinity = jax.nn.softmax(logits, axis=-1)
    gates, idx = lax.top_k(jnp.swapaxes(affinity, 1, 2), cap)
    xe = jax.vmap(lambda hb, ib: hb[ib])(h, idx)
    a = jnp.einsum('becd,edf->becf', xe, w_gate)
    u = jnp.einsum('becd,edf->becf', xe, w_up)
    ye = jnp.einsum('becf,efd->becd', jax.nn.silu(a) * u, w_down)
    ye = ye * gates[..., None].astype(ye.dtype)
    out = jax.vmap(lambda yb, ib: jax.ops.segment_sum(
        yb.reshape(-1, dm), ib.reshape(-1), num_segments=s))(ye, idx)
    return out.astype(h.dtype)


def setup_inputs(seed: int = 0) -> dict:
    key = jax.random.key(seed)
    ks = jax.random.split(key, 14)
    f32 = jnp.float32
    nrm = lambda k, shape, fan: jax.random.normal(k, shape, f32) * (fan ** -0.5)
    gain = lambda k, shape: 1.0 + 0.02 * jax.random.normal(k, shape, f32)
    return {
        "x": jax.random.normal(ks[0], (BATCH, SEQ, D_MODEL), f32),
        "g_mix": gain(ks[1], (DEPTH, D_MODEL)),
        "w_in": nrm(ks[2], (DEPTH, D_MODEL, IN_WIDTH), D_MODEL),
        "a_sink": 0.5 * jax.random.normal(ks[3], (DEPTH, A_HEADS), f32),
        "g_out_a": gain(ks[4], (DEPTH, A_Q)),
        "g_out_b": gain(ks[5], (DEPTH, B_W)),
        "w_out": nrm(ks[6], (DEPTH, MIX_WIDTH, D_MODEL), MIX_WIDTH),
        "g_ffn": gain(ks[7], (DEPTH, D_MODEL)),
        "w_router": nrm(ks[8], (DEPTH, D_MODEL, N_EXPERTS), D_MODEL),
        "w_gate": nrm(ks[9], (DEPTH, N_EXPERTS, D_MODEL, EXPERT_FF), D_MODEL),
        "w_up": nrm(ks[10], (DEPTH, N_EXPERTS, D_MODEL, EXPERT_FF), D_MODEL),
        "w_down": nrm(ks[11], (DEPTH, N_EXPERTS, EXPERT_FF, D_MODEL), EXPERT_FF),
        "g_final": gain(ks[12], (D_MODEL,)),
    }


def reference(x, g_mix, w_in, a_sink, g_out_a, g_out_b, w_out, g_ffn,
              w_router, w_gate, w_up, w_down, g_final):
    b, s, _ = x.shape
    cos, sin = rope_tables(s)
    splits = np.cumsum([A_Q, A_KV, A_KV, B_W, B_W]).tolist()
    for l in range(DEPTH):
        h = rms_norm(x, g_mix[l])
        proj = jnp.einsum('bsd,de->bse', h, w_in[l])
        qa, ka, va, qb, kb, vb = jnp.split(proj, splits, axis=-1)
        qa = partial_rope(qa.reshape(b, s, A_KV_HEADS, A_GROUP, HEAD_DIM), cos, sin)
        ka = partial_rope(ka.reshape(b, s, A_KV_HEADS, HEAD_DIM), cos, sin)
        va = va.reshape(b, s, A_KV_HEADS, HEAD_DIM)
        oa, _ = banded_attention(qa, ka, va, A_HALF_WINDOW, A_BLOCK,
                                 sink=a_sink[l].reshape(A_KV_HEADS, A_GROUP))
        oa = oa.reshape(b, s, A_Q)
        qb = partial_rope(qb.reshape(b, s, B_HEADS, HEAD_DIM), cos, sin)
        kb = partial_rope(kb.reshape(b, s, B_HEADS, HEAD_DIM), cos, sin)
        vb = vb.reshape(b, s, B_HEADS, HEAD_DIM)
        ob = dilated_attention(qb, kb, vb).reshape(b, s, B_W)
        mixed = jnp.concatenate([rms_norm(oa, g_out_a[l]), rms_norm(ob, g_out_b[l])], axis=-1)
        x = x + jnp.einsum('bse,ed->bsd', mixed, w_out[l])
        h2 = rms_norm(x, g_ffn[l])
        x = x + expert_choice_moe(h2, w_router[l], w_gate[l], w_up[l], w_down[l])
    return rms_norm(x, g_final)
```

```python
import functools

import jax
import jax.numpy as jnp
from jax import lax
from jax.experimental import pallas as pl
from jax.experimental.pallas import tpu as pltpu

F32 = jnp.float32
BF16 = jnp.bfloat16

LANES = 128
SUBLANES = 8
VMEM_LIMIT = 56 * 1024 * 1024

HEAD_DIM = 64
A_HEADS = 8
A_KV_HEADS = 2
A_GROUP = A_HEADS // A_KV_HEADS
A_HALF_WINDOW = 128
B_HEADS = 8
DILATED_PATTERNS = ((128, 1), (512, 4), (2048, 16))
A_Q = A_HEADS * HEAD_DIM
A_KV = A_KV_HEADS * HEAD_DIM
B_W = B_HEADS * HEAD_DIM
ROPE_THETA = 500000.0
ROT_DIM = HEAD_DIM // 4
N_EXPERTS = 16
CAPACITY_FACTOR = 2
NORM_EPS = 1e-6
NEG_INF = -1e30

PROJ_TM = 512
ATT_A_TQ = 512
ATT_A_QB = 128
ATT_B_QB = 128
MIX_TM = 512
FFN_TF = 512
FFN_TM = 512
FINAL_TM = 512
RMW_UNROLL = 8


def _cparams(sem, vmem=None):
    return pltpu.CompilerParams(dimension_semantics=sem, vmem_limit_bytes=vmem)


def _rope_tables(seq_len):
    half = ROT_DIM // 2
    inv_freq = ROPE_THETA ** (-jnp.arange(0, ROT_DIM, 2, dtype=F32) / ROT_DIM)
    ang = jnp.arange(seq_len, dtype=F32)[:, None] * inv_freq[None, :]
    cos, sin = jnp.cos(ang), jnp.sin(ang)
    ones = jnp.ones((seq_len, HEAD_DIM - ROT_DIM), F32)
    zeros = jnp.zeros((seq_len, HEAD_DIM - ROT_DIM), F32)
    zh = jnp.zeros((seq_len, half), F32)
    cos_t = jnp.concatenate([cos, cos, ones], axis=1)
    sin_a = jnp.concatenate([-sin, zh, zeros], axis=1)
    sin_b = jnp.concatenate([zh, sin, zeros], axis=1)
    rep = LANES // HEAD_DIM
    return tuple(jnp.tile(t, (1, rep)) for t in (cos_t, sin_a, sin_b))


def _rms(x, g):
    return x * lax.rsqrt(jnp.mean(x * x, axis=-1, keepdims=True) + NORM_EPS) * g


def _proj_kernel(x_ref, g_ref, w_ref, cos_ref, sa_ref, sb_ref,
                 qa_ref, ka_ref, va_ref, qb_ref, kb_ref, vb_ref):
    h = _rms(x_ref[...], g_ref[...]).astype(BF16)
    cos, sa, sb = cos_ref[...], sa_ref[...], sb_ref[...]
    half = ROT_DIM // 2

    def rope(t):
        return t * cos + pltpu.roll(t, LANES - half, 1) * sa + pltpu.roll(t, half, 1) * sb

    def section(col0, width):
        return jnp.dot(h, w_ref[:, col0:col0 + width], preferred_element_type=F32)

    scale = HEAD_DIM ** -0.5
    qa = section(0, A_Q)
    for c in range(A_Q // LANES):
        qa_ref[:, c * LANES:(c + 1) * LANES] = rope(qa[:, c * LANES:(c + 1) * LANES]) * scale
    kv = section(A_Q, 2 * A_KV)
    ka_ref[...] = rope(kv[:, :A_KV])
    va_ref[...] = kv[:, A_KV:]
    qb = section(A_Q + 2 * A_KV, B_W)
    for c in range(B_W // LANES):
        qb_ref[:, c * LANES:(c + 1) * LANES] = rope(qb[:, c * LANES:(c + 1) * LANES]) * scale
    kb = section(A_Q + 2 * A_KV + B_W, B_W)
    for c in range(B_W // LANES):
        kb_ref[:, c * LANES:(c + 1) * LANES] = rope(kb[:, c * LANES:(c + 1) * LANES])
    vb_ref[...] = section(A_Q + 2 * A_KV + 2 * B_W, B_W)


def _proj(x2, g_mix, w_in, seq):
    t, d = x2.shape
    tm = PROJ_TM
    assert t % tm == 0 and seq % tm == 0 and A_KV == LANES
    nseq = seq // tm
    tables = _rope_tables(seq)
    row = lambda i: (i, 0)
    const = lambda i: (0, 0)
    tab = lambda i: (i % nseq, 0)
    widths = (A_Q, A_KV, A_KV, B_W, B_W, B_W)
    return pl.pallas_call(
        _proj_kernel,
        grid=(t // tm,),
        in_specs=[pl.BlockSpec((tm, d), row), pl.BlockSpec((1, d), const),
                  pl.BlockSpec(w_in.shape, const)] + [pl.BlockSpec((tm, LANES), tab)] * 3,
        out_specs=[pl.BlockSpec((tm, w), row) for w in widths],
        out_shape=[jax.ShapeDtypeStruct((t, w), F32) for w in widths],
        compiler_params=_cparams(("parallel",), VMEM_LIMIT),
        name="proj",
    )(x2, g_mix.reshape(1, d), w_in.astype(BF16), *tables)


def _attn_a_kernel(sink_ref, q_ref, kp_ref, kc_ref, kn_ref, vp_ref, vc_ref, vn_ref, o_ref, *, seq):
    i = pl.program_id(1)
    tq, qb, hw = ATT_A_TQ, ATT_A_QB, A_HALF_WINDOW
    kw = qb + 2 * hw
    k_all = jnp.concatenate([kp_ref[...], kc_ref[...], kn_ref[...]], axis=0).astype(BF16)
    v_all = jnp.concatenate([vp_ref[...], vc_ref[...], vn_ref[...]], axis=0).astype(BF16)
    row = lax.broadcasted_iota(jnp.int32, (qb, kw), 0)
    col = lax.broadcasted_iota(jnp.int32, (qb, kw), 1)
    band = jnp.abs(col - hw - row) <= hw
    for sb in range(tq // qb):
        kpos = i * tq + sb * qb - hw + col
        valid = band & (kpos >= 0) & (kpos < seq)
        q = q_ref[sb * qb:(sb + 1) * qb, :].astype(BF16)
        k_win = k_all[sb * qb:sb * qb + kw, :]
        v_win = v_all[sb * qb:sb * qb + kw, :]
        outs = []
        for h in range(A_KV_HEADS):
            k_h = k_win[:, h * HEAD_DIM:(h + 1) * HEAD_DIM]
            v_h = v_win[:, h * HEAD_DIM:(h + 1) * HEAD_DIM]
            for g in range(A_GROUP):
                head = h * A_GROUP + g
                q_g = q[:, head * HEAD_DIM:(head + 1) * HEAD_DIM]
                s = lax.dot_general(q_g, k_h, (((1,), (1,)), ((), ())), preferred_element_type=F32)
                s = jnp.where(valid, s, NEG_INF)
                sk = sink_ref[head]
                m = jnp.maximum(jnp.max(s, axis=1, keepdims=True), sk)
                p = jnp.exp(s - m)
                den = jnp.sum(p, axis=1, keepdims=True) + jnp.exp(sk - m)
                pv = jnp.dot(p.astype(BF16), v_h, preferred_element_type=F32)
                outs.append(pv / den)
        o_ref[sb * qb:(sb + 1) * qb, :] = jnp.concatenate(outs, axis=1)


def _attn_a(qa, ka, va, sink, batch, seq):
    tq, hw = ATT_A_TQ, A_HALF_WINDOW
    assert seq % tq == 0 and tq % hw == 0 and ATT_A_QB == hw
    nq = seq // tq
    r = tq // hw
    nhb = seq // hw
    qmap = lambda b, i: (b * nq + i, 0)
    prev = lambda b, i: (b * nhb + jnp.maximum(i * r - 1, 0), 0)
    nxt = lambda b, i: (b * nhb + jnp.minimum((i + 1) * r, nhb - 1), 0)
    kv_specs = [pl.BlockSpec((hw, A_KV), prev), pl.BlockSpec((tq, A_KV), qmap), pl.BlockSpec((hw, A_KV), nxt)]
    return pl.pallas_call(
        functools.partial(_attn_a_kernel, seq=seq),
        grid=(batch, nq),
        in_specs=[pl.BlockSpec(memory_space=pltpu.SMEM), pl.BlockSpec((tq, A_Q), qmap)] + kv_specs * 2,
        out_specs=pl.BlockSpec((tq, A_Q), qmap),
        out_shape=jax.ShapeDtypeStruct(qa.shape, F32),
        compiler_params=_cparams(("parallel", "parallel"), VMEM_LIMIT),
        name="attn_a",
    )(sink, qa, ka, ka, ka, va, va, va)


def _attn_b_kernel(q_ref, k_ref, v_ref, o_ref, acc_ref, m_ref, l_ref, *, seq):
    qb = ATT_B_QB
    lane = lax.broadcasted_iota(jnp.int32, (qb, LANES), 1)
    lo = lane < HEAD_DIM
    n_pat = len(DILATED_PATTERNS)
    for pi, (w, d) in enumerate(DILATED_PATTERNS):
        n = seq // d
        hw = w // (2 * d)
        kw = qb + 2 * hw
        nblk = n // qb
        row = lax.broadcasted_iota(jnp.int32, (qb, kw), 0)
        col = lax.broadcasted_iota(jnp.int32, (qb, kw), 1)

        def body(it, carry, d=d, n=n, hw=hw, kw=kw, nblk=nblk, row=row, col=col, pi=pi):
            r = it // nblk
            qs = (it % nblk) * qb
            ks = jnp.clip(qs - hw, 0, n - kw)
            q_rows = pl.ds(qs * d + r, qb, stride=d)
            k_rows = pl.ds(ks * d + r, kw, stride=d)
            q = q_ref[q_rows, :]
            k = k_ref[k_rows, :].astype(BF16)
            v = v_ref[k_rows, :].astype(BF16)
            valid = jnp.abs(ks + col - qs - row) <= hw
            ms, ls, pvs = [], [], []
            for hh in range(LANES // HEAD_DIM):
                q_h = jnp.where(lo if hh == 0 else ~lo, q, 0.0).astype(BF16)
                s = lax.dot_general(q_h, k, (((1,), (1,)), ((), ())), preferred_element_type=F32)
                s = jnp.where(valid, s, NEG_INF)
                m = jnp.max(s, axis=1, keepdims=True)
                p = jnp.exp(s - m)
                ms.append(m)
                ls.append(jnp.sum(p, axis=1, keepdims=True))
                pvs.append(jnp.dot(p.astype(BF16), v, preferred_element_type=F32))
            m_new = jnp.where(lo, ms[0], ms[1])
            l_new = jnp.where(lo, ls[0], ls[1])
            acc_new = jnp.where(lo, pvs[0], pvs[1])
            if pi > 0:
                m_old, l_old, acc_old = m_ref[q_rows, :], l_ref[q_rows, :], acc_ref[q_rows, :]
                m_tot = jnp.maximum(m_old, m_new)
                a_old, a_new = jnp.exp(m_old - m_tot), jnp.exp(m_new - m_tot)
                l_new = a_old * l_old + a_new * l_new
                acc_new = a_old * acc_old + a_new * acc_new
                m_new = m_tot
            if pi == n_pat - 1:
                o_ref[q_rows, :] = acc_new / l_new
            else:
                m_ref[q_rows, :] = m_new
                l_ref[q_rows, :] = l_new
                acc_ref[q_rows, :] = acc_new
            return carry

        lax.fori_loop(0, d * nblk, body, 0)


def _attn_b(qb, kb, vb, batch, seq):
    for (w, d) in DILATED_PATTERNS:
        assert seq % d == 0 and (seq // d) % ATT_B_QB == 0 and seq // d >= ATT_B_QB + w // d
    npair = B_W // LANES
    spec = pl.BlockSpec((seq, LANES), lambda b, p: (b, p))
    return pl.pallas_call(
        functools.partial(_attn_b_kernel, seq=seq),
        grid=(batch, npair),
        in_specs=[spec] * 3,
        out_specs=spec,
        out_shape=jax.ShapeDtypeStruct(qb.shape, F32),
        scratch_shapes=[pltpu.VMEM((seq, LANES), F32)] * 3,
        compiler_params=_cparams(("parallel", "parallel"), VMEM_LIMIT),
        name="attn_b",
    )(qb, kb, vb)


def _split3(v):
    hi = v.astype(BF16)
    r1 = v - hi.astype(F32)
    mid = r1.astype(BF16)
    lo = (r1 - mid.astype(F32)).astype(BF16)
    return hi, mid, lo


def _mix_kernel(x_ref, oa_ref, ob_ref, ga_ref, gb_ref, wo_ref, gf_ref, wr_ref,
                x1_ref, h2_ref, aff_ref):
    na = _rms(oa_ref[...], ga_ref[...]).astype(BF16)
    nb = _rms(ob_ref[...], gb_ref[...]).astype(BF16)
    x1 = (x_ref[...] + jnp.dot(na, wo_ref[:A_Q, :], preferred_element_type=F32)
          + jnp.dot(nb, wo_ref[A_Q:, :], preferred_element_type=F32))
    x1_ref[...] = x1
    h2 = _rms(x1, gf_ref[...])
    tm = h2.shape[0]
    for c in range(h2.shape[1] // LANES):
        h2_ref[pl.ds(c, tm, stride=SUBLANES), :] = h2[:, c * LANES:(c + 1) * LANES]
    dn = (((1,), (1,)), ((), ()))
    h_hi, h_mid, _ = _split3(h2)
    w_hi, w_mid, _ = _split3(wr_ref[...])
    logits = (lax.dot_general(w_hi, h_hi, dn, preferred_element_type=F32)
              + lax.dot_general(w_mid, h_hi, dn, preferred_element_type=F32)
              + lax.dot_general(w_hi, h_mid, dn, preferred_element_type=F32))
    e = jnp.exp(logits - jnp.max(logits, axis=0, keepdims=True))
    aff_ref[...] = e / jnp.sum(e, axis=0, keepdims=True)


def _mix(x2, oa, ob, g_out_a, g_out_b, w_out, g_ffn, w_router, batch, seq):
    t, d = x2.shape
    tm = MIX_TM
    assert seq % tm == 0 and d % LANES == 0 and d // LANES == SUBLANES
    ns = seq // tm
    row = lambda i: (i, 0)
    const = lambda i: (0, 0)
    ne = w_router.shape[1]
    return pl.pallas_call(
        _mix_kernel,
        grid=(t // tm,),
        in_specs=[pl.BlockSpec((tm, d), row), pl.BlockSpec((tm, A_Q), row), pl.BlockSpec((tm, B_W), row),
                  pl.BlockSpec((1, A_Q), const), pl.BlockSpec((1, B_W), const),
                  pl.BlockSpec(w_out.shape, const), pl.BlockSpec((1, d), const),
                  pl.BlockSpec((ne, d), const)],
        out_specs=[pl.BlockSpec((tm, d), row), pl.BlockSpec((tm * SUBLANES, LANES), row),
                   pl.BlockSpec((None, ne, tm), lambda i: (i // ns, 0, i % ns))],
        out_shape=[jax.ShapeDtypeStruct((t, d), F32),
                   jax.ShapeDtypeStruct((t * SUBLANES, LANES), F32),
                   jax.ShapeDtypeStruct((batch, ne, seq), F32)],
        compiler_params=_cparams(("parallel",), VMEM_LIMIT),
        name="mix",
    )(x2, oa, ob, g_out_a.reshape(1, -1), g_out_b.reshape(1, -1), w_out.astype(BF16),
      g_ffn.reshape(1, d), w_router.T)


def _lane_cumsum(mask_f, tri):
    rows, s = mask_f.shape
    carry = jnp.zeros((rows, 1), F32)
    out = []
    for c in range(s // LANES):
        inc = jnp.dot(mask_f[:, c * LANES:(c + 1) * LANES].astype(BF16), tri,
                      preferred_element_type=F32) + carry
        out.append(inc)
        carry = inc[:, LANES - 1:LANES]
    return jnp.concatenate(out, axis=1)


def _route_kernel(aff_ref, idx_ref, gate_ref, *, cap, seq):
    b = pl.program_id(0)
    aff = aff_ref[...]
    ne = aff.shape[0]

    def search(i, thr):
        cand = thr | (jnp.int32(1) << (30 - i))
        cnt = jnp.sum((aff >= lax.bitcast_convert_type(cand, F32)).astype(F32), axis=1, keepdims=True)
        return jnp.where(cnt >= cap, cand, thr)

    thr = lax.bitcast_convert_type(lax.fori_loop(0, 31, search, jnp.zeros((ne, 1), jnp.int32)), F32)
    gt = aff > thr
    eq = aff == thr
    n_gt = jnp.sum(gt.astype(F32), axis=1, keepdims=True)
    ri = lax.broadcasted_iota(jnp.int32, (LANES, LANES), 0)
    ci = lax.broadcasted_iota(jnp.int32, (LANES, LANES), 1)
    tri = (ri <= ci).astype(BF16)
    eq_f = eq.astype(F32)
    eq_rank = _lane_cumsum(eq_f, tri) - eq_f
    sel = gt | (eq & (eq_rank < cap - n_gt))
    sel_f = sel.astype(F32)
    slot = _lane_cumsum(sel_f, tri) - sel_f
    key_t = jnp.where(sel, slot, -1.0).T
    tok = lax.broadcasted_iota(jnp.int32, (SUBLANES, seq), 1)
    piece = lax.broadcasted_iota(jnp.int32, (SUBLANES, seq), 0)
    tok_rows = jnp.where(piece == 0, tok // 64, jnp.where(piece == 1, tok % 64, 0)).astype(F32)
    p_iota = lax.broadcasted_iota(jnp.int32, (1, cap), 1).astype(F32)
    for e in range(ne):
        onehot = (key_t[:, e:e + 1] == p_iota).astype(BF16)
        g = aff[e:e + 1, :]
        g_hi = g.astype(BF16).astype(F32)
        g_mid = (g - g_hi).astype(BF16).astype(F32)
        g_lo = g - g_hi - g_mid
        lhs = jnp.where(piece == 2, g_hi, jnp.where(piece == 3, g_mid, jnp.where(piece == 4, g_lo, tok_rows)))
        res = jnp.dot(lhs.astype(BF16), onehot, preferred_element_type=F32)
        idx_ref[e:e + 1, :] = (res[0:1] * 64.0 + res[1:2]).astype(jnp.int32) + b * seq
        gate_ref[e:e + 1, :] = res[2:3] + res[3:4] + res[4:5]


def _route(aff_t, cap):
    batch, ne, seq = aff_t.shape
    assert seq % LANES == 0 and seq <= 64 * 256
    return pl.pallas_call(
        functools.partial(_route_kernel, cap=cap, seq=seq),
        grid=(batch,),
        in_specs=[pl.BlockSpec((None, ne, seq), lambda b: (b, 0, 0))],
        out_specs=[pl.BlockSpec((None, ne, cap), lambda b: (b, 0, 0))] * 2,
        out_shape=[jax.ShapeDtypeStruct((batch, ne, cap), jnp.int32),
                   jax.ShapeDtypeStruct((batch, ne, cap), F32)],
        compiler_params=_cparams(("parallel",), VMEM_LIMIT),
        name="route",
    )(aff_t)


def _experts_kernel(idx_ref, h_hbm, wg_ref, wu_ref, wd_ref, y_ref, rows_ref, xe_ref, sem):
    f = pl.program_id(1)
    ntok, d = xe_ref.shape

    @pl.when(f == 0)
    def _gather():
        def issue(i, carry):
            src = pl.multiple_of(idx_ref[0, 0, i] * SUBLANES, SUBLANES)
            dst = pl.multiple_of(i * SUBLANES, SUBLANES)
            pltpu.make_async_copy(h_hbm.at[pl.ds(src, SUBLANES), :],
                                  rows_ref.at[pl.ds(dst, SUBLANES), :], sem).start()
            return carry

        lax.fori_loop(0, ntok, issue, 0, unroll=8)
        pltpu.make_async_copy(h_hbm.at[pl.ds(0, ntok * SUBLANES), :], rows_ref, sem).wait()
        for m in range(ntok // FFN_TM):
            xe_ref[m * FFN_TM:(m + 1) * FFN_TM, :] = jnp.concatenate(
                [rows_ref[pl.ds(m * FFN_TM * SUBLANES + c, FFN_TM, stride=SUBLANES), :]
                 for c in range(d // LANES)], axis=1).astype(BF16)
        y_ref[...] = jnp.zeros_like(y_ref)

    wg = wg_ref[...].astype(BF16)
    wu = wu_ref[...].astype(BF16)
    wd = wd_ref[...].astype(BF16)
    for m in range(ntok // FFN_TM):
        xe = xe_ref[m * FFN_TM:(m + 1) * FFN_TM, :]
        a = jnp.dot(xe, wg, preferred_element_type=F32)
        u = jnp.dot(xe, wu, preferred_element_type=F32)
        hmid = (a * jax.nn.sigmoid(a) * u).astype(BF16)
        y_ref[m * FFN_TM:(m + 1) * FFN_TM, :] += jnp.dot(hmid, wd, preferred_element_type=F32)


def _experts(idx_e, h2_rows, w_gate, w_up, w_down):
    ne, d, ff = w_gate.shape
    ntok = idx_e.shape[-1]
    tf = FFN_TF
    assert ff % tf == 0 and ntok % FFN_TM == 0 and d // LANES == SUBLANES
    return pl.pallas_call(
        _experts_kernel,
        grid=(ne, ff // tf),
        in_specs=[pl.BlockSpec((1, 1, ntok), lambda e, f: (e, 0, 0), memory_space=pltpu.SMEM),
                  pl.BlockSpec(memory_space=pl.ANY),
                  pl.BlockSpec((None, d, tf), lambda e, f: (e, 0, f)),
                  pl.BlockSpec((None, d, tf), lambda e, f: (e, 0, f)),
                  pl.BlockSpec((None, tf, d), lambda e, f: (e, f, 0))],
        out_specs=pl.BlockSpec((None, ntok, d), lambda e, f: (e, 0, 0)),
        out_shape=jax.ShapeDtypeStruct((ne, ntok, d), F32),
        scratch_shapes=[pltpu.VMEM((ntok * SUBLANES, LANES), F32), pltpu.VMEM((ntok, d), BF16),
                        pltpu.SemaphoreType.DMA(())],
        compiler_params=_cparams(("arbitrary", "arbitrary"), VMEM_LIMIT),
        name="experts",
    )(idx_e, h2_rows, w_gate, w_up, w_down)


def _combine_kernel(idx_ref, gate_ref, y_ref, acc_ref, ybuf_ref):
    e = pl.program_id(1)
    cap, d = y_ref.shape

    @pl.when(e == 0)
    def _init():
        acc_ref[...] = jnp.zeros_like(acc_ref)

    for c in range(d // LANES):
        ybuf_ref[pl.ds(c, cap, stride=SUBLANES), :] = y_ref[:, c * LANES:(c + 1) * LANES]

    def group(j, carry):
        rows, sums = [], []
        for u in range(RMW_UNROLL):
            i = j * RMW_UNROLL + u
            t = pl.ds(pl.multiple_of(idx_ref[0, 0, i] * SUBLANES, SUBLANES), SUBLANES)
            yi = ybuf_ref[pl.ds(pl.multiple_of(i * SUBLANES, SUBLANES), SUBLANES), :]
            rows.append(t)
            sums.append(acc_ref[t, :] + gate_ref[0, 0, i] * yi)
        for t, s in zip(rows, sums):
            acc_ref[t, :] = s
        return carry

    lax.fori_loop(0, cap // RMW_UNROLL, group, 0)


def _combine(idx_l, gates, y, batch, seq):
    ne, ntok, d = y.shape
    cap = ntok // batch
    assert cap % RMW_UNROLL == 0 and d // LANES == SUBLANES
    smem = pl.BlockSpec((1, 1, cap), lambda b, e: (b * ne + e, 0, 0), memory_space=pltpu.SMEM)
    return pl.pallas_call(
        _combine_kernel,
        grid=(batch, ne),
        in_specs=[smem, smem, pl.BlockSpec((None, cap, d), lambda b, e: (e, b, 0))],
        out_specs=pl.BlockSpec((seq * SUBLANES, LANES), lambda b, e: (b, 0)),
        out_shape=jax.ShapeDtypeStruct((batch * seq * SUBLANES, LANES), F32),
        scratch_shapes=[pltpu.VMEM((cap * SUBLANES, LANES), F32)],
        compiler_params=_cparams(("parallel", "arbitrary"), VMEM_LIMIT),
        name="combine",
    )(idx_l.reshape(batch * ne, 1, cap), gates.reshape(batch * ne, 1, cap), y)


def _final_kernel(x1_ref, moe_ref, g_ref, o_ref):
    tm, d = x1_ref.shape
    moe = jnp.concatenate([moe_ref[pl.ds(c, tm, stride=SUBLANES), :] for c in range(d // LANES)], axis=1)
    o_ref[...] = _rms(x1_ref[...] + moe, g_ref[...])


def _final(x1, moe_rows, g_final):
    t, d = x1.shape
    tm = FINAL_TM
    row = lambda i: (i, 0)
    return pl.pallas_call(
        _final_kernel,
        grid=(t // tm,),
        in_specs=[pl.BlockSpec((tm, d), row), pl.BlockSpec((tm * SUBLANES, LANES), row),
                  pl.BlockSpec((1, d), lambda i: (0, 0))],
        out_specs=pl.BlockSpec((tm, d), row),
        out_shape=jax.ShapeDtypeStruct((t, d), F32),
        compiler_params=_cparams(("parallel",), VMEM_LIMIT),
        name="final",
    )(x1, moe_rows, g_final.reshape(1, d))


def kernel(x, g_mix, w_in, a_sink, g_out_a, g_out_b, w_out, g_ffn, w_router, w_gate, w_up, w_down, g_final):
    batch, seq, d = x.shape
    depth = w_in.shape[0]
    ne = w_router.shape[-1]
    cap = CAPACITY_FACTOR * seq // ne
    assert depth == 1, "the last kernel fuses the MoE residual with the output norm of a single layer"
    l = 0
    x2 = x.reshape(batch * seq, d)
    qa, ka, va, qb, kb, vb = _proj(x2, g_mix[l], w_in[l], seq)
    oa = _attn_a(qa, ka, va, a_sink[l], batch, seq)
    ob = _attn_b(qb, kb, vb, batch, seq)
    x1, h2_rows, aff_t = _mix(x2, oa, ob, g_out_a[l], g_out_b[l], w_out[l], g_ffn[l], w_router[l],
                              batch, seq)
    idx_g, gates = _route(aff_t, cap)
    idx_e = idx_g.transpose(1, 0, 2).reshape(ne, 1, batch * cap)
    y = _experts(idx_e, h2_rows, w_gate[l], w_up[l], w_down[l])
    idx_l = idx_g - (jnp.arange(batch, dtype=jnp.int32) * seq)[:, None, None]
    moe_rows = _combine(idx_l, gates, y, batch, seq)
    out = _final(x1, moe_rows, g_final)
    return out.reshape(batch, seq, d)
```

```python
import functools

import jax
import jax.numpy as jnp
from jax import lax
from jax.experimental import pallas as pl
from jax.experimental.pallas import tpu as pltpu

F32 = jnp.float32
BF16 = jnp.bfloat16

LANES = 128
SUBLANES = 8
BF16_ROWS = 16
VMEM_LIMIT = 56 * 1024 * 1024

HEAD_DIM = 64
A_HEADS = 8
A_KV_HEADS = 2
A_GROUP = A_HEADS // A_KV_HEADS
A_HALF_WINDOW = 128
B_HEADS = 8
DILATED_PATTERNS = ((128, 1), (512, 4), (2048, 16))
A_Q = A_HEADS * HEAD_DIM
A_KV = A_KV_HEADS * HEAD_DIM
B_W = B_HEADS * HEAD_DIM
ROPE_THETA = 500000.0
ROT_DIM = HEAD_DIM // 4
N_EXPERTS = 16
CAPACITY_FACTOR = 2
NORM_EPS = 1e-6
NEG_INF = -1e30

PROJ_TM = 512
ATT_QB = 128
BAND_SLOTS = 4
MIX_TM = 512
FFN_TF = 512
FFN_TM = 512
FINAL_TM = 512
RMW_UNROLL = 8

A_HEAD_ORDER = tuple(h for j in range(A_GROUP) for h in (j, A_GROUP + j))


def _cparams(sem, vmem=None):
    return pltpu.CompilerParams(dimension_semantics=sem, vmem_limit_bytes=vmem)


def _rope_tables(seq_len):
    half = ROT_DIM // 2
    inv_freq = ROPE_THETA ** (-jnp.arange(0, ROT_DIM, 2, dtype=F32) / ROT_DIM)
    ang = jnp.arange(seq_len, dtype=F32)[:, None] * inv_freq[None, :]
    cos, sin = jnp.cos(ang), jnp.sin(ang)
    ones = jnp.ones((seq_len, HEAD_DIM - ROT_DIM), F32)
    zeros = jnp.zeros((seq_len, HEAD_DIM - ROT_DIM), F32)
    zh = jnp.zeros((seq_len, half), F32)
    cos_t = jnp.concatenate([cos, cos, ones], axis=1)
    sin_a = jnp.concatenate([-sin, zh, zeros], axis=1)
    sin_b = jnp.concatenate([zh, sin, zeros], axis=1)
    rep = LANES // HEAD_DIM
    return tuple(jnp.tile(t, (1, rep)) for t in (cos_t, sin_a, sin_b))


def _rms(x, g):
    return x * lax.rsqrt(jnp.mean(x * x, axis=-1, keepdims=True) + NORM_EPS) * g


def _proj_kernel(x_ref, g_ref, w_ref, cos_ref, sa_ref, sb_ref,
                 qa_ref, ka_ref, va_ref, qb_ref, kb_ref, vb_ref):
    h = _rms(x_ref[...], g_ref[...]).astype(BF16)
    cos, sa, sb = cos_ref[...], sa_ref[...], sb_ref[...]
    half = ROT_DIM // 2

    def rope(t):
        return t * cos + pltpu.roll(t, LANES - half, 1) * sa + pltpu.roll(t, half, 1) * sb

    def section(col0, width):
        return jnp.dot(h, w_ref[:, col0:col0 + width], preferred_element_type=F32)

    scale = HEAD_DIM ** -0.5
    qa = section(0, A_Q)
    for c in range(A_Q // LANES):
        qa_ref[:, c * LANES:(c + 1) * LANES] = rope(qa[:, c * LANES:(c + 1) * LANES]) * scale
    kv = section(A_Q, 2 * A_KV)
    ka_ref[...] = rope(kv[:, :A_KV])
    va_ref[...] = kv[:, A_KV:]
    qb = section(A_Q + 2 * A_KV, B_W)
    for c in range(B_W // LANES):
        qb_ref[:, c * LANES:(c + 1) * LANES] = rope(qb[:, c * LANES:(c + 1) * LANES]) * scale
    kb = section(A_Q + 2 * A_KV + B_W, B_W)
    for c in range(B_W // LANES):
        kb_ref[:, c * LANES:(c + 1) * LANES] = rope(kb[:, c * LANES:(c + 1) * LANES])
    vb_ref[...] = section(A_Q + 2 * A_KV + 2 * B_W, B_W)


def _proj(x2, g_mix, w_in, seq):
    t, d = x2.shape
    tm = PROJ_TM
    assert t % tm == 0 and seq % tm == 0 and A_KV == LANES
    nseq = seq // tm
    tables = _rope_tables(seq)
    row = lambda i: (i, 0)
    const = lambda i: (0, 0)
    tab = lambda i: (i % nseq, 0)
    widths = (A_Q, A_KV, A_KV, B_W, B_W, B_W)
    return pl.pallas_call(
        _proj_kernel,
        grid=(t // tm,),
        in_specs=[pl.BlockSpec((tm, d), row), pl.BlockSpec((1, d), const),
                  pl.BlockSpec(w_in.shape, const)] + [pl.BlockSpec((tm, LANES), tab)] * 3,
        out_specs=[pl.BlockSpec((tm, w), row) for w in widths],
        out_shape=[jax.ShapeDtypeStruct((t, w), F32) for w in widths],
        compiler_params=_cparams(("parallel",), VMEM_LIMIT),
        name="proj",
    )(x2, g_mix.reshape(1, d), w_in.astype(BF16), *tables)


def _band_bias(bias_ref, qb, kw, hw):
    kk = lax.broadcasted_iota(jnp.int32, (kw, 2 * qb), 0)
    qq = lax.broadcasted_iota(jnp.int32, (kw, 2 * qb), 1) % qb
    for t, delta in enumerate((0, -hw, -2 * hw)):
        bias_ref[t] = jnp.where(jnp.abs(kk + delta - qq) <= hw, 0.0, NEG_INF)


def _band_scratch(qb, kw):
    return ([pltpu.VMEM((kw, 2 * qb), F32)] * BAND_SLOTS + [pltpu.VMEM((kw, 2 * qb), BF16)] * BAND_SLOTS
            + [pltpu.VMEM((SUBLANES, 2 * qb), F32)] * BAND_SLOTS)


def _band_pipeline(q_ref, k_ref, v_ref, bias_ref, o_ref, lse_ref, bufs, *, nblocks, geometry, sink_row=None):
    qb = ATT_QB
    s_bufs, p_bufs, ml_bufs = (bufs[i * BAND_SLOTS:(i + 1) * BAND_SLOTS] for i in range(3))
    kw = s_bufs[0].shape[0]
    for s_buf, p_buf, ml_buf in zip(s_bufs, p_bufs, ml_bufs):
        s_buf[...] = jnp.zeros_like(s_buf)
        p_buf[...] = jnp.zeros_like(p_buf)
        ml_buf[...] = jnp.ones_like(ml_buf)

    def rows(block):
        q0, k0, kind = geometry(jnp.clip(block, 0, nblocks - 1))
        return pl.ds(pl.multiple_of(q0, BF16_ROWS), qb), pl.ds(pl.multiple_of(k0, BF16_ROWS), kw), kind

    def scores(block, s_buf):
        q_rows, k_rows, kind = rows(block)
        q = q_ref[q_rows, :]
        lo = lax.broadcasted_iota(jnp.int32, q.shape, 1) < HEAD_DIM
        q2 = jnp.concatenate([jnp.where(lo, q, 0.0), jnp.where(lo, 0.0, q)], axis=0).astype(BF16)
        s_buf[...] = lax.dot_general(k_ref[k_rows, :].astype(BF16), q2, (((1,), (1,)), ((), ())),
                                     preferred_element_type=F32) + bias_ref[kind]

    def softmax(s_buf, p_buf, ml_buf):
        m = jnp.max(s_buf[...], axis=0, keepdims=True)
        if sink_row is not None:
            m = jnp.maximum(m, sink_row)
        p = jnp.exp(s_buf[...] - m)
        l = jnp.sum(p, axis=0, keepdims=True)
        if sink_row is not None:
            l = l + jnp.exp(sink_row - m)
        p_buf[...] = p.astype(BF16)
        ml_buf[0:1, :] = m
        ml_buf[1:2, :] = l

    def output(block, p_buf, ml_buf):
        q_rows, k_rows, _ = rows(block)
        o_t = lax.dot_general(v_ref[k_rows, :].astype(BF16), p_buf[...], (((0,), (0,)), ((), ())),
                              preferred_element_type=F32)
        o_tc = jnp.concatenate([o_t[:HEAD_DIM, :qb], o_t[HEAD_DIM:, qb:]], axis=0)
        top = lax.broadcasted_iota(jnp.int32, (LANES, qb), 0) < HEAD_DIM
        m, l = ml_buf[0:1, :], ml_buf[1:2, :]
        l_t = jnp.where(top, l[:, :qb], l[:, qb:])
        o_ref[q_rows, :] = (o_tc / l_t).T
        if lse_ref is not None:
            m_t = jnp.where(top, m[:, :qb], m[:, qb:])
            lse_ref[q_rows, :] = (m_t + jnp.log(l_t)).T

    def trip(j, carry):
        for slot in range(BAND_SLOTS):
            t = j * BAND_SLOTS + slot
            output(t - 2 * BAND_SLOTS, p_bufs[slot], ml_bufs[slot])
            softmax(s_bufs[slot], p_bufs[slot], ml_bufs[slot])
            scores(t, s_bufs[slot])
        return carry

    assert nblocks % BAND_SLOTS == 0
    lax.fori_loop(0, nblocks // BAND_SLOTS + 2, trip, 0)


def _segment_geometry(block, n, hw, base=0):
    qb = ATT_QB
    nblk = n // qb
    seg, jb = block // nblk, block % nblk
    qs = jb * qb
    ks = jnp.clip(qs - hw, 0, n - (qb + 2 * hw))
    kind = jnp.where(jb == 0, 0, jnp.where(jb == nblk - 1, 2, 1))
    return base + seg * n + qs, base + seg * n + ks, kind


def _attn_a_kernel(sink_ref, q_ref, k_ref, v_ref, o_ref, bias_ref, *bufs, seq):
    j = pl.program_id(1)
    qb, hw = ATT_QB, A_HALF_WINDOW
    _band_bias(bias_ref, qb, qb + 2 * hw, hw)
    col = lax.broadcasted_iota(jnp.int32, (1, 2 * qb), 1)
    sink_row = jnp.where(col < qb, sink_ref[j], sink_ref[A_GROUP + j])
    _band_pipeline(q_ref, k_ref, v_ref, bias_ref, o_ref, None, bufs, nblocks=seq // qb,
                   geometry=functools.partial(_segment_geometry, n=seq, hw=hw), sink_row=sink_row)


def _attn_a(qa, ka, va, sink, batch, seq):
    qb, hw = ATT_QB, A_HALF_WINDOW
    assert seq % qb == 0 and seq >= qb + 2 * hw and A_KV == LANES
    qspec = pl.BlockSpec((seq, LANES), lambda b, j: (b, j))
    kvspec = pl.BlockSpec((seq, LANES), lambda b, j: (b, 0))
    return pl.pallas_call(
        functools.partial(_attn_a_kernel, seq=seq),
        grid=(batch, A_Q // LANES),
        in_specs=[pl.BlockSpec(memory_space=pltpu.SMEM), qspec, kvspec, kvspec],
        out_specs=qspec,
        out_shape=jax.ShapeDtypeStruct(qa.shape, F32),
        scratch_shapes=[pltpu.VMEM((3, qb + 2 * hw, 2 * qb), F32)] + _band_scratch(qb, qb + 2 * hw),
        compiler_params=_cparams(("parallel", "parallel"), VMEM_LIMIT),
        name="attn_a",
    )(sink, qa, ka, va)


def _attn_b_kernel(q_ref, k_ref, v_ref, out_ref, bias_ref, tmp_ref, qs_ref, ks_ref, vs_ref, os_ref, ls_ref,
                   *bufs, seq):
    (w1, d1), (w4, d4), (w16, d16) = DILATED_PATTERNS
    hw = w1 // (2 * d1)
    qb = ATT_QB
    _band_bias(bias_ref, qb, qb + 2 * hw, hw)
    n4, n16 = seq // d4, seq // d16
    step = d16 // d4
    chunk = 512
    for src, dst in ((q_ref, qs_ref), (k_ref, ks_ref), (v_ref, vs_ref)):
        for c in range(seq // chunk):
            dst[c * chunk:(c + 1) * chunk, :] = src[c * chunk:(c + 1) * chunk, :].astype(dst.dtype)
        for r in range(d4):
            tmp_ref[r * n4:(r + 1) * n4, :] = src[pl.ds(r, n4, stride=d4), :]
        for c in range(seq // chunk):
            dst[seq + c * chunk:seq + (c + 1) * chunk, :] = tmp_ref[c * chunk:(c + 1) * chunk, :].astype(dst.dtype)
        for r in range(d16):
            dst[2 * seq + r * n16:2 * seq + (r + 1) * n16, :] = tmp_ref[
                pl.ds((r % d4) * n4 + r // d4, n16, stride=step), :].astype(dst.dtype)

    bps = seq // qb
    lg = [(seq // d // qb).bit_length() - 1 for (_, d) in DILATED_PATTERNS]

    def geometry(block):
        pat, within = block // bps, block % bps
        sh = jnp.where(pat == 0, lg[0], jnp.where(pat == 1, lg[1], lg[2]))
        seg, jb = within >> sh, within & ((1 << sh) - 1)
        n = qb << sh
        qs = jb * qb
        ks = jnp.clip(qs - hw, 0, n - (qb + 2 * hw))
        kind = jnp.where(jb == 0, 0, jnp.where(jb == (1 << sh) - 1, 2, 1))
        base = pat * seq + seg * n
        return base + qs, base + ks, kind

    _band_pipeline(qs_ref, ks_ref, vs_ref, bias_ref, os_ref, ls_ref, bufs,
                   nblocks=len(DILATED_PATTERNS) * bps, geometry=geometry)

    def merge(r, carry):
        rows16 = pl.ds(pl.multiple_of(2 * seq + r * n16, SUBLANES), n16)
        rows4 = pl.ds(seq + (r % d4) * n4 + r // d4, n16, stride=step)
        rows1 = pl.ds(r, n16, stride=d16)
        la, lb, lc = ls_ref[rows1, :], ls_ref[rows4, :], ls_ref[rows16, :]
        top = jnp.maximum(jnp.maximum(la, lb), lc)
        wa, wb, wc = jnp.exp(la - top), jnp.exp(lb - top), jnp.exp(lc - top)
        num = wa * os_ref[rows1, :] + wb * os_ref[rows4, :] + wc * os_ref[rows16, :]
        out_ref[rows1, :] = num / (wa + wb + wc)
        return carry

    lax.fori_loop(0, d16, merge, 0)


def _attn_b(qb_, kb, vb, batch, seq):
    (w1, d1), (w4, d4), (w16, d16) = DILATED_PATTERNS
    hw = w1 // (2 * d1)
    qb = ATT_QB
    assert d1 == 1 and d16 % d4 == 0 and w4 // (2 * d4) == hw and w16 // (2 * d16) == hw
    assert seq % 512 == 0 and seq // d16 >= qb + 2 * hw
    for (_, d) in DILATED_PATTERNS:
        nblk = seq // d // qb
        assert nblk * d * qb == seq and nblk & (nblk - 1) == 0
    spec = pl.BlockSpec((seq, LANES), lambda b, p: (b, p))
    npat = len(DILATED_PATTERNS)
    return pl.pallas_call(
        functools.partial(_attn_b_kernel, seq=seq),
        grid=(batch, B_W // LANES),
        in_specs=[spec] * 3,
        out_specs=spec,
        out_shape=jax.ShapeDtypeStruct(qb_.shape, F32),
        scratch_shapes=[pltpu.VMEM((3, qb + 2 * hw, 2 * qb), F32), pltpu.VMEM((seq, LANES), F32),
                        pltpu.VMEM((npat * seq, LANES), F32), pltpu.VMEM((npat * seq, LANES), BF16),
                        pltpu.VMEM((npat * seq, LANES), BF16), pltpu.VMEM((npat * seq, LANES), F32),
                        pltpu.VMEM((npat * seq, LANES), F32)] + _band_scratch(qb, qb + 2 * hw),
        compiler_params=_cparams(("parallel", "parallel"), VMEM_LIMIT),
        name="attn_b",
    )(qb_, kb, vb)


def _split3(v):
    hi = v.astype(BF16)
    r1 = v - hi.astype(F32)
    mid = r1.astype(BF16)
    lo = (r1 - mid.astype(F32)).astype(BF16)
    return hi, mid, lo


def _mix_kernel(x_ref, oa_ref, ob_ref, ga_ref, gb_ref, wo_ref, gf_ref, wr_ref,
                x1_ref, h2_ref, aff_ref):
    na = _rms(oa_ref[...], ga_ref[...]).astype(BF16)
    nb = _rms(ob_ref[...], gb_ref[...]).astype(BF16)
    x1 = (x_ref[...] + jnp.dot(na, wo_ref[:A_Q, :], preferred_element_type=F32)
          + jnp.dot(nb, wo_ref[A_Q:, :], preferred_element_type=F32))
    x1_ref[...] = x1
    h2 = _rms(x1, gf_ref[...])
    tm = h2.shape[0]
    for c in range(h2.shape[1] // LANES):
        h2_ref[pl.ds(c, tm, stride=SUBLANES), :] = h2[:, c * LANES:(c + 1) * LANES]
    dn = (((1,), (1,)), ((), ()))
    h_hi, h_mid, _ = _split3(h2)
    w_hi, w_mid, _ = _split3(wr_ref[...])
    logits = (lax.dot_general(w_hi, h_hi, dn, preferred_element_type=F32)
              + lax.dot_general(w_mid, h_hi, dn, preferred_element_type=F32)
              + lax.dot_general(w_hi, h_mid, dn, preferred_element_type=F32))
    e = jnp.exp(logits - jnp.max(logits, axis=0, keepdims=True))
    aff_ref[...] = e / jnp.sum(e, axis=0, keepdims=True)


def _mix(x2, oa, ob, g_out_a, g_out_b, w_out, g_ffn, w_router, batch, seq):
    t, d = x2.shape
    tm = MIX_TM
    assert seq % tm == 0 and d % LANES == 0 and d // LANES == SUBLANES
    ns = seq // tm
    row = lambda i: (i, 0)
    const = lambda i: (0, 0)
    ne = w_router.shape[1]
    return pl.pallas_call(
        _mix_kernel,
        grid=(t // tm,),
        in_specs=[pl.BlockSpec((tm, d), row), pl.BlockSpec((tm, A_Q), row), pl.BlockSpec((tm, B_W), row),
                  pl.BlockSpec((1, A_Q), const), pl.BlockSpec((1, B_W), const),
                  pl.BlockSpec(w_out.shape, const), pl.BlockSpec((1, d), const),
                  pl.BlockSpec((ne, d), const)],
        out_specs=[pl.BlockSpec((tm, d), row), pl.BlockSpec((tm * SUBLANES, LANES), row),
                   pl.BlockSpec((None, ne, tm), lambda i: (i // ns, 0, i % ns))],
        out_shape=[jax.ShapeDtypeStruct((t, d), F32),
                   jax.ShapeDtypeStruct((t * SUBLANES, LANES), F32),
                   jax.ShapeDtypeStruct((batch, ne, seq), F32)],
        compiler_params=_cparams(("parallel",), VMEM_LIMIT),
        name="mix",
    )(x2, oa, ob, g_out_a.reshape(1, -1), g_out_b.reshape(1, -1), w_out.astype(BF16),
      g_ffn.reshape(1, d), w_router.T)


def _lane_cumsum(mask_f, tri):
    rows, s = mask_f.shape
    carry = jnp.zeros((rows, 1), F32)
    out = []
    for c in range(s // LANES):
        inc = jnp.dot(mask_f[:, c * LANES:(c + 1) * LANES].astype(BF16), tri,
                      preferred_element_type=F32) + carry
        out.append(inc)
        carry = inc[:, LANES - 1:LANES]
    return jnp.concatenate(out, axis=1)


def _route_kernel(aff_ref, idx_ref, gate_ref, *, cap, seq):
    b = pl.program_id(0)
    aff = aff_ref[...]
    ne = aff.shape[0]

    def search(i, thr):
        cand = thr | (jnp.int32(1) << (30 - i))
        cnt = jnp.sum((aff >= lax.bitcast_convert_type(cand, F32)).astype(F32), axis=1, keepdims=True)
        return jnp.where(cnt >= cap, cand, thr)

    thr = lax.bitcast_convert_type(lax.fori_loop(0, 31, search, jnp.zeros((ne, 1), jnp.int32)), F32)
    gt = aff > thr
    eq = aff == thr
    n_gt = jnp.sum(gt.astype(F32), axis=1, keepdims=True)
    ri = lax.broadcasted_iota(jnp.int32, (LANES, LANES), 0)
    ci = lax.broadcasted_iota(jnp.int32, (LANES, LANES), 1)
    tri = (ri <= ci).astype(BF16)
    eq_f = eq.astype(F32)
    eq_rank = _lane_cumsum(eq_f, tri) - eq_f
    sel = gt | (eq & (eq_rank < cap - n_gt))
    sel_f = sel.astype(F32)
    slot = _lane_cumsum(sel_f, tri) - sel_f
    key_t = jnp.where(sel, slot, -1.0).T
    tok = lax.broadcasted_iota(jnp.int32, (SUBLANES, seq), 1)
    piece = lax.broadcasted_iota(jnp.int32, (SUBLANES, seq), 0)
    tok_rows = jnp.where(piece == 0, tok // 64, jnp.where(piece == 1, tok % 64, 0)).astype(F32)
    p_iota = lax.broadcasted_iota(jnp.int32, (1, cap), 1).astype(F32)
    for e in range(ne):
        onehot = (key_t[:, e:e + 1] == p_iota).astype(BF16)
        g = aff[e:e + 1, :]
        g_hi = g.astype(BF16).astype(F32)
        g_mid = (g - g_hi).astype(BF16).astype(F32)
        g_lo = g - g_hi - g_mid
        lhs = jnp.where(piece == 2, g_hi, jnp.where(piece == 3, g_mid, jnp.where(piece == 4, g_lo, tok_rows)))
        res = jnp.dot(lhs.astype(BF16), onehot, preferred_element_type=F32)
        idx_ref[e:e + 1, :] = (res[0:1] * 64.0 + res[1:2]).astype(jnp.int32) + b * seq
        gate_ref[e:e + 1, :] = res[2:3] + res[3:4] + res[4:5]


def _route(aff_t, cap):
    batch, ne, seq = aff_t.shape
    assert seq % LANES == 0 and seq <= 64 * 256
    return pl.pallas_call(
        functools.partial(_route_kernel, cap=cap, seq=seq),
        grid=(batch,),
        in_specs=[pl.BlockSpec((None, ne, seq), lambda b: (b, 0, 0))],
        out_specs=[pl.BlockSpec((None, ne, cap), lambda b: (b, 0, 0))] * 2,
        out_shape=[jax.ShapeDtypeStruct((batch, ne, cap), jnp.int32),
                   jax.ShapeDtypeStruct((batch, ne, cap), F32)],
        compiler_params=_cparams(("parallel",), VMEM_LIMIT),
        name="route",
    )(aff_t)


def _experts_kernel(idx_ref, h_hbm, wg_ref, wu_ref, wd_ref, y_ref, rows_ref, xe_ref, sem):
    f = pl.program_id(1)
    ntok, d = xe_ref.shape

    @pl.when(f == 0)
    def _gather():
        def issue(i, carry):
            src = pl.multiple_of(idx_ref[0, 0, i] * SUBLANES, SUBLANES)
            dst = pl.multiple_of(i * SUBLANES, SUBLANES)
            pltpu.make_async_copy(h_hbm.at[pl.ds(src, SUBLANES), :],
                                  rows_ref.at[pl.ds(dst, SUBLANES), :], sem).start()
            return carry

        lax.fori_loop(0, ntok, issue, 0, unroll=8)
        pltpu.make_async_copy(h_hbm.at[pl.ds(0, ntok * SUBLANES), :], rows_ref, sem).wait()
        for m in range(ntok // FFN_TM):
            xe_ref[m * FFN_TM:(m + 1) * FFN_TM, :] = jnp.concatenate(
                [rows_ref[pl.ds(m * FFN_TM * SUBLANES + c, FFN_TM, stride=SUBLANES), :]
                 for c in range(d // LANES)], axis=1).astype(BF16)
        y_ref[...] = jnp.zeros_like(y_ref)

    wg = wg_ref[...].astype(BF16)
    wu = wu_ref[...].astype(BF16)
    wd = wd_ref[...].astype(BF16)
    for m in range(ntok // FFN_TM):
        xe = xe_ref[m * FFN_TM:(m + 1) * FFN_TM, :]
        a = jnp.dot(xe, wg, preferred_element_type=F32)
        u = jnp.dot(xe, wu, preferred_element_type=F32)
        hmid = (a * jax.nn.sigmoid(a) * u).astype(BF16)
        y_ref[m * FFN_TM:(m + 1) * FFN_TM, :] += jnp.dot(hmid, wd, preferred_element_type=F32)


def _experts(idx_e, h2_rows, w_gate, w_up, w_down):
    ne, d, ff = w_gate.shape
    ntok = idx_e.shape[-1]
    tf = FFN_TF
    assert ff % tf == 0 and ntok % FFN_TM == 0 and d // LANES == SUBLANES
    return pl.pallas_call(
        _experts_kernel,
        grid=(ne, ff // tf),
        in_specs=[pl.BlockSpec((1, 1, ntok), lambda e, f: (e, 0, 0), memory_space=pltpu.SMEM),
                  pl.BlockSpec(memory_space=pl.ANY),
                  pl.BlockSpec((None, d, tf), lambda e, f: (e, 0, f)),
                  pl.BlockSpec((None, d, tf), lambda e, f: (e, 0, f)),
                  pl.BlockSpec((None, tf, d), lambda e, f: (e, f, 0))],
        out_specs=pl.BlockSpec((None, ntok, d), lambda e, f: (e, 0, 0)),
        out_shape=jax.ShapeDtypeStruct((ne, ntok, d), F32),
        scratch_shapes=[pltpu.VMEM((ntok * SUBLANES, LANES), F32), pltpu.VMEM((ntok, d), BF16),
                        pltpu.SemaphoreType.DMA(())],
        compiler_params=_cparams(("arbitrary", "arbitrary"), VMEM_LIMIT),
        name="experts",
    )(idx_e, h2_rows, w_gate, w_up, w_down)


def _combine_kernel(idx_ref, gate_ref, y_ref, acc_ref, ybuf_ref):
    e = pl.program_id(1)
    cap, d = y_ref.shape

    @pl.when(e == 0)
    def _init():
        acc_ref[...] = jnp.zeros_like(acc_ref)

    for c in range(d // LANES):
        ybuf_ref[pl.ds(c, cap, stride=SUBLANES), :] = y_ref[:, c * LANES:(c + 1) * LANES]

    def group(j, carry):
        rows, sums = [], []
        for u in range(RMW_UNROLL):
            i = j * RMW_UNROLL + u
            t = pl.ds(pl.multiple_of(idx_ref[0, 0, i] * SUBLANES, SUBLANES), SUBLANES)
            yi = ybuf_ref[pl.ds(pl.multiple_of(i * SUBLANES, SUBLANES), SUBLANES), :]
            rows.append(t)
            sums.append(acc_ref[t, :] + gate_ref[0, 0, i] * yi)
        for t, s in zip(rows, sums):
            acc_ref[t, :] = s
        return carry

    lax.fori_loop(0, cap // RMW_UNROLL, group, 0)


def _combine(idx_l, gates, y, batch, seq):
    ne, ntok, d = y.shape
    cap = ntok // batch
    assert cap % RMW_UNROLL == 0 and d // LANES == SUBLANES
    smem = pl.BlockSpec((1, 1, cap), lambda b, e: (b * ne + e, 0, 0), memory_space=pltpu.SMEM)
    return pl.pallas_call(
        _combine_kernel,
        grid=(batch, ne),
        in_specs=[smem, smem, pl.BlockSpec((None, cap, d), lambda b, e: (e, b, 0))],
        out_specs=pl.BlockSpec((seq * SUBLANES, LANES), lambda b, e: (b, 0)),
        out_shape=jax.ShapeDtypeStruct((batch * seq * SUBLANES, LANES), F32),
        scratch_shapes=[pltpu.VMEM((cap * SUBLANES, LANES), F32)],
        compiler_params=_cparams(("parallel", "arbitrary"), VMEM_LIMIT),
        name="combine",
    )(idx_l.reshape(batch * ne, 1, cap), gates.reshape(batch * ne, 1, cap), y)


def _final_kernel(x1_ref, moe_ref, g_ref, o_ref):
    tm, d = x1_ref.shape
    moe = jnp.concatenate([moe_ref[pl.ds(c, tm, stride=SUBLANES), :] for c in range(d // LANES)], axis=1)
    o_ref[...] = _rms(x1_ref[...] + moe, g_ref[...])


def _final(x1, moe_rows, g_final):
    t, d = x1.shape
    tm = FINAL_TM
    row = lambda i: (i, 0)
    return pl.pallas_call(
        _final_kernel,
        grid=(t // tm,),
        in_specs=[pl.BlockSpec((tm, d), row), pl.BlockSpec((tm * SUBLANES, LANES), row),
                  pl.BlockSpec((1, d), lambda i: (0, 0))],
        out_specs=pl.BlockSpec((tm, d), row),
        out_shape=jax.ShapeDtypeStruct((t, d), F32),
        compiler_params=_cparams(("parallel",), VMEM_LIMIT),
        name="final",
    )(x1, moe_rows, g_final.reshape(1, d))


def _permute_a_heads(w_in, g_out_a, w_out):
    cols = jnp.asarray([h * HEAD_DIM + i for h in A_HEAD_ORDER for i in range(HEAD_DIM)], jnp.int32)
    w_in = jnp.concatenate([w_in[:, :A_Q][:, cols], w_in[:, A_Q:]], axis=1)
    w_out = jnp.concatenate([w_out[:A_Q][cols], w_out[A_Q:]], axis=0)
    return w_in, g_out_a[cols], w_out


def kernel(x, g_mix, w_in, a_sink, g_out_a, g_out_b, w_out, g_ffn, w_router, w_gate, w_up, w_down, g_final):
    batch, seq, d = x.shape
    depth = w_in.shape[0]
    ne = w_router.shape[-1]
    cap = CAPACITY_FACTOR * seq // ne
    assert depth == 1, "the last kernel fuses the MoE residual with the output norm of a single layer"
    l = 0
    x2 = x.reshape(batch * seq, d)
    w_in_p, g_out_a_p, w_out_p = _permute_a_heads(w_in[l], g_out_a[l], w_out[l])
    qa, ka, va, qb, kb, vb = _proj(x2, g_mix[l], w_in_p, seq)
    oa = _attn_a(qa, ka, va, a_sink[l], batch, seq)
    ob = _attn_b(qb, kb, vb, batch, seq)
    x1, h2_rows, aff_t = _mix(x2, oa, ob, g_out_a_p, g_out_b[l], w_out_p, g_ffn[l], w_router[l], batch, seq)
    idx_g, gates = _route(aff_t, cap)
    idx_e = idx_g.transpose(1, 0, 2).reshape(ne, 1, batch * cap)
    y = _experts(idx_e, h2_rows, w_gate[l], w_up[l], w_down[l])
    idx_l = idx_g - (jnp.arange(batch, dtype=jnp.int32) * seq)[:, None, None]
    moe_rows = _combine(idx_l, gates, y, batch, seq)
    out = _final(x1, moe_rows, g_final)
    return out.reshape(batch, seq, d)
```

```python
import functools

import jax
import jax.numpy as jnp
from jax import lax
from jax.experimental import pallas as pl
from jax.experimental.pallas import tpu as pltpu

F32 = jnp.float32
BF16 = jnp.bfloat16

LANES = 128
SUBLANES = 8
BF16_ROWS = 16
VMEM_LIMIT = 56 * 1024 * 1024

HEAD_DIM = 64
A_HEADS = 8
A_KV_HEADS = 2
A_GROUP = A_HEADS // A_KV_HEADS
A_HALF_WINDOW = 128
B_HEADS = 8
DILATED_PATTERNS = ((128, 1), (512, 4), (2048, 16))
A_Q = A_HEADS * HEAD_DIM
A_KV = A_KV_HEADS * HEAD_DIM
B_W = B_HEADS * HEAD_DIM
ROPE_THETA = 500000.0
ROT_DIM = HEAD_DIM // 4
N_EXPERTS = 16
CAPACITY_FACTOR = 2
NORM_EPS = 1e-6
NEG_INF = -1e30

PROJ_TM = 512
ATT_QB = 128
BAND_SLOTS = 4
MIX_TM = 512
FFN_TF = 512
FFN_TM = 512
FINAL_TM = 512
RMW_UNROLL = 8

A_HEAD_ORDER = tuple(h for j in range(A_GROUP) for h in (j, A_GROUP + j))


def _cparams(sem, vmem=None):
    return pltpu.CompilerParams(dimension_semantics=sem, vmem_limit_bytes=vmem)


def _rope_tables(seq_len):
    half = ROT_DIM // 2
    inv_freq = ROPE_THETA ** (-jnp.arange(0, ROT_DIM, 2, dtype=F32) / ROT_DIM)
    ang = jnp.arange(seq_len, dtype=F32)[:, None] * inv_freq[None, :]
    cos, sin = jnp.cos(ang), jnp.sin(ang)
    ones = jnp.ones((seq_len, HEAD_DIM - ROT_DIM), F32)
    zeros = jnp.zeros((seq_len, HEAD_DIM - ROT_DIM), F32)
    zh = jnp.zeros((seq_len, half), F32)
    cos_t = jnp.concatenate([cos, cos, ones], axis=1)
    sin_a = jnp.concatenate([-sin, zh, zeros], axis=1)
    sin_b = jnp.concatenate([zh, sin, zeros], axis=1)
    rep = LANES // HEAD_DIM
    return tuple(jnp.tile(t, (1, rep)) for t in (cos_t, sin_a, sin_b))


def _rms(x, g):
    return x * lax.rsqrt(jnp.mean(x * x, axis=-1, keepdims=True) + NORM_EPS) * g


def _proj_kernel(x_ref, g_ref, w_ref, cos_ref, sa_ref, sb_ref,
                 qa_ref, ka_ref, va_ref, qb_ref, kb_ref, vb_ref):
    h = _rms(x_ref[...], g_ref[...]).astype(BF16)
    cos, sa, sb = cos_ref[...], sa_ref[...], sb_ref[...]
    half = ROT_DIM // 2

    def rope(t):
        return t * cos + pltpu.roll(t, LANES - half, 1) * sa + pltpu.roll(t, half, 1) * sb

    def section(col0, width):
        return jnp.dot(h, w_ref[:, col0:col0 + width], preferred_element_type=F32)

    scale = HEAD_DIM ** -0.5
    qa = section(0, A_Q)
    for c in range(A_Q // LANES):
        qa_ref[:, c * LANES:(c + 1) * LANES] = rope(qa[:, c * LANES:(c + 1) * LANES]) * scale
    kv = section(A_Q, 2 * A_KV)
    ka_ref[...] = rope(kv[:, :A_KV])
    va_ref[...] = kv[:, A_KV:]
    qb = section(A_Q + 2 * A_KV, B_W)
    for c in range(B_W // LANES):
        qb_ref[:, c * LANES:(c + 1) * LANES] = rope(qb[:, c * LANES:(c + 1) * LANES]) * scale
    kb = section(A_Q + 2 * A_KV + B_W, B_W)
    for c in range(B_W // LANES):
        kb_ref[:, c * LANES:(c + 1) * LANES] = rope(kb[:, c * LANES:(c + 1) * LANES])
    vb_ref[...] = section(A_Q + 2 * A_KV + 2 * B_W, B_W)


def _proj(x2, g_mix, w_in, seq):
    t, d = x2.shape
    tm = PROJ_TM
    assert t % tm == 0 and seq % tm == 0 and A_KV == LANES
    nseq = seq // tm
    tables = _rope_tables(seq)
    row = lambda i: (i, 0)
    const = lambda i: (0, 0)
    tab = lambda i: (i % nseq, 0)
    widths = (A_Q, A_KV, A_KV, B_W, B_W, B_W)
    return pl.pallas_call(
        _proj_kernel,
        grid=(t // tm,),
        in_specs=[pl.BlockSpec((tm, d), row), pl.BlockSpec((1, d), const),
                  pl.BlockSpec(w_in.shape, const)] + [pl.BlockSpec((tm, LANES), tab)] * 3,
        out_specs=[pl.BlockSpec((tm, w), row) for w in widths],
        out_shape=[jax.ShapeDtypeStruct((t, w), F32) for w in widths],
        compiler_params=_cparams(("parallel",), VMEM_LIMIT),
        name="proj",
    )(x2, g_mix.reshape(1, d), w_in.astype(BF16), *tables)


def _band_bias(bias_ref, qb, kw, hw):
    kk = lax.broadcasted_iota(jnp.int32, (kw, 2 * qb), 0)
    qq = lax.broadcasted_iota(jnp.int32, (kw, 2 * qb), 1) % qb
    for t, delta in enumerate((0, -hw, -2 * hw)):
        bias_ref[t] = jnp.where(jnp.abs(kk + delta - qq) <= hw, 0.0, NEG_INF)


def _band_scratch(qb, kw):
    return ([pltpu.VMEM((kw, 2 * qb), F32)] * BAND_SLOTS + [pltpu.VMEM((kw, 2 * qb), BF16)] * BAND_SLOTS
            + [pltpu.VMEM((SUBLANES, 2 * qb), F32)] * BAND_SLOTS)


def _band_pipeline(q_ref, k_ref, v_ref, bias_ref, o_ref, lse_ref, bufs, *, nblocks, geometry, sink_row=None):
    qb = ATT_QB
    s_bufs, p_bufs, ml_bufs = (bufs[i * BAND_SLOTS:(i + 1) * BAND_SLOTS] for i in range(3))
    kw = s_bufs[0].shape[0]
    for s_buf, p_buf, ml_buf in zip(s_bufs, p_bufs, ml_bufs):
        s_buf[...] = jnp.zeros_like(s_buf)
        p_buf[...] = jnp.zeros_like(p_buf)
        ml_buf[...] = jnp.ones_like(ml_buf)

    def rows(block):
        q0, k0, kind = geometry(jnp.clip(block, 0, nblocks - 1))
        return pl.ds(pl.multiple_of(q0, BF16_ROWS), qb), pl.ds(pl.multiple_of(k0, BF16_ROWS), kw), kind

    def scores(block, s_buf):
        q_rows, k_rows, kind = rows(block)
        q = q_ref[q_rows, :]
        lo = lax.broadcasted_iota(jnp.int32, q.shape, 1) < HEAD_DIM
        q2 = jnp.concatenate([jnp.where(lo, q, 0.0), jnp.where(lo, 0.0, q)], axis=0).astype(BF16)
        s_buf[...] = lax.dot_general(k_ref[k_rows, :].astype(BF16), q2, (((1,), (1,)), ((), ())),
                                     preferred_element_type=F32) + bias_ref[kind]

    def softmax(s_buf, p_buf, ml_buf):
        m = jnp.max(s_buf[...], axis=0, keepdims=True)
        if sink_row is not None:
            m = jnp.maximum(m, sink_row)
        p = jnp.exp(s_buf[...] - m)
        l = jnp.sum(p, axis=0, keepdims=True)
        if sink_row is not None:
            l = l + jnp.exp(sink_row - m)
        p_buf[...] = p.astype(BF16)
        ml_buf[0:1, :] = m
        ml_buf[1:2, :] = l

    def output(block, p_buf, ml_buf):
        q_rows, k_rows, _ = rows(block)
        o_t = lax.dot_general(v_ref[k_rows, :].astype(BF16), p_buf[...], (((0,), (0,)), ((), ())),
                              preferred_element_type=F32)
        o_tc = jnp.concatenate([o_t[:HEAD_DIM, :qb], o_t[HEAD_DIM:, qb:]], axis=0)
        top = lax.broadcasted_iota(jnp.int32, (LANES, qb), 0) < HEAD_DIM
        m, l = ml_buf[0:1, :], ml_buf[1:2, :]
        l_t = jnp.where(top, l[:, :qb], l[:, qb:])
        o_ref[q_rows, :] = (o_tc / l_t).T
        if lse_ref is not None:
            m_t = jnp.where(top, m[:, :qb], m[:, qb:])
            lse_ref[q_rows, :] = (m_t + jnp.log(l_t)).T

    def trip(j, carry):
        for slot in range(BAND_SLOTS):
            t = j * BAND_SLOTS + slot
            output(t - 2 * BAND_SLOTS, p_bufs[slot], ml_bufs[slot])
            softmax(s_bufs[slot], p_bufs[slot], ml_bufs[slot])
            scores(t, s_bufs[slot])
        return carry

    assert nblocks % BAND_SLOTS == 0
    lax.fori_loop(0, nblocks // BAND_SLOTS + 2, trip, 0)


def _segment_geometry(block, n, hw, base=0):
    qb = ATT_QB
    nblk = n // qb
    seg, jb = block // nblk, block % nblk
    qs = jb * qb
    ks = jnp.clip(qs - hw, 0, n - (qb + 2 * hw))
    kind = jnp.where(jb == 0, 0, jnp.where(jb == nblk - 1, 2, 1))
    return base + seg * n + qs, base + seg * n + ks, kind


def _attn_a_kernel(sink_ref, q_ref, k_ref, v_ref, o_ref, bias_ref, *bufs, seq):
    j = pl.program_id(1)
    qb, hw = ATT_QB, A_HALF_WINDOW
    _band_bias(bias_ref, qb, qb + 2 * hw, hw)
    col = lax.broadcasted_iota(jnp.int32, (1, 2 * qb), 1)
    sink_row = jnp.where(col < qb, sink_ref[j], sink_ref[A_GROUP + j])
    _band_pipeline(q_ref, k_ref, v_ref, bias_ref, o_ref, None, bufs, nblocks=seq // qb,
                   geometry=functools.partial(_segment_geometry, n=seq, hw=hw), sink_row=sink_row)


def _attn_a(qa, ka, va, sink, batch, seq):
    qb, hw = ATT_QB, A_HALF_WINDOW
    assert seq % qb == 0 and seq >= qb + 2 * hw and A_KV == LANES
    qspec = pl.BlockSpec((seq, LANES), lambda b, j: (b, j))
    kvspec = pl.BlockSpec((seq, LANES), lambda b, j: (b, 0))
    return pl.pallas_call(
        functools.partial(_attn_a_kernel, seq=seq),
        grid=(batch, A_Q // LANES),
        in_specs=[pl.BlockSpec(memory_space=pltpu.SMEM), qspec, kvspec, kvspec],
        out_specs=qspec,
        out_shape=jax.ShapeDtypeStruct(qa.shape, F32),
        scratch_shapes=[pltpu.VMEM((3, qb + 2 * hw, 2 * qb), F32)] + _band_scratch(qb, qb + 2 * hw),
        compiler_params=_cparams(("parallel", "parallel"), VMEM_LIMIT),
        name="attn_a",
    )(sink, qa, ka, va)


def _attn_b_kernel(q_ref, k_ref, v_ref, out_ref, bias_ref, tmp_ref, qs_ref, ks_ref, vs_ref, os_ref, ls_ref,
                   *bufs, seq):
    (w1, d1), (w4, d4), (w16, d16) = DILATED_PATTERNS
    hw = w1 // (2 * d1)
    qb = ATT_QB
    _band_bias(bias_ref, qb, qb + 2 * hw, hw)
    n4, n16 = seq // d4, seq // d16
    step = d16 // d4
    chunk = 512
    for src, dst in ((q_ref, qs_ref), (k_ref, ks_ref), (v_ref, vs_ref)):
        for c in range(seq // chunk):
            dst[c * chunk:(c + 1) * chunk, :] = src[c * chunk:(c + 1) * chunk, :].astype(dst.dtype)
        for r in range(d4):
            tmp_ref[r * n4:(r + 1) * n4, :] = src[pl.ds(r, n4, stride=d4), :]
        for c in range(seq // chunk):
            dst[seq + c * chunk:seq + (c + 1) * chunk, :] = tmp_ref[c * chunk:(c + 1) * chunk, :].astype(dst.dtype)
        for r in range(d16):
            dst[2 * seq + r * n16:2 * seq + (r + 1) * n16, :] = tmp_ref[
                pl.ds((r % d4) * n4 + r // d4, n16, stride=step), :].astype(dst.dtype)

    bps = seq // qb
    lg = [(seq // d // qb).bit_length() - 1 for (_, d) in DILATED_PATTERNS]

    def geometry(block):
        pat, within = block // bps, block % bps
        sh = jnp.where(pat == 0, lg[0], jnp.where(pat == 1, lg[1], lg[2]))
        seg, jb = within >> sh, within & ((1 << sh) - 1)
        n = qb << sh
        qs = jb * qb
        ks = jnp.clip(qs - hw, 0, n - (qb + 2 * hw))
        kind = jnp.where(jb == 0, 0, jnp.where(jb == (1 << sh) - 1, 2, 1))
        base = pat * seq + seg * n
        return base + qs, base + ks, kind

    _band_pipeline(qs_ref, ks_ref, vs_ref, bias_ref, os_ref, ls_ref, bufs,
                   nblocks=len(DILATED_PATTERNS) * bps, geometry=geometry)

    def merge(r, carry):
        rows16 = pl.ds(pl.multiple_of(2 * seq + r * n16, SUBLANES), n16)
        rows4 = pl.ds(seq + (r % d4) * n4 + r // d4, n16, stride=step)
        rows1 = pl.ds(r, n16, stride=d16)
        la, lb, lc = ls_ref[rows1, :], ls_ref[rows4, :], ls_ref[rows16, :]
        top = jnp.maximum(jnp.maximum(la, lb), lc)
        wa, wb, wc = jnp.exp(la - top), jnp.exp(lb - top), jnp.exp(lc - top)
        num = wa * os_ref[rows1, :] + wb * os_ref[rows4, :] + wc * os_ref[rows16, :]
        out_ref[rows1, :] = num / (wa + wb + wc)
        return carry

    lax.fori_loop(0, d16, merge, 0)


def _attn_b(qb_, kb, vb, batch, seq):
    (w1, d1), (w4, d4), (w16, d16) = DILATED_PATTERNS
    hw = w1 // (2 * d1)
    qb = ATT_QB
    assert d1 == 1 and d16 % d4 == 0 and w4 // (2 * d4) == hw and w16 // (2 * d16) == hw
    assert seq % 512 == 0 and seq // d16 >= qb + 2 * hw
    for (_, d) in DILATED_PATTERNS:
        nblk = seq // d // qb
        assert nblk * d * qb == seq and nblk & (nblk - 1) == 0
    spec = pl.BlockSpec((seq, LANES), lambda b, p: (b, p))
    npat = len(DILATED_PATTERNS)
    return pl.pallas_call(
        functools.partial(_attn_b_kernel, seq=seq),
        grid=(batch, B_W // LANES),
        in_specs=[spec] * 3,
        out_specs=spec,
        out_shape=jax.ShapeDtypeStruct(qb_.shape, F32),
        scratch_shapes=[pltpu.VMEM((3, qb + 2 * hw, 2 * qb), F32), pltpu.VMEM((seq, LANES), F32),
                        pltpu.VMEM((npat * seq, LANES), F32), pltpu.VMEM((npat * seq, LANES), BF16),
                        pltpu.VMEM((npat * seq, LANES), BF16), pltpu.VMEM((npat * seq, LANES), F32),
                        pltpu.VMEM((npat * seq, LANES), F32)] + _band_scratch(qb, qb + 2 * hw),
        compiler_params=_cparams(("parallel", "parallel"), VMEM_LIMIT),
        name="attn_b",
    )(qb_, kb, vb)


def _split3(v):
    hi = v.astype(BF16)
    r1 = v - hi.astype(F32)
    mid = r1.astype(BF16)
    lo = (r1 - mid.astype(F32)).astype(BF16)
    return hi, mid, lo


def _mix_kernel(x_ref, oa_ref, ob_ref, ga_ref, gb_ref, wo_ref, gf_ref, wr_ref,
                x1_ref, h2_ref, aff_ref):
    na = _rms(oa_ref[...], ga_ref[...]).astype(BF16)
    nb = _rms(ob_ref[...], gb_ref[...]).astype(BF16)
    x1 = (x_ref[...] + jnp.dot(na, wo_ref[:A_Q, :], preferred_element_type=F32)
          + jnp.dot(nb, wo_ref[A_Q:, :], preferred_element_type=F32))
    x1_ref[...] = x1
    h2 = _rms(x1, gf_ref[...])
    tm = h2.shape[0]
    for c in range(h2.shape[1] // LANES):
        h2_ref[pl.ds(c, tm, stride=SUBLANES), :] = h2[:, c * LANES:(c + 1) * LANES]
    dn = (((1,), (1,)), ((), ()))
    h_hi, h_mid, _ = _split3(h2)
    w_hi, w_mid, _ = _split3(wr_ref[...])
    logits = (lax.dot_general(w_hi, h_hi, dn, preferred_element_type=F32)
              + lax.dot_general(w_mid, h_hi, dn, preferred_element_type=F32)
              + lax.dot_general(w_hi, h_mid, dn, preferred_element_type=F32))
    e = jnp.exp(logits - jnp.max(logits, axis=0, keepdims=True))
    aff_ref[...] = e / jnp.sum(e, axis=0, keepdims=True)


def _mix(x2, oa, ob, g_out_a, g_out_b, w_out, g_ffn, w_router, batch, seq):
    t, d = x2.shape
    tm = MIX_TM
    assert seq % tm == 0 and d % LANES == 0 and d // LANES == SUBLANES
    ns = seq // tm
    row = lambda i: (i, 0)
    const = lambda i: (0, 0)
    ne = w_router.shape[1]
    return pl.pallas_call(
        _mix_kernel,
        grid=(t // tm,),
        in_specs=[pl.BlockSpec((tm, d), row), pl.BlockSpec((tm, A_Q), row), pl.BlockSpec((tm, B_W), row),
                  pl.BlockSpec((1, A_Q), const), pl.BlockSpec((1, B_W), const),
                  pl.BlockSpec(w_out.shape, const), pl.BlockSpec((1, d), const),
                  pl.BlockSpec((ne, d), const)],
        out_specs=[pl.BlockSpec((tm, d), row), pl.BlockSpec((tm * SUBLANES, LANES), row),
                   pl.BlockSpec((None, ne, tm), lambda i: (i // ns, 0, i % ns))],
        out_shape=[jax.ShapeDtypeStruct((t, d), F32),
                   jax.ShapeDtypeStruct((t * SUBLANES, LANES), F32),
                   jax.ShapeDtypeStruct((batch, ne, seq), F32)],
        compiler_params=_cparams(("parallel",), VMEM_LIMIT),
        name="mix",
    )(x2, oa, ob, g_out_a.reshape(1, -1), g_out_b.reshape(1, -1), w_out.astype(BF16),
      g_ffn.reshape(1, d), w_router.T)


def _lane_cumsum(mask_f, tri):
    rows, s = mask_f.shape
    carry = jnp.zeros((rows, 1), F32)
    out = []
    for c in range(s // LANES):
        inc = jnp.dot(mask_f[:, c * LANES:(c + 1) * LANES].astype(BF16), tri,
                      preferred_element_type=F32) + carry
        out.append(inc)
        carry = inc[:, LANES - 1:LANES]
    return jnp.concatenate(out, axis=1)


def _route_kernel(aff_ref, idx_ref, gate_ref, *, cap, seq):
    b = pl.program_id(0)
    aff = aff_ref[...]
    ne = aff.shape[0]

    def search(i, thr):
        cand = thr | (jnp.int32(1) << (30 - i))
        cnt = jnp.sum((aff >= lax.bitcast_convert_type(cand, F32)).astype(F32), axis=1, keepdims=True)
        return jnp.where(cnt >= cap, cand, thr)

    thr = lax.bitcast_convert_type(lax.fori_loop(0, 31, search, jnp.zeros((ne, 1), jnp.int32)), F32)
    gt = aff > thr
    eq = aff == thr
    n_gt = jnp.sum(gt.astype(F32), axis=1, keepdims=True)
    ri = lax.broadcasted_iota(jnp.int32, (LANES, LANES), 0)
    ci = lax.broadcasted_iota(jnp.int32, (LANES, LANES), 1)
    tri = (ri <= ci).astype(BF16)
    eq_f = eq.astype(F32)
    eq_rank = _lane_cumsum(eq_f, tri) - eq_f
    sel = gt | (eq & (eq_rank < cap - n_gt))
    sel_f = sel.astype(F32)
    slot = _lane_cumsum(sel_f, tri) - sel_f
    key_t = jnp.where(sel, slot, -1.0).T
    tok = lax.broadcasted_iota(jnp.int32, (SUBLANES, seq), 1)
    piece = lax.broadcasted_iota(jnp.int32, (SUBLANES, seq), 0)
    tok_rows = jnp.where(piece == 0, tok // 64, jnp.where(piece == 1, tok % 64, 0)).astype(F32)
    p_iota = lax.broadcasted_iota(jnp.int32, (1, cap), 1).astype(F32)
    for e in range(ne):
        onehot = (key_t[:, e:e + 1] == p_iota).astype(BF16)
        g = aff[e:e + 1, :]
        g_hi = g.astype(BF16).astype(F32)
        g_mid = (g - g_hi).astype(BF16).astype(F32)
        g_lo = g - g_hi - g_mid
        lhs = jnp.where(piece == 2, g_hi, jnp.where(piece == 3, g_mid, jnp.where(piece == 4, g_lo, tok_rows)))
        res = jnp.dot(lhs.astype(BF16), onehot, preferred_element_type=F32)
        idx_ref[e:e + 1, :] = (res[0:1] * 64.0 + res[1:2]).astype(jnp.int32) + b * seq
        gate_ref[e:e + 1, :] = res[2:3] + res[3:4] + res[4:5]


def _route(aff_t, cap):
    batch, ne, seq = aff_t.shape
    assert seq % LANES == 0 and seq <= 64 * 256
    return pl.pallas_call(
        functools.partial(_route_kernel, cap=cap, seq=seq),
        grid=(batch,),
        in_specs=[pl.BlockSpec((None, ne, seq), lambda b: (b, 0, 0))],
        out_specs=[pl.BlockSpec((None, ne, cap), lambda b: (b, 0, 0))] * 2,
        out_shape=[jax.ShapeDtypeStruct((batch, ne, cap), jnp.int32),
                   jax.ShapeDtypeStruct((batch, ne, cap), F32)],
        compiler_params=_cparams(("parallel",), VMEM_LIMIT),
        name="route",
    )(aff_t)


def _experts_kernel(idx_ref, idx_next_ref, h_hbm, wg_ref, wu_ref, wd_ref, y_ref, rows_ref, xe_ref, sems,
                    *, ne, nf):
    e, f = pl.program_id(0), pl.program_id(1)
    ntok, d = xe_ref.shape
    slot = e % 2
    chunks = ntok // FFN_TM
    per_chunk = ntok // (nf * chunks)

    def row_copy(idx_smem, i, buf):
        src = pl.multiple_of(idx_smem[0, 0, i] * SUBLANES, SUBLANES)
        dst = pl.multiple_of(i * SUBLANES, SUBLANES)
        return pltpu.make_async_copy(h_hbm.at[pl.ds(src, SUBLANES), :],
                                     rows_ref.at[buf, pl.ds(dst, SUBLANES), :], sems.at[buf])

    def wait_rows(buf):
        pltpu.make_async_copy(h_hbm.at[pl.ds(0, ntok * SUBLANES), :], rows_ref.at[buf], sems.at[buf]).wait()

    @pl.when((e == 0) & (f == 0))
    def _first_gather():
        def issue(i, carry):
            row_copy(idx_ref, i, 0).start()
            return carry

        lax.fori_loop(0, ntok, issue, 0, unroll=8)

    @pl.when(f == 0)
    def _repack():
        wait_rows(slot)
        for m in range(chunks):
            xe_ref[m * FFN_TM:(m + 1) * FFN_TM, :] = jnp.concatenate(
                [rows_ref[slot, pl.ds(m * FFN_TM * SUBLANES + c, FFN_TM, stride=SUBLANES), :]
                 for c in range(d // LANES)], axis=1).astype(BF16)
        y_ref[...] = jnp.zeros_like(y_ref)

    wg = wg_ref[...].astype(BF16)
    wu = wu_ref[...].astype(BF16)
    wd = wd_ref[...].astype(BF16)
    for m in range(chunks):
        xe = xe_ref[m * FFN_TM:(m + 1) * FFN_TM, :]
        a = jnp.dot(xe, wg, preferred_element_type=F32)
        u = jnp.dot(xe, wu, preferred_element_type=F32)
        hmid = (a * jax.nn.sigmoid(a) * u).astype(BF16)
        y_ref[m * FFN_TM:(m + 1) * FFN_TM, :] += jnp.dot(hmid, wd, preferred_element_type=F32)
        for i in range(per_chunk):
            row_copy(idx_next_ref, (f * chunks + m) * per_chunk + i, 1 - slot).start()

    @pl.when((e == ne - 1) & (f == nf - 1))
    def _drain():
        wait_rows(1 - slot)


def _experts(idx_e, h2_rows, w_gate, w_up, w_down):
    ne, d, ff = w_gate.shape
    ntok = idx_e.shape[-1]
    tf = FFN_TF
    nf = ff // tf
    assert ff % tf == 0 and ntok % (FFN_TM * nf) == 0 and d // LANES == SUBLANES
    return pl.pallas_call(
        functools.partial(_experts_kernel, ne=ne, nf=nf),
        grid=(ne, nf),
        in_specs=[pl.BlockSpec((1, 1, ntok), lambda e, f: (e, 0, 0), memory_space=pltpu.SMEM),
                  pl.BlockSpec((1, 1, ntok), lambda e, f: (jnp.minimum(e + 1, ne - 1), 0, 0),
                               memory_space=pltpu.SMEM),
                  pl.BlockSpec(memory_space=pl.ANY),
                  pl.BlockSpec((None, d, tf), lambda e, f: (e, 0, f)),
                  pl.BlockSpec((None, d, tf), lambda e, f: (e, 0, f)),
                  pl.BlockSpec((None, tf, d), lambda e, f: (e, f, 0))],
        out_specs=pl.BlockSpec((None, ntok, d), lambda e, f: (e, 0, 0)),
        out_shape=jax.ShapeDtypeStruct((ne, ntok, d), F32),
        scratch_shapes=[pltpu.VMEM((2, ntok * SUBLANES, LANES), F32), pltpu.VMEM((ntok, d), BF16),
                        pltpu.SemaphoreType.DMA((2,))],
        compiler_params=_cparams(("arbitrary", "arbitrary"), VMEM_LIMIT),
        name="experts",
    )(idx_e, idx_e, h2_rows, w_gate, w_up, w_down)


def _combine_kernel(idx_ref, gate_ref, y_ref, x1_ref, g_ref, o_ref, acc_ref, ybuf_ref, *, ne):
    j = pl.program_id(1)
    cap, d = y_ref.shape
    tm = x1_ref.shape[0]

    @pl.when(j == 0)
    def _init():
        acc_ref[...] = jnp.zeros_like(acc_ref)

    @pl.when(j < ne)
    def _scatter():
        for c in range(d // LANES):
            ybuf_ref[pl.ds(c, cap, stride=SUBLANES), :] = y_ref[:, c * LANES:(c + 1) * LANES]

        def group(k, carry):
            rows, sums = [], []
            for u in range(RMW_UNROLL):
                i = k * RMW_UNROLL + u
                t = pl.ds(pl.multiple_of(idx_ref[0, 0, i] * SUBLANES, SUBLANES), SUBLANES)
                yi = ybuf_ref[pl.ds(pl.multiple_of(i * SUBLANES, SUBLANES), SUBLANES), :]
                rows.append(t)
                sums.append(acc_ref[t, :] + gate_ref[0, 0, i] * yi)
            for t, s in zip(rows, sums):
                acc_ref[t, :] = s
            return carry

        lax.fori_loop(0, cap // RMW_UNROLL, group, 0)

    @pl.when(j >= ne)
    def _emit():
        base = pl.multiple_of((j - ne) * tm * SUBLANES, SUBLANES)
        moe = jnp.concatenate([acc_ref[pl.ds(base + c, tm, stride=SUBLANES), :] for c in range(d // LANES)],
                              axis=1)
        o_ref[...] = _rms(x1_ref[...] + moe, g_ref[...])


def _combine(idx_l, gates, y, x1, g_final, batch, seq):
    ne, ntok, d = y.shape
    cap = ntok // batch
    tm = FINAL_TM
    nt = seq // tm
    assert cap % RMW_UNROLL == 0 and d // LANES == SUBLANES and seq % tm == 0
    expert = lambda j: jnp.minimum(j, ne - 1)
    rows = lambda b, j: (b * nt + jnp.maximum(j - ne, 0), 0)
    smem = pl.BlockSpec((1, 1, cap), lambda b, j: (b * ne + expert(j), 0, 0), memory_space=pltpu.SMEM)
    return pl.pallas_call(
        functools.partial(_combine_kernel, ne=ne),
        grid=(batch, ne + nt),
        in_specs=[smem, smem, pl.BlockSpec((None, cap, d), lambda b, j: (expert(j), b, 0)),
                  pl.BlockSpec((tm, d), rows), pl.BlockSpec((1, d), lambda b, j: (0, 0))],
        out_specs=pl.BlockSpec((tm, d), rows),
        out_shape=jax.ShapeDtypeStruct((batch * seq, d), F32),
        scratch_shapes=[pltpu.VMEM((seq * SUBLANES, LANES), F32), pltpu.VMEM((cap * SUBLANES, LANES), F32)],
        compiler_params=_cparams(("parallel", "arbitrary"), VMEM_LIMIT),
        name="combine",
    )(idx_l.reshape(batch * ne, 1, cap), gates.reshape(batch * ne, 1, cap), y, x1, g_final.reshape(1, d))


def _permute_a_heads(w_in, g_out_a, w_out):
    cols = jnp.asarray([h * HEAD_DIM + i for h in A_HEAD_ORDER for i in range(HEAD_DIM)], jnp.int32)
    w_in = jnp.concatenate([w_in[:, :A_Q][:, cols], w_in[:, A_Q:]], axis=1)
    w_out = jnp.concatenate([w_out[:A_Q][cols], w_out[A_Q:]], axis=0)
    return w_in, g_out_a[cols], w_out


def kernel(x, g_mix, w_in, a_sink, g_out_a, g_out_b, w_out, g_ffn, w_router, w_gate, w_up, w_down, g_final):
    batch, seq, d = x.shape
    depth = w_in.shape[0]
    ne = w_router.shape[-1]
    cap = CAPACITY_FACTOR * seq // ne
    assert depth == 1, "the last kernel fuses the MoE residual with the output norm of a single layer"
    l = 0
    x2 = x.reshape(batch * seq, d)
    w_in_p, g_out_a_p, w_out_p = _permute_a_heads(w_in[l], g_out_a[l], w_out[l])
    qa, ka, va, qb, kb, vb = _proj(x2, g_mix[l], w_in_p, seq)
    oa = _attn_a(qa, ka, va, a_sink[l], batch, seq)
    ob = _attn_b(qb, kb, vb, batch, seq)
    x1, h2_rows, aff_t = _mix(x2, oa, ob, g_out_a_p, g_out_b[l], w_out_p, g_ffn[l], w_router[l], batch, seq)
    idx_g, gates = _route(aff_t, cap)
    idx_e = idx_g.transpose(1, 0, 2).reshape(ne, 1, batch * cap)
    y = _experts(idx_e, h2_rows, w_gate[l], w_up[l], w_down[l])
    idx_l = idx_g - (jnp.arange(batch, dtype=jnp.int32) * seq)[:, None, None]
    out = _combine(idx_l, gates, y, x1, g_final, batch, seq)
    return out.reshape(batch, seq, d)
```

```python
import functools

import jax
import jax.numpy as jnp
from jax import lax
from jax.experimental import pallas as pl
from jax.experimental.pallas import tpu as pltpu

F32 = jnp.float32
BF16 = jnp.bfloat16

LANES = 128
SUBLANES = 8
BF16_ROWS = 16
VMEM_LIMIT = 56 * 1024 * 1024

HEAD_DIM = 64
A_HEADS = 8
A_KV_HEADS = 2
A_GROUP = A_HEADS // A_KV_HEADS
A_HALF_WINDOW = 128
B_HEADS = 8
DILATED_PATTERNS = ((128, 1), (512, 4), (2048, 16))
A_Q = A_HEADS * HEAD_DIM
A_KV = A_KV_HEADS * HEAD_DIM
B_W = B_HEADS * HEAD_DIM
ROPE_THETA = 500000.0
ROT_DIM = HEAD_DIM // 4
N_EXPERTS = 16
CAPACITY_FACTOR = 2
NORM_EPS = 1e-6
NEG_INF = -1e30
LOG2_E = 1.4426950408889634

PROJ_TM = 512
ATT_QB = 128
BAND_SLOTS = 4
MIX_TM = 512
FFN_TF = 512
FFN_TM = 512
FINAL_TM = 1024
COMBINE_EXPERTS = 2
RMW_UNROLL = 8

A_HEAD_ORDER = tuple(h for j in range(A_GROUP) for h in (j, A_GROUP + j))


def _cparams(sem, vmem=None):
    return pltpu.CompilerParams(dimension_semantics=sem, vmem_limit_bytes=vmem)


def _rope_tables(seq_len):
    half = ROT_DIM // 2
    inv_freq = ROPE_THETA ** (-jnp.arange(0, ROT_DIM, 2, dtype=F32) / ROT_DIM)
    ang = jnp.arange(seq_len, dtype=F32)[:, None] * inv_freq[None, :]
    cos, sin = jnp.cos(ang), jnp.sin(ang)
    ones = jnp.ones((seq_len, HEAD_DIM - ROT_DIM), F32)
    zeros = jnp.zeros((seq_len, HEAD_DIM - ROT_DIM), F32)
    zh = jnp.zeros((seq_len, half), F32)
    cos_t = jnp.concatenate([cos, cos, ones], axis=1)
    sin_a = jnp.concatenate([-sin, zh, zeros], axis=1)
    sin_b = jnp.concatenate([zh, sin, zeros], axis=1)
    rep = LANES // HEAD_DIM
    return tuple(jnp.tile(t, (1, rep)) for t in (cos_t, sin_a, sin_b))


def _rms(x, g):
    return x * lax.rsqrt(jnp.mean(x * x, axis=-1, keepdims=True) + NORM_EPS) * g


def _proj_kernel(x_ref, g_ref, w_ref, cos_ref, sa_ref, sb_ref,
                 qa_ref, ka_ref, va_ref, qb_ref, kb_ref, vb_ref):
    h = _rms(x_ref[...], g_ref[...]).astype(BF16)
    cos, sa, sb = cos_ref[...], sa_ref[...], sb_ref[...]
    half = ROT_DIM // 2

    def rope(t):
        return t * cos + pltpu.roll(t, LANES - half, 1) * sa + pltpu.roll(t, half, 1) * sb

    def section(col0, width):
        return jnp.dot(h, w_ref[:, col0:col0 + width], preferred_element_type=F32)

    scale = HEAD_DIM ** -0.5 * LOG2_E
    qa = section(0, A_Q)
    for c in range(A_Q // LANES):
        qa_ref[:, c * LANES:(c + 1) * LANES] = rope(qa[:, c * LANES:(c + 1) * LANES]) * scale
    kv = section(A_Q, 2 * A_KV)
    ka_ref[...] = rope(kv[:, :A_KV])
    va_ref[...] = kv[:, A_KV:]
    qb = section(A_Q + 2 * A_KV, B_W)
    for c in range(B_W // LANES):
        qb_ref[:, c * LANES:(c + 1) * LANES] = rope(qb[:, c * LANES:(c + 1) * LANES]) * scale
    kb = section(A_Q + 2 * A_KV + B_W, B_W)
    for c in range(B_W // LANES):
        kb_ref[:, c * LANES:(c + 1) * LANES] = rope(kb[:, c * LANES:(c + 1) * LANES])
    vb_ref[...] = section(A_Q + 2 * A_KV + 2 * B_W, B_W)


def _proj(x2, g_mix, w_in, seq):
    t, d = x2.shape
    tm = PROJ_TM
    assert t % tm == 0 and seq % tm == 0 and A_KV == LANES
    nseq = seq // tm
    tables = _rope_tables(seq)
    row = lambda i: (i, 0)
    const = lambda i: (0, 0)
    tab = lambda i: (i % nseq, 0)
    widths = (A_Q, A_KV, A_KV, B_W, B_W, B_W)
    return pl.pallas_call(
        _proj_kernel,
        grid=(t // tm,),
        in_specs=[pl.BlockSpec((tm, d), row), pl.BlockSpec((1, d), const),
                  pl.BlockSpec(w_in.shape, const)] + [pl.BlockSpec((tm, LANES), tab)] * 3,
        out_specs=[pl.BlockSpec((tm, w), row) for w in widths],
        out_shape=[jax.ShapeDtypeStruct((t, w), F32) for w in widths],
        compiler_params=_cparams(("parallel",), VMEM_LIMIT),
        name="proj",
    )(x2, g_mix.reshape(1, d), w_in.astype(BF16), *tables)


def _band_bias(bias_ref, qb, kw, hw):
    kk = lax.broadcasted_iota(jnp.int32, (kw, 2 * qb), 0)
    qq = lax.broadcasted_iota(jnp.int32, (kw, 2 * qb), 1) % qb
    for t, delta in enumerate((0, -hw, -2 * hw)):
        bias_ref[t] = jnp.where(jnp.abs(kk + delta - qq) <= hw, 0.0, NEG_INF)


def _band_scratch(qb, kw):
    return ([pltpu.VMEM((kw, 2 * qb), F32)] * BAND_SLOTS + [pltpu.VMEM((kw, 2 * qb), BF16)] * BAND_SLOTS
            + [pltpu.VMEM((SUBLANES, 2 * qb), F32)] * BAND_SLOTS + [pltpu.VMEM((LANES, kw), BF16)] * BAND_SLOTS)


def _band_pipeline(q_ref, k_ref, v_ref, bias_ref, o_ref, lse_ref, bufs, *, nblocks, geometry, sink_row=None):
    qb = ATT_QB
    s_bufs, p_bufs, ml_bufs, vt_bufs = (bufs[i * BAND_SLOTS:(i + 1) * BAND_SLOTS] for i in range(4))
    kw = s_bufs[0].shape[0]
    for s_buf, p_buf, ml_buf, vt_buf in zip(s_bufs, p_bufs, ml_bufs, vt_bufs):
        s_buf[...] = jnp.zeros_like(s_buf)
        p_buf[...] = jnp.zeros_like(p_buf)
        ml_buf[...] = jnp.ones_like(ml_buf)
        vt_buf[...] = jnp.zeros_like(vt_buf)

    def rows(block):
        q0, k0, kind = geometry(jnp.clip(block, 0, nblocks - 1))
        return pl.ds(pl.multiple_of(q0, BF16_ROWS), qb), pl.ds(pl.multiple_of(k0, BF16_ROWS), kw), kind

    def scores(block, s_buf):
        q_rows, k_rows, kind = rows(block)
        q = q_ref[q_rows, :]
        lo = lax.broadcasted_iota(jnp.int32, q.shape, 1) < HEAD_DIM
        q2 = jnp.concatenate([jnp.where(lo, q, 0.0), jnp.where(lo, 0.0, q)], axis=0).astype(BF16)
        s_buf[...] = lax.dot_general(k_ref[k_rows, :].astype(BF16), q2, (((1,), (1,)), ((), ())),
                                     preferred_element_type=F32) + bias_ref[kind]

    def softmax(block, s_buf, p_buf, ml_buf, vt_buf):
        _, k_rows, _ = rows(block)
        vt_buf[...] = v_ref[k_rows, :].astype(F32).T.astype(BF16)
        m = jnp.max(s_buf[...], axis=0, keepdims=True)
        if sink_row is not None:
            m = jnp.maximum(m, sink_row)
        p = jnp.exp2(s_buf[...] - m)
        l = jnp.sum(p, axis=0, keepdims=True)
        if sink_row is not None:
            l = l + jnp.exp2(sink_row - m)
        p_buf[...] = p.astype(BF16)
        ml_buf[0:1, :] = m
        ml_buf[1:2, :] = l

    def output(block, p_buf, ml_buf, vt_buf):
        q_rows, _, _ = rows(block)
        o_t = jnp.dot(vt_buf[...], p_buf[...], preferred_element_type=F32)
        o_tc = jnp.concatenate([o_t[:HEAD_DIM, :qb], o_t[HEAD_DIM:, qb:]], axis=0)
        top = lax.broadcasted_iota(jnp.int32, (LANES, qb), 0) < HEAD_DIM
        m, l = ml_buf[0:1, :], ml_buf[1:2, :]
        inv = 1.0 / l
        o_ref[q_rows, :] = (o_tc * jnp.where(top, inv[:, :qb], inv[:, qb:])).T
        if lse_ref is not None:
            lse = m + jnp.log2(l)
            lse_ref[q_rows, :] = jnp.where(top, lse[:, :qb], lse[:, qb:]).T

    def trip(j, carry):
        for slot in range(BAND_SLOTS):
            t = j * BAND_SLOTS + slot
            output(t - 2 * BAND_SLOTS, p_bufs[slot], ml_bufs[slot], vt_bufs[slot])
            softmax(t - BAND_SLOTS, s_bufs[slot], p_bufs[slot], ml_bufs[slot], vt_bufs[slot])
            scores(t, s_bufs[slot])
        return carry

    assert nblocks % BAND_SLOTS == 0
    lax.fori_loop(0, nblocks // BAND_SLOTS + 2, trip, 0)


def _segment_geometry(block, n, hw, base=0):
    qb = ATT_QB
    nblk = n // qb
    seg, jb = block // nblk, block % nblk
    qs = jb * qb
    ks = jnp.clip(qs - hw, 0, n - (qb + 2 * hw))
    kind = jnp.where(jb == 0, 0, jnp.where(jb == nblk - 1, 2, 1))
    return base + seg * n + qs, base + seg * n + ks, kind


def _attn_a_kernel(sink_ref, q_ref, k_ref, v_ref, o_ref, bias_ref, *bufs, seq):
    j = pl.program_id(1)
    qb, hw = ATT_QB, A_HALF_WINDOW
    _band_bias(bias_ref, qb, qb + 2 * hw, hw)
    col = lax.broadcasted_iota(jnp.int32, (1, 2 * qb), 1)
    sink_row = jnp.where(col < qb, sink_ref[j], sink_ref[A_GROUP + j]) * LOG2_E
    _band_pipeline(q_ref, k_ref, v_ref, bias_ref, o_ref, None, bufs, nblocks=seq // qb,
                   geometry=functools.partial(_segment_geometry, n=seq, hw=hw), sink_row=sink_row)


def _attn_a(qa, ka, va, sink, batch, seq):
    qb, hw = ATT_QB, A_HALF_WINDOW
    assert seq % qb == 0 and seq >= qb + 2 * hw and A_KV == LANES
    qspec = pl.BlockSpec((seq, LANES), lambda b, j: (b, j))
    kvspec = pl.BlockSpec((seq, LANES), lambda b, j: (b, 0))
    return pl.pallas_call(
        functools.partial(_attn_a_kernel, seq=seq),
        grid=(batch, A_Q // LANES),
        in_specs=[pl.BlockSpec(memory_space=pltpu.SMEM), qspec, kvspec, kvspec],
        out_specs=qspec,
        out_shape=jax.ShapeDtypeStruct(qa.shape, F32),
        scratch_shapes=[pltpu.VMEM((3, qb + 2 * hw, 2 * qb), F32)] + _band_scratch(qb, qb + 2 * hw),
        compiler_params=_cparams(("parallel", "parallel"), VMEM_LIMIT),
        name="attn_a",
    )(sink, qa, ka, va)


def _attn_b_kernel(q_ref, k_ref, v_ref, out_ref, bias_ref, tmp_ref, qs_ref, ks_ref, vs_ref, os_ref, ls_ref,
                   *bufs, seq):
    (w1, d1), (w4, d4), (w16, d16) = DILATED_PATTERNS
    hw = w1 // (2 * d1)
    qb = ATT_QB
    _band_bias(bias_ref, qb, qb + 2 * hw, hw)
    n4, n16 = seq // d4, seq // d16
    step = d16 // d4
    chunk = 512
    for src, dst in ((q_ref, qs_ref), (k_ref, ks_ref), (v_ref, vs_ref)):
        for c in range(seq // chunk):
            dst[c * chunk:(c + 1) * chunk, :] = src[c * chunk:(c + 1) * chunk, :].astype(dst.dtype)
        for r in range(d4):
            tmp_ref[r * n4:(r + 1) * n4, :] = src[pl.ds(r, n4, stride=d4), :]
        for c in range(seq // chunk):
            dst[seq + c * chunk:seq + (c + 1) * chunk, :] = tmp_ref[c * chunk:(c + 1) * chunk, :].astype(dst.dtype)
        for r in range(d16):
            dst[2 * seq + r * n16:2 * seq + (r + 1) * n16, :] = tmp_ref[
                pl.ds((r % d4) * n4 + r // d4, n16, stride=step), :].astype(dst.dtype)

    bps = seq // qb
    lg = [(seq // d // qb).bit_length() - 1 for (_, d) in DILATED_PATTERNS]

    def geometry(block):
        pat, within = block // bps, block % bps
        sh = jnp.where(pat == 0, lg[0], jnp.where(pat == 1, lg[1], lg[2]))
        seg, jb = within >> sh, within & ((1 << sh) - 1)
        n = qb << sh
        qs = jb * qb
        ks = jnp.clip(qs - hw, 0, n - (qb + 2 * hw))
        kind = jnp.where(jb == 0, 0, jnp.where(jb == (1 << sh) - 1, 2, 1))
        base = pat * seq + seg * n
        return base + qs, base + ks, kind

    _band_pipeline(qs_ref, ks_ref, vs_ref, bias_ref, os_ref, ls_ref, bufs,
                   nblocks=len(DILATED_PATTERNS) * bps, geometry=geometry)

    def merge(r, carry):
        rows16 = pl.ds(pl.multiple_of(2 * seq + r * n16, SUBLANES), n16)
        rows4 = pl.ds(seq + (r % d4) * n4 + r // d4, n16, stride=step)
        rows1 = pl.ds(r, n16, stride=d16)
        la, lb, lc = ls_ref[rows1, :], ls_ref[rows4, :], ls_ref[rows16, :]
        top = jnp.maximum(jnp.maximum(la, lb), lc)
        wa, wb, wc = jnp.exp2(la - top), jnp.exp2(lb - top), jnp.exp2(lc - top)
        num = wa * os_ref[rows1, :] + wb * os_ref[rows4, :] + wc * os_ref[rows16, :]
        out_ref[rows1, :] = num * (1.0 / (wa + wb + wc))
        return carry

    lax.fori_loop(0, d16, merge, 0)


def _attn_b(qb_, kb, vb, batch, seq):
    (w1, d1), (w4, d4), (w16, d16) = DILATED_PATTERNS
    hw = w1 // (2 * d1)
    qb = ATT_QB
    assert d1 == 1 and d16 % d4 == 0 and w4 // (2 * d4) == hw and w16 // (2 * d16) == hw
    assert seq % 512 == 0 and seq // d16 >= qb + 2 * hw
    for (_, d) in DILATED_PATTERNS:
        nblk = seq // d // qb
        assert nblk * d * qb == seq and nblk & (nblk - 1) == 0
    spec = pl.BlockSpec((seq, LANES), lambda b, p: (b, p))
    npat = len(DILATED_PATTERNS)
    return pl.pallas_call(
        functools.partial(_attn_b_kernel, seq=seq),
        grid=(batch, B_W // LANES),
        in_specs=[spec] * 3,
        out_specs=spec,
        out_shape=jax.ShapeDtypeStruct(qb_.shape, F32),
        scratch_shapes=[pltpu.VMEM((3, qb + 2 * hw, 2 * qb), F32), pltpu.VMEM((seq, LANES), F32),
                        pltpu.VMEM((npat * seq, LANES), F32), pltpu.VMEM((npat * seq, LANES), BF16),
                        pltpu.VMEM((npat * seq, LANES), BF16), pltpu.VMEM((npat * seq, LANES), F32),
                        pltpu.VMEM((npat * seq, LANES), F32)] + _band_scratch(qb, qb + 2 * hw),
        compiler_params=_cparams(("parallel", "parallel"), VMEM_LIMIT),
        name="attn_b",
    )(qb_, kb, vb)


def _mix_kernel(x_ref, oa_ref, ob_ref, ga_ref, gb_ref, wo_ref, gf_ref, wr_ref,
                x1_ref, h2_ref, aff_ref):
    na = _rms(oa_ref[...], ga_ref[...]).astype(BF16)
    nb = _rms(ob_ref[...], gb_ref[...]).astype(BF16)
    x1 = (x_ref[...] + jnp.dot(na, wo_ref[:A_Q, :], preferred_element_type=F32)
          + jnp.dot(nb, wo_ref[A_Q:, :], preferred_element_type=F32))
    x1_ref[...] = x1
    h2 = _rms(x1, gf_ref[...])
    tm = h2.shape[0]
    for c in range(h2.shape[1] // LANES):
        h2_ref[pl.ds(c, tm, stride=SUBLANES), :] = h2[:, c * LANES:(c + 1) * LANES]
    ne = aff_ref.shape[0]
    dn = (((1,), (1,)), ((), ()))
    h_hi = h2.astype(BF16)
    h_mid = (h2 - h_hi.astype(F32)).astype(BF16)
    r = lax.dot_general(wr_ref[...], h_hi, dn, preferred_element_type=F32)
    logits = r[:ne] + r[ne:] + lax.dot_general(wr_ref[:ne, :], h_mid, dn, preferred_element_type=F32)
    e = jnp.exp(logits - jnp.max(logits, axis=0, keepdims=True))
    aff_ref[...] = e / jnp.sum(e, axis=0, keepdims=True)


def _mix(x2, oa, ob, g_out_a, g_out_b, w_out, g_ffn, w_router, batch, seq):
    t, d = x2.shape
    tm = MIX_TM
    assert seq % tm == 0 and d % LANES == 0 and d // LANES == SUBLANES
    ns = seq // tm
    row = lambda i: (i, 0)
    const = lambda i: (0, 0)
    ne = w_router.shape[1]
    w_hi = w_router.T.astype(BF16)
    w_mid = (w_router.T - w_hi.astype(F32)).astype(BF16)
    wr = jnp.concatenate([w_hi, w_mid], axis=0)
    return pl.pallas_call(
        _mix_kernel,
        grid=(t // tm,),
        in_specs=[pl.BlockSpec((tm, d), row), pl.BlockSpec((tm, A_Q), row), pl.BlockSpec((tm, B_W), row),
                  pl.BlockSpec((1, A_Q), const), pl.BlockSpec((1, B_W), const),
                  pl.BlockSpec(w_out.shape, const), pl.BlockSpec((1, d), const),
                  pl.BlockSpec((2 * ne, d), const)],
        out_specs=[pl.BlockSpec((tm, d), row), pl.BlockSpec((tm * SUBLANES, LANES), row),
                   pl.BlockSpec((None, ne, tm), lambda i: (i // ns, 0, i % ns))],
        out_shape=[jax.ShapeDtypeStruct((t, d), F32),
                   jax.ShapeDtypeStruct((t * SUBLANES, LANES), F32),
                   jax.ShapeDtypeStruct((batch, ne, seq), F32)],
        compiler_params=_cparams(("parallel",), VMEM_LIMIT),
        name="mix",
    )(x2, oa, ob, g_out_a.reshape(1, -1), g_out_b.reshape(1, -1), w_out.astype(BF16),
      g_ffn.reshape(1, d), wr)


def _lane_cumsum(mask_f, tri):
    rows, s = mask_f.shape
    carry = jnp.zeros((rows, 1), F32)
    out = []
    for c in range(s // LANES):
        inc = jnp.dot(mask_f[:, c * LANES:(c + 1) * LANES].astype(BF16), tri,
                      preferred_element_type=F32) + carry
        out.append(inc)
        carry = inc[:, LANES - 1:LANES]
    return jnp.concatenate(out, axis=1)


def _route_kernel(aff_ref, idx_ref, gate_ref, *, cap, seq):
    b = pl.program_id(0)
    aff = aff_ref[...]
    ne = aff.shape[0]

    def search(i, thr):
        cand = thr | (jnp.int32(1) << (30 - i))
        cnt = jnp.sum((aff >= lax.bitcast_convert_type(cand, F32)).astype(F32), axis=1, keepdims=True)
        return jnp.where(cnt >= cap, cand, thr)

    thr = lax.bitcast_convert_type(lax.fori_loop(0, 31, search, jnp.zeros((ne, 1), jnp.int32)), F32)
    gt = aff > thr
    eq = aff == thr
    n_gt = jnp.sum(gt.astype(F32), axis=1, keepdims=True)
    ri = lax.broadcasted_iota(jnp.int32, (LANES, LANES), 0)
    ci = lax.broadcasted_iota(jnp.int32, (LANES, LANES), 1)
    tri = (ri <= ci).astype(BF16)
    eq_f = eq.astype(F32)
    eq_rank = _lane_cumsum(eq_f, tri) - eq_f
    sel = gt | (eq & (eq_rank < cap - n_gt))
    sel_f = sel.astype(F32)
    slot = _lane_cumsum(sel_f, tri) - sel_f
    key_t = jnp.where(sel, slot, -1.0).T
    tok = lax.broadcasted_iota(jnp.int32, (SUBLANES, seq), 1)
    piece = lax.broadcasted_iota(jnp.int32, (SUBLANES, seq), 0)
    tok_rows = jnp.where(piece == 0, tok // 64, jnp.where(piece == 1, tok % 64, 0)).astype(F32)
    p_iota = lax.broadcasted_iota(jnp.int32, (1, cap), 1).astype(F32)
    for e in range(ne):
        onehot = (key_t[:, e:e + 1] == p_iota).astype(BF16)
        g = aff[e:e + 1, :]
        g_hi = g.astype(BF16).astype(F32)
        g_mid = (g - g_hi).astype(BF16).astype(F32)
        g_lo = g - g_hi - g_mid
        lhs = jnp.where(piece == 2, g_hi, jnp.where(piece == 3, g_mid, jnp.where(piece == 4, g_lo, tok_rows)))
        res = jnp.dot(lhs.astype(BF16), onehot, preferred_element_type=F32)
        idx_ref[e:e + 1, :] = (res[0:1] * 64.0 + res[1:2]).astype(jnp.int32) + b * seq
        gate_ref[e:e + 1, :] = res[2:3] + res[3:4] + res[4:5]


def _route(aff_t, cap):
    batch, ne, seq = aff_t.shape
    assert seq % LANES == 0 and seq <= 64 * 256
    return pl.pallas_call(
        functools.partial(_route_kernel, cap=cap, seq=seq),
        grid=(batch,),
        in_specs=[pl.BlockSpec((None, ne, seq), lambda b: (b, 0, 0))],
        out_specs=[pl.BlockSpec((None, ne, cap), lambda b: (b, 0, 0))] * 2,
        out_shape=[jax.ShapeDtypeStruct((batch, ne, cap), jnp.int32),
                   jax.ShapeDtypeStruct((batch, ne, cap), F32)],
        compiler_params=_cparams(("parallel",), VMEM_LIMIT),
        name="route",
    )(aff_t)


def _experts_kernel(idx_ref, idx_next_ref, h_hbm, wg_ref, wu_ref, wd_ref, y_ref, rows_ref, xe_ref, sems,
                    *, ne, nf):
    e, f = pl.program_id(0), pl.program_id(1)
    ntok, d = xe_ref.shape
    slot = e % 2
    chunks = ntok // FFN_TM
    per_chunk = ntok // (nf * chunks)

    def row_copy(idx_smem, i, buf):
        src = pl.multiple_of(idx_smem[0, 0, i] * SUBLANES, SUBLANES)
        dst = pl.multiple_of(i * SUBLANES, SUBLANES)
        return pltpu.make_async_copy(h_hbm.at[pl.ds(src, SUBLANES), :],
                                     rows_ref.at[buf, pl.ds(dst, SUBLANES), :], sems.at[buf])

    def wait_rows(buf):
        pltpu.make_async_copy(h_hbm.at[pl.ds(0, ntok * SUBLANES), :], rows_ref.at[buf], sems.at[buf]).wait()

    @pl.when((e == 0) & (f == 0))
    def _first_gather():
        def issue(i, carry):
            row_copy(idx_ref, i, 0).start()
            return carry

        lax.fori_loop(0, ntok, issue, 0, unroll=8)

    @pl.when(f == 0)
    def _repack():
        wait_rows(slot)
        for m in range(chunks):
            xe_ref[m * FFN_TM:(m + 1) * FFN_TM, :] = jnp.concatenate(
                [rows_ref[slot, pl.ds(m * FFN_TM * SUBLANES + c, FFN_TM, stride=SUBLANES), :]
                 for c in range(d // LANES)], axis=1).astype(BF16)
        y_ref[...] = jnp.zeros_like(y_ref)

    wg = wg_ref[...].astype(BF16)
    wu = wu_ref[...].astype(BF16)
    wd = wd_ref[...].astype(BF16)
    for m in range(chunks):
        xe = xe_ref[m * FFN_TM:(m + 1) * FFN_TM, :]
        a = jnp.dot(xe, wg, preferred_element_type=F32)
        u = jnp.dot(xe, wu, preferred_element_type=F32)
        hmid = (a * jax.nn.sigmoid(a) * u).astype(BF16)
        y_ref[m * FFN_TM:(m + 1) * FFN_TM, :] += jnp.dot(hmid, wd, preferred_element_type=F32)
        for i in range(per_chunk):
            row_copy(idx_next_ref, (f * chunks + m) * per_chunk + i, 1 - slot).start()

    @pl.when((e == ne - 1) & (f == nf - 1))
    def _drain():
        wait_rows(1 - slot)


def _experts(idx_e, h2_rows, w_gate, w_up, w_down):
    ne, d, ff = w_gate.shape
    ntok = idx_e.shape[-1]
    tf = FFN_TF
    nf = ff // tf
    assert ff % tf == 0 and ntok % (FFN_TM * nf) == 0 and d // LANES == SUBLANES
    return pl.pallas_call(
        functools.partial(_experts_kernel, ne=ne, nf=nf),
        grid=(ne, nf),
        in_specs=[pl.BlockSpec((1, 1, ntok), lambda e, f: (e, 0, 0), memory_space=pltpu.SMEM),
                  pl.BlockSpec((1, 1, ntok), lambda e, f: (jnp.minimum(e + 1, ne - 1), 0, 0),
                               memory_space=pltpu.SMEM),
                  pl.BlockSpec(memory_space=pl.ANY),
                  pl.BlockSpec((None, d, tf), lambda e, f: (e, 0, f)),
                  pl.BlockSpec((None, d, tf), lambda e, f: (e, 0, f)),
                  pl.BlockSpec((None, tf, d), lambda e, f: (e, f, 0))],
        out_specs=pl.BlockSpec((None, ntok, d), lambda e, f: (e, 0, 0)),
        out_shape=jax.ShapeDtypeStruct((ne, ntok, d), F32),
        scratch_shapes=[pltpu.VMEM((2, ntok * SUBLANES, LANES), F32), pltpu.VMEM((ntok, d), BF16),
                        pltpu.SemaphoreType.DMA((2,))],
        compiler_params=_cparams(("arbitrary", "arbitrary"), VMEM_LIMIT),
        name="experts",
    )(idx_e, idx_e, h2_rows, w_gate, w_up, w_down)


def _combine_kernel(idx_ref, gate_ref, y_ref, x1_ref, g_ref, o_ref, acc_ref, ybuf_ref, *, ns):
    j = pl.program_id(1)
    cap = idx_ref.shape[-1]
    tm, d = x1_ref.shape

    @pl.when(j == 0)
    def _init():
        acc_ref[...] = jnp.zeros_like(acc_ref)

    @pl.when(j < ns)
    def _scatter():
        for x in range(COMBINE_EXPERTS):
            for c in range(d // LANES):
                ybuf_ref[x, pl.ds(c, cap, stride=SUBLANES), :] = y_ref[x, :, c * LANES:(c + 1) * LANES]

            def group(k, carry, x=x):
                rows, sums = [], []
                for u in range(RMW_UNROLL):
                    i = k * RMW_UNROLL + u
                    t = pl.ds(pl.multiple_of(idx_ref[x, 0, i] * SUBLANES, SUBLANES), SUBLANES)
                    yi = ybuf_ref[x, pl.ds(pl.multiple_of(i * SUBLANES, SUBLANES), SUBLANES), :]
                    rows.append(t)
                    sums.append(acc_ref[t, :] + gate_ref[x, 0, i] * yi)
                for t, s in zip(rows, sums):
                    acc_ref[t, :] = s
                return carry

            lax.fori_loop(0, cap // RMW_UNROLL, group, 0)

    @pl.when(j >= ns)
    def _emit():
        base = pl.multiple_of((j - ns) * tm * SUBLANES, SUBLANES)
        moe = jnp.concatenate([acc_ref[pl.ds(base + c, tm, stride=SUBLANES), :] for c in range(d // LANES)],
                              axis=1)
        o_ref[...] = _rms(x1_ref[...] + moe, g_ref[...])


def _combine(idx_l, gates, y, x1, g_final, batch, seq):
    ne = y.shape[0]
    t, d = x1.shape
    cap = idx_l.shape[-1]
    tm = FINAL_TM
    nt = seq // tm
    ns = ne // COMBINE_EXPERTS
    assert cap % RMW_UNROLL == 0 and d // LANES == SUBLANES and seq % tm == 0 and ne % COMBINE_EXPERTS == 0
    step = lambda j: jnp.minimum(j, ns - 1)
    rows = lambda b, j: (b * nt + jnp.maximum(j - ns, 0), 0)
    smem = pl.BlockSpec((COMBINE_EXPERTS, 1, cap), lambda b, j: (b * ns + step(j), 0, 0), memory_space=pltpu.SMEM)
    return pl.pallas_call(
        functools.partial(_combine_kernel, ns=ns),
        grid=(batch, ns + nt),
        in_specs=[smem, smem,
                  pl.BlockSpec((COMBINE_EXPERTS, cap, d), lambda b, j: (step(j), b, 0)),
                  pl.BlockSpec((tm, d), rows), pl.BlockSpec((1, d), lambda b, j: (0, 0))],
        out_specs=pl.BlockSpec((tm, d), rows),
        out_shape=jax.ShapeDtypeStruct((t, d), F32),
        scratch_shapes=[pltpu.VMEM((seq * SUBLANES, LANES), F32),
                        pltpu.VMEM((COMBINE_EXPERTS, cap * SUBLANES, LANES), F32)],
        compiler_params=_cparams(("parallel", "arbitrary"), VMEM_LIMIT),
        name="combine",
    )(idx_l.reshape(batch * ne, 1, cap), gates.reshape(batch * ne, 1, cap),
      y, x1, g_final.reshape(1, d))


def _permute_a_heads(w_in, g_out_a, w_out):
    cols = jnp.asarray([h * HEAD_DIM + i for h in A_HEAD_ORDER for i in range(HEAD_DIM)], jnp.int32)
    w_in = jnp.concatenate([w_in[:, :A_Q][:, cols], w_in[:, A_Q:]], axis=1)
    w_out = jnp.concatenate([w_out[:A_Q][cols], w_out[A_Q:]], axis=0)
    return w_in, g_out_a[cols], w_out


def kernel(x, g_mix, w_in, a_sink, g_out_a, g_out_b, w_out, g_ffn, w_router, w_gate, w_up, w_down, g_final):
    batch, seq, d = x.shape
    depth = w_in.shape[0]
    ne = w_router.shape[-1]
    cap = CAPACITY_FACTOR * seq // ne
    assert depth == 1, "the last kernel fuses the MoE residual with the output norm of a single layer"
    l = 0
    x2 = x.reshape(batch * seq, d)
    w_in_p, g_out_a_p, w_out_p = _permute_a_heads(w_in[l], g_out_a[l], w_out[l])
    qa, ka, va, qb, kb, vb = _proj(x2, g_mix[l], w_in_p, seq)
    oa = _attn_a(qa, ka, va, a_sink[l], batch, seq)
    ob = _attn_b(qb, kb, vb, batch, seq)
    x1, h2_rows, aff_t = _mix(x2, oa, ob, g_out_a_p, g_out_b[l], w_out_p, g_ffn[l], w_router[l], batch, seq)
    idx_g, gates = _route(aff_t, cap)
    idx_e = idx_g.transpose(1, 0, 2).reshape(ne, 1, batch * cap)
    y = _experts(idx_e, h2_rows, w_gate[l], w_up[l], w_down[l])
    idx_l = idx_g - (jnp.arange(batch, dtype=jnp.int32) * seq)[:, None, None]
    out = _combine(idx_l, gates, y, x1, g_final, batch, seq)
    return out.reshape(batch, seq, d)
```

```python
import functools

import jax
import jax.numpy as jnp
from jax import lax
from jax.experimental import pallas as pl
from jax.experimental.pallas import tpu as pltpu

F32 = jnp.float32
BF16 = jnp.bfloat16

LANES = 128
SUBLANES = 8
BF16_ROWS = 16
VMEM_LIMIT = 56 * 1024 * 1024

HEAD_DIM = 64
A_HEADS = 8
A_KV_HEADS = 2
A_GROUP = A_HEADS // A_KV_HEADS
A_HALF_WINDOW = 128
B_HEADS = 8
DILATED_PATTERNS = ((128, 1), (512, 4), (2048, 16))
A_Q = A_HEADS * HEAD_DIM
A_KV = A_KV_HEADS * HEAD_DIM
B_W = B_HEADS * HEAD_DIM
ROPE_THETA = 500000.0
ROT_DIM = HEAD_DIM // 4
N_EXPERTS = 16
CAPACITY_FACTOR = 2
NORM_EPS = 1e-6
NEG_INF = -1e30
LOG2_E = 1.4426950408889634

PROJ_TM = 1024
ATT_QB = 128
ATT_A_SLOTS = 2
ATT_B_SLOTS = 4
MIX_TM = 1024
FFN_TF = 512
FFN_TM = 512
RMW_UNROLL = 8

A_HEAD_ORDER = tuple(h for j in range(A_GROUP) for h in (j, A_GROUP + j))


def _cparams(sem, vmem=None):
    return pltpu.CompilerParams(dimension_semantics=sem, vmem_limit_bytes=vmem)


def _rope_tables(seq_len):
    half = ROT_DIM // 2
    inv_freq = ROPE_THETA ** (-jnp.arange(0, ROT_DIM, 2, dtype=F32) / ROT_DIM)
    ang = jnp.arange(seq_len, dtype=F32)[:, None] * inv_freq[None, :]
    cos, sin = jnp.cos(ang), jnp.sin(ang)
    ones = jnp.ones((seq_len, HEAD_DIM - ROT_DIM), F32)
    zeros = jnp.zeros((seq_len, HEAD_DIM - ROT_DIM), F32)
    zh = jnp.zeros((seq_len, half), F32)
    cos_t = jnp.concatenate([cos, cos, ones], axis=1)
    sin_a = jnp.concatenate([-sin, zh, zeros], axis=1)
    sin_b = jnp.concatenate([zh, sin, zeros], axis=1)
    rep = LANES // HEAD_DIM
    return tuple(jnp.tile(t, (1, rep)) for t in (cos_t, sin_a, sin_b))


def _rms(x, g):
    return x * lax.rsqrt(jnp.mean(x * x, axis=-1, keepdims=True) + NORM_EPS) * g


def _proj_kernel(x_ref, g_ref, w_ref, cos_ref, sa_ref, sb_ref,
                 qa_ref, ka_ref, va_ref, qb_ref, kb_ref, vb_ref):
    h = _rms(x_ref[...], g_ref[...]).astype(BF16)
    cos, sa, sb = cos_ref[...], sa_ref[...], sb_ref[...]
    half = ROT_DIM // 2

    def rope(t):
        return t * cos + pltpu.roll(t, LANES - half, 1) * sa + pltpu.roll(t, half, 1) * sb

    def section(col0, width):
        return jnp.dot(h, w_ref[:, col0:col0 + width], preferred_element_type=F32)

    scale = HEAD_DIM ** -0.5 * LOG2_E
    qa = section(0, A_Q)
    for c in range(A_Q // LANES):
        qa_ref[c] = rope(qa[:, c * LANES:(c + 1) * LANES]) * scale
    kv = section(A_Q, 2 * A_KV)
    ka_ref[...] = rope(kv[:, :A_KV])
    va_ref[...] = kv[:, A_KV:]
    qb = section(A_Q + 2 * A_KV, B_W)
    for c in range(B_W // LANES):
        qb_ref[c] = rope(qb[:, c * LANES:(c + 1) * LANES]) * scale
    kb = section(A_Q + 2 * A_KV + B_W, B_W)
    for c in range(B_W // LANES):
        kb_ref[c] = rope(kb[:, c * LANES:(c + 1) * LANES])
    vb = section(A_Q + 2 * A_KV + 2 * B_W, B_W)
    for c in range(B_W // LANES):
        vb_ref[c] = vb[:, c * LANES:(c + 1) * LANES]


def _proj(x2, g_mix, w_in, seq):
    t, d = x2.shape
    tm = PROJ_TM
    assert t % tm == 0 and seq % tm == 0 and A_KV == LANES
    nseq = seq // tm
    tables = _rope_tables(seq)
    row = lambda i: (i, 0)
    slab = lambda i: (0, i, 0)
    const = lambda i: (0, 0)
    tab = lambda i: (i % nseq, 0)
    widths = (A_Q, A_KV, A_KV, B_W, B_W, B_W)
    return pl.pallas_call(
        _proj_kernel,
        grid=(t // tm,),
        in_specs=[pl.BlockSpec((tm, d), row), pl.BlockSpec((1, d), const),
                  pl.BlockSpec(w_in.shape, const)] + [pl.BlockSpec((tm, LANES), tab)] * 3,
        out_specs=[pl.BlockSpec((w // LANES, tm, LANES), slab) if w > LANES else pl.BlockSpec((tm, w), row)
                   for w in widths],
        out_shape=[jax.ShapeDtypeStruct((w // LANES, t, LANES) if w > LANES else (t, w), F32) for w in widths],
        compiler_params=_cparams(("parallel",), VMEM_LIMIT),
        name="proj",
    )(x2, g_mix.reshape(1, d), w_in.astype(BF16), *tables)


def _band_bias(bias_ref, qb, kw, hw):
    kk = lax.broadcasted_iota(jnp.int32, (kw, 2 * qb), 0)
    qq = lax.broadcasted_iota(jnp.int32, (kw, 2 * qb), 1) % qb
    for t, delta in enumerate((0, -hw, -2 * hw)):
        bias_ref[t] = jnp.where(jnp.abs(kk + delta - qq) <= hw, 0.0, NEG_INF)


def _band_scratch(qb, kw, slots):
    n = 2 * slots
    return ([pltpu.VMEM((kw, 2 * qb), F32)] * n + [pltpu.VMEM((kw, 2 * qb), BF16)] * n
            + [pltpu.VMEM((SUBLANES, 2 * qb), F32)] * n + [pltpu.VMEM((LANES, kw), BF16)] * n)


def _band_pipeline(q_ref, k_ref, v_ref, bias_ref, o_ref, lse_ref, bufs, *, nblocks, geometry, sink_row=None):
    qb = ATT_QB
    n = len(bufs) // 4
    slots = n // 2
    s_bufs, p_bufs, ml_bufs, vt_bufs = (bufs[i * n:(i + 1) * n] for i in range(4))
    kw = s_bufs[0].shape[0]
    for s_buf, p_buf, ml_buf, vt_buf in zip(s_bufs, p_bufs, ml_bufs, vt_bufs):
        s_buf[...] = jnp.zeros_like(s_buf)
        p_buf[...] = jnp.zeros_like(p_buf)
        ml_buf[...] = jnp.ones_like(ml_buf)
        vt_buf[...] = jnp.zeros_like(vt_buf)

    def rows(block):
        q0, k0, kind = geometry(jnp.clip(block, 0, nblocks - 1))
        return pl.ds(pl.multiple_of(q0, BF16_ROWS), qb), pl.ds(pl.multiple_of(k0, BF16_ROWS), kw), kind

    def scores(block, s_buf):
        q_rows, k_rows, kind = rows(block)
        q = q_ref[q_rows, :]
        lo = lax.broadcasted_iota(jnp.int32, q.shape, 1) < HEAD_DIM
        q2 = jnp.concatenate([jnp.where(lo, q, 0.0), jnp.where(lo, 0.0, q)], axis=0).astype(BF16)
        s_buf[...] = lax.dot_general(k_ref[k_rows, :].astype(BF16), q2, (((1,), (1,)), ((), ())),
                                     preferred_element_type=F32) + bias_ref[kind]

    def softmax(block, s_buf, p_buf, ml_buf, vt_buf):
        _, k_rows, _ = rows(block)
        vt_buf[...] = v_ref[k_rows, :].astype(F32).T.astype(BF16)
        m = jnp.max(s_buf[...], axis=0, keepdims=True)
        if sink_row is not None:
            m = jnp.maximum(m, sink_row)
        p = jnp.exp2(s_buf[...] - m)
        l = jnp.sum(p, axis=0, keepdims=True)
        if sink_row is not None:
            l = l + jnp.exp2(sink_row - m)
        p_buf[...] = p.astype(BF16)
        ml_buf[0:1, :] = m
        ml_buf[1:2, :] = l

    def output(block, p_buf, ml_buf, vt_buf):
        q_rows, _, _ = rows(block)
        o_t = jnp.dot(vt_buf[...], p_buf[...], preferred_element_type=F32)
        o_tc = jnp.concatenate([o_t[:HEAD_DIM, :qb], o_t[HEAD_DIM:, qb:]], axis=0)
        top = lax.broadcasted_iota(jnp.int32, (LANES, qb), 0) < HEAD_DIM
        m, l = ml_buf[0:1, :], ml_buf[1:2, :]
        inv = 1.0 / l
        o_ref[q_rows, :] = (o_tc * jnp.where(top, inv[:, :qb], inv[:, qb:])).T
        if lse_ref is not None:
            lse = m + jnp.log2(l)
            lse_ref[q_rows, :] = jnp.where(top, lse[:, :qb], lse[:, qb:]).T

    def trip_pair(jj, carry):
        for parity in range(2):
            for slot in range(slots):
                t = (2 * jj + parity) * slots + slot
                rd, wr = (1 - parity) * slots + slot, parity * slots + slot
                output(t - 2 * slots, p_bufs[rd], ml_bufs[rd], vt_bufs[rd])
                softmax(t - slots, s_bufs[rd], p_bufs[wr], ml_bufs[wr], vt_bufs[wr])
                scores(t, s_bufs[wr])
        return carry

    trips = nblocks // slots + 2
    assert nblocks % slots == 0 and trips % 2 == 0
    lax.fori_loop(0, trips // 2, trip_pair, 0)


def _segment_geometry(block, n, hw, base=0):
    qb = ATT_QB
    nblk = n // qb
    seg, jb = block // nblk, block % nblk
    qs = jb * qb
    ks = jnp.clip(qs - hw, 0, n - (qb + 2 * hw))
    kind = jnp.where(jb == 0, 0, jnp.where(jb == nblk - 1, 2, 1))
    return base + seg * n + qs, base + seg * n + ks, kind


def _attn_a_kernel(sink_ref, q_ref, k_ref, v_ref, o_ref, bias_ref, *bufs, seq):
    j = pl.program_id(1)
    qb, hw = ATT_QB, A_HALF_WINDOW
    _band_bias(bias_ref, qb, qb + 2 * hw, hw)
    col = lax.broadcasted_iota(jnp.int32, (1, 2 * qb), 1)
    sink_row = jnp.where(col < qb, sink_ref[j], sink_ref[A_GROUP + j]) * LOG2_E
    _band_pipeline(q_ref, k_ref, v_ref, bias_ref, o_ref, None, bufs, nblocks=seq // qb,
                   geometry=functools.partial(_segment_geometry, n=seq, hw=hw), sink_row=sink_row)


def _attn_a(qa, ka, va, sink, batch, seq):
    qb, hw = ATT_QB, A_HALF_WINDOW
    assert seq % qb == 0 and seq >= qb + 2 * hw and A_KV == LANES
    qspec = pl.BlockSpec((None, seq, LANES), lambda b, j: (j, b, 0))
    kvspec = pl.BlockSpec((seq, LANES), lambda b, j: (b, 0))
    return pl.pallas_call(
        functools.partial(_attn_a_kernel, seq=seq),
        grid=(batch, A_Q // LANES),
        in_specs=[pl.BlockSpec(memory_space=pltpu.SMEM), qspec, kvspec, kvspec],
        out_specs=qspec,
        out_shape=jax.ShapeDtypeStruct(qa.shape, F32),
        scratch_shapes=[pltpu.VMEM((3, qb + 2 * hw, 2 * qb), F32)] + _band_scratch(qb, qb + 2 * hw, ATT_A_SLOTS),
        compiler_params=_cparams(("parallel", "parallel"), VMEM_LIMIT),
        name="attn_a",
    )(sink, qa, ka, va)


def _attn_b_kernel(q_ref, k_ref, v_ref, out_ref, bias_ref, tmp_ref, qs_ref, ks_ref, vs_ref, os_ref, ls_ref,
                   *bufs, seq):
    (w1, d1), (w4, d4), (w16, d16) = DILATED_PATTERNS
    hw = w1 // (2 * d1)
    qb = ATT_QB
    _band_bias(bias_ref, qb, qb + 2 * hw, hw)
    n4, n16 = seq // d4, seq // d16
    step = d16 // d4
    chunk = 512
    for src, dst in ((q_ref, qs_ref), (k_ref, ks_ref), (v_ref, vs_ref)):
        for c in range(seq // chunk):
            dst[c * chunk:(c + 1) * chunk, :] = src[c * chunk:(c + 1) * chunk, :].astype(dst.dtype)
        for r in range(d4):
            tmp_ref[r * n4:(r + 1) * n4, :] = src[pl.ds(r, n4, stride=d4), :]
        for c in range(seq // chunk):
            dst[seq + c * chunk:seq + (c + 1) * chunk, :] = tmp_ref[c * chunk:(c + 1) * chunk, :].astype(dst.dtype)
        for r in range(d16):
            dst[2 * seq + r * n16:2 * seq + (r + 1) * n16, :] = tmp_ref[
                pl.ds((r % d4) * n4 + r // d4, n16, stride=step), :].astype(dst.dtype)

    bps = seq // qb
    lg = [(seq // d // qb).bit_length() - 1 for (_, d) in DILATED_PATTERNS]

    def geometry(block):
        pat, within = block // bps, block % bps
        sh = jnp.where(pat == 0, lg[0], jnp.where(pat == 1, lg[1], lg[2]))
        seg, jb = within >> sh, within & ((1 << sh) - 1)
        n = qb << sh
        qs = jb * qb
        ks = jnp.clip(qs - hw, 0, n - (qb + 2 * hw))
        kind = jnp.where(jb == 0, 0, jnp.where(jb == (1 << sh) - 1, 2, 1))
        base = pat * seq + seg * n
        return base + qs, base + ks, kind

    _band_pipeline(qs_ref, ks_ref, vs_ref, bias_ref, os_ref, ls_ref, bufs,
                   nblocks=len(DILATED_PATTERNS) * bps, geometry=geometry)

    def merge(r, carry):
        rows16 = pl.ds(pl.multiple_of(2 * seq + r * n16, SUBLANES), n16)
        rows4 = pl.ds(seq + (r % d4) * n4 + r // d4, n16, stride=step)
        rows1 = pl.ds(r, n16, stride=d16)
        la, lb, lc = ls_ref[rows1, :], ls_ref[rows4, :], ls_ref[rows16, :]
        top = jnp.maximum(jnp.maximum(la, lb), lc)
        wa, wb, wc = jnp.exp2(la - top), jnp.exp2(lb - top), jnp.exp2(lc - top)
        num = wa * os_ref[rows1, :] + wb * os_ref[rows4, :] + wc * os_ref[rows16, :]
        out_ref[rows1, :] = num * (1.0 / (wa + wb + wc))
        return carry

    lax.fori_loop(0, d16, merge, 0)


def _attn_b(qb_, kb, vb, batch, seq):
    (w1, d1), (w4, d4), (w16, d16) = DILATED_PATTERNS
    hw = w1 // (2 * d1)
    qb = ATT_QB
    assert d1 == 1 and d16 % d4 == 0 and w4 // (2 * d4) == hw and w16 // (2 * d16) == hw
    assert seq % 512 == 0 and seq // d16 >= qb + 2 * hw
    for (_, d) in DILATED_PATTERNS:
        nblk = seq // d // qb
        assert nblk * d * qb == seq and nblk & (nblk - 1) == 0
    spec = pl.BlockSpec((None, seq, LANES), lambda b, p: (p, b, 0))
    npat = len(DILATED_PATTERNS)
    return pl.pallas_call(
        functools.partial(_attn_b_kernel, seq=seq),
        grid=(batch, B_W // LANES),
        in_specs=[spec] * 3,
        out_specs=spec,
        out_shape=jax.ShapeDtypeStruct(qb_.shape, F32),
        scratch_shapes=[pltpu.VMEM((3, qb + 2 * hw, 2 * qb), F32), pltpu.VMEM((seq, LANES), F32),
                        pltpu.VMEM((npat * seq, LANES), F32), pltpu.VMEM((npat * seq, LANES), BF16),
                        pltpu.VMEM((npat * seq, LANES), BF16), pltpu.VMEM((npat * seq, LANES), F32),
                        pltpu.VMEM((npat * seq, LANES), F32)] + _band_scratch(qb, qb + 2 * hw, ATT_B_SLOTS),
        compiler_params=_cparams(("parallel", "parallel"), VMEM_LIMIT),
        name="attn_b",
    )(qb_, kb, vb)


def _mix_kernel(x_ref, oa_ref, ob_ref, ga_ref, gb_ref, wo_ref, gf_ref, wr_ref,
                x1_ref, h2_ref, aff_ref):
    oa = jnp.concatenate([oa_ref[c] for c in range(oa_ref.shape[0])], axis=1)
    ob = jnp.concatenate([ob_ref[c] for c in range(ob_ref.shape[0])], axis=1)
    na = _rms(oa, ga_ref[...]).astype(BF16)
    nb = _rms(ob, gb_ref[...]).astype(BF16)
    x1 = (x_ref[...] + jnp.dot(na, wo_ref[:A_Q, :], preferred_element_type=F32)
          + jnp.dot(nb, wo_ref[A_Q:, :], preferred_element_type=F32))
    x1_ref[...] = x1
    h2 = _rms(x1, gf_ref[...])
    tm = h2.shape[0]
    for c in range(h2.shape[1] // LANES):
        h2_ref[pl.ds(c, tm, stride=SUBLANES), :] = h2[:, c * LANES:(c + 1) * LANES]
    ne = aff_ref.shape[0]
    dn = (((1,), (1,)), ((), ()))
    h_hi = h2.astype(BF16)
    h_mid = (h2 - h_hi.astype(F32)).astype(BF16)
    r = lax.dot_general(wr_ref[...], h_hi, dn, preferred_element_type=F32)
    logits = r[:ne] + r[ne:] + lax.dot_general(wr_ref[:ne, :], h_mid, dn, preferred_element_type=F32)
    e = jnp.exp(logits - jnp.max(logits, axis=0, keepdims=True))
    aff_ref[...] = e / jnp.sum(e, axis=0, keepdims=True)


def _mix(x2, oa, ob, g_out_a, g_out_b, w_out, g_ffn, w_router, batch, seq):
    t, d = x2.shape
    tm = MIX_TM
    assert seq % tm == 0 and d % LANES == 0 and d // LANES == SUBLANES
    ns = seq // tm
    row = lambda i: (i, 0)
    const = lambda i: (0, 0)
    ne = w_router.shape[1]
    w_hi = w_router.T.astype(BF16)
    w_mid = (w_router.T - w_hi.astype(F32)).astype(BF16)
    wr = jnp.concatenate([w_hi, w_mid], axis=0)
    return pl.pallas_call(
        _mix_kernel,
        grid=(t // tm,),
        in_specs=[pl.BlockSpec((tm, d), row), pl.BlockSpec((A_Q // LANES, tm, LANES), lambda i: (0, i, 0)),
                  pl.BlockSpec((B_W // LANES, tm, LANES), lambda i: (0, i, 0)),
                  pl.BlockSpec((1, A_Q), const), pl.BlockSpec((1, B_W), const),
                  pl.BlockSpec(w_out.shape, const), pl.BlockSpec((1, d), const),
                  pl.BlockSpec((2 * ne, d), const)],
        out_specs=[pl.BlockSpec((tm, d), row), pl.BlockSpec((tm * SUBLANES, LANES), row),
                   pl.BlockSpec((None, ne, tm), lambda i: (i // ns, 0, i % ns))],
        out_shape=[jax.ShapeDtypeStruct((t, d), F32),
                   jax.ShapeDtypeStruct((t * SUBLANES, LANES), F32),
                   jax.ShapeDtypeStruct((batch, ne, seq), F32)],
        compiler_params=_cparams(("parallel",), VMEM_LIMIT),
        name="mix",
    )(x2, oa, ob, g_out_a.reshape(1, -1), g_out_b.reshape(1, -1), w_out.astype(BF16),
      g_ffn.reshape(1, d), wr)


def _lane_cumsum(mask_f, tri):
    rows, s = mask_f.shape
    carry = jnp.zeros((rows, 1), F32)
    out = []
    for c in range(s // LANES):
        inc = jnp.dot(mask_f[:, c * LANES:(c + 1) * LANES].astype(BF16), tri,
                      preferred_element_type=F32) + carry
        out.append(inc)
        carry = inc[:, LANES - 1:LANES]
    return jnp.concatenate(out, axis=1)


def _route_kernel(aff_ref, idx_ref, gate_ref, *, cap, seq):
    b = pl.program_id(0)
    aff = aff_ref[...]
    ne = aff.shape[0]

    def search(i, thr):
        cand = thr | (jnp.int32(1) << (30 - i))
        cnt = jnp.sum((aff >= lax.bitcast_convert_type(cand, F32)).astype(F32), axis=1, keepdims=True)
        return jnp.where(cnt >= cap, cand, thr)

    thr = lax.bitcast_convert_type(lax.fori_loop(0, 31, search, jnp.zeros((ne, 1), jnp.int32)), F32)
    gt = aff > thr
    eq = aff == thr
    n_gt = jnp.sum(gt.astype(F32), axis=1, keepdims=True)
    ri = lax.broadcasted_iota(jnp.int32, (LANES, LANES), 0)
    ci = lax.broadcasted_iota(jnp.int32, (LANES, LANES), 1)
    tri = (ri <= ci).astype(BF16)
    eq_f = eq.astype(F32)
    eq_rank = _lane_cumsum(eq_f, tri) - eq_f
    sel = gt | (eq & (eq_rank < cap - n_gt))
    sel_f = sel.astype(F32)
    slot = _lane_cumsum(sel_f, tri) - sel_f
    key_t = jnp.where(sel, slot, -1.0).T
    tok = lax.broadcasted_iota(jnp.int32, (SUBLANES, seq), 1)
    piece = lax.broadcasted_iota(jnp.int32, (SUBLANES, seq), 0)
    tok_rows = jnp.where(piece == 0, tok // 64, jnp.where(piece == 1, tok % 64, 0)).astype(F32)
    p_iota = lax.broadcasted_iota(jnp.int32, (1, cap), 1).astype(F32)
    for e in range(ne):
        onehot = (key_t[:, e:e + 1] == p_iota).astype(BF16)
        g = aff[e:e + 1, :]
        g_hi = g.astype(BF16).astype(F32)
        g_mid = (g - g_hi).astype(BF16).astype(F32)
        g_lo = g - g_hi - g_mid
        lhs = jnp.where(piece == 2, g_hi, jnp.where(piece == 3, g_mid, jnp.where(piece == 4, g_lo, tok_rows)))
        res = jnp.dot(lhs.astype(BF16), onehot, preferred_element_type=F32)
        idx_ref[e:e + 1, :] = (res[0:1] * 64.0 + res[1:2]).astype(jnp.int32) + b * seq
        gate_ref[e:e + 1, :] = res[2:3] + res[3:4] + res[4:5]


def _route(aff_t, cap):
    batch, ne, seq = aff_t.shape
    assert seq % LANES == 0 and seq <= 64 * 256
    return pl.pallas_call(
        functools.partial(_route_kernel, cap=cap, seq=seq),
        grid=(batch,),
        in_specs=[pl.BlockSpec((None, ne, seq), lambda b: (b, 0, 0))],
        out_specs=[pl.BlockSpec((None, ne, cap), lambda b: (b, 0, 0))] * 2,
        out_shape=[jax.ShapeDtypeStruct((batch, ne, cap), jnp.int32),
                   jax.ShapeDtypeStruct((batch, ne, cap), F32)],
        compiler_params=_cparams(("parallel",), VMEM_LIMIT),
        name="route",
    )(aff_t)


def _experts_kernel(idx_ref, idx_next_ref, h_hbm, wg_ref, wu_ref, wd_ref, y_ref, rows_ref, xe_ref, sems,
                    *, ne, nf):
    e, f = pl.program_id(0), pl.program_id(1)
    ntok, d = xe_ref.shape
    slot = e % 2
    chunks = ntok // FFN_TM
    per_chunk = ntok // (nf * chunks)

    def row_copy(idx_smem, i, buf):
        src = pl.multiple_of(idx_smem[0, 0, i] * SUBLANES, SUBLANES)
        dst = pl.multiple_of(i * SUBLANES, SUBLANES)
        return pltpu.make_async_copy(h_hbm.at[pl.ds(src, SUBLANES), :],
                                     rows_ref.at[buf, pl.ds(dst, SUBLANES), :], sems.at[buf])

    def wait_rows(buf):
        pltpu.make_async_copy(h_hbm.at[pl.ds(0, ntok * SUBLANES), :], rows_ref.at[buf], sems.at[buf]).wait()

    @pl.when((e == 0) & (f == 0))
    def _first_gather():
        def issue(i, carry):
            row_copy(idx_ref, i, 0).start()
            return carry

        lax.fori_loop(0, ntok, issue, 0, unroll=8)

    @pl.when(f == 0)
    def _repack():
        wait_rows(slot)
        for m in range(chunks):
            xe_ref[m * FFN_TM:(m + 1) * FFN_TM, :] = jnp.concatenate(
                [rows_ref[slot, pl.ds(m * FFN_TM * SUBLANES + c, FFN_TM, stride=SUBLANES), :]
                 for c in range(d // LANES)], axis=1).astype(BF16)
        y_ref[...] = jnp.zeros_like(y_ref)

    wg = wg_ref[...].astype(BF16)
    wu = wu_ref[...].astype(BF16)
    wd = wd_ref[...].astype(BF16)
    for m in range(chunks):
        xe = xe_ref[m * FFN_TM:(m + 1) * FFN_TM, :]
        a = jnp.dot(xe, wg, preferred_element_type=F32)
        u = jnp.dot(xe, wu, preferred_element_type=F32)
        hmid = (a * jax.nn.sigmoid(a) * u).astype(BF16)
        y_ref[m * FFN_TM:(m + 1) * FFN_TM, :] += jnp.dot(hmid, wd, preferred_element_type=F32)
        for i in range(per_chunk):
            row_copy(idx_next_ref, (f * chunks + m) * per_chunk + i, 1 - slot).start()

    @pl.when((e == ne - 1) & (f == nf - 1))
    def _drain():
        wait_rows(1 - slot)


def _experts(idx_e, h2_rows, w_gate, w_up, w_down):
    ne, d, ff = w_gate.shape
    ntok = idx_e.shape[-1]
    tf = FFN_TF
    nf = ff // tf
    assert ff % tf == 0 and ntok % (FFN_TM * nf) == 0 and d // LANES == SUBLANES
    return pl.pallas_call(
        functools.partial(_experts_kernel, ne=ne, nf=nf),
        grid=(ne, nf),
        in_specs=[pl.BlockSpec((1, 1, ntok), lambda e, f: (e, 0, 0), memory_space=pltpu.SMEM),
                  pl.BlockSpec((1, 1, ntok), lambda e, f: (jnp.minimum(e + 1, ne - 1), 0, 0),
                               memory_space=pltpu.SMEM),
                  pl.BlockSpec(memory_space=pl.ANY),
                  pl.BlockSpec((None, d, tf), lambda e, f: (e, 0, f)),
                  pl.BlockSpec((None, d, tf), lambda e, f: (e, 0, f)),
                  pl.BlockSpec((None, tf, d), lambda e, f: (e, f, 0))],
        out_specs=pl.BlockSpec((None, ntok, d), lambda e, f: (e, 0, 0)),
        out_shape=jax.ShapeDtypeStruct((ne, ntok, d), F32),
        scratch_shapes=[pltpu.VMEM((2, ntok * SUBLANES, LANES), F32), pltpu.VMEM((ntok, d), BF16),
                        pltpu.SemaphoreType.DMA((2,))],
        compiler_params=_cparams(("arbitrary", "arbitrary"), VMEM_LIMIT),
        name="experts",
    )(idx_e, idx_e, h2_rows, w_gate, w_up, w_down)


def _combine_kernel(idx_ref, gate_ref, y_ref, x1_ref, g_ref, o_ref, acc_ref, ybuf_ref, *, batch):
    b, e = pl.program_id(0), pl.program_id(1)
    cap, d = y_ref.shape
    tm = x1_ref.shape[0]
    cur = b % 2

    @pl.when(b < batch)
    def _scatter():
        @pl.when(e == 0)
        def _init():
            acc_ref[cur] = jnp.zeros(acc_ref.shape[1:], F32)

        for c in range(d // LANES):
            ybuf_ref[pl.ds(c, cap, stride=SUBLANES), :] = y_ref[:, c * LANES:(c + 1) * LANES]

        def group(k, carry):
            rows, sums = [], []
            for u in range(RMW_UNROLL):
                i = k * RMW_UNROLL + u
                t = pl.ds(pl.multiple_of(idx_ref[0, 0, i] * SUBLANES, SUBLANES), SUBLANES)
                yi = ybuf_ref[pl.ds(pl.multiple_of(i * SUBLANES, SUBLANES), SUBLANES), :]
                rows.append(t)
                sums.append(acc_ref[cur, t, :] + gate_ref[0, 0, i] * yi)
            for t, s in zip(rows, sums):
                acc_ref[cur, t, :] = s
            return carry

        lax.fori_loop(0, cap // RMW_UNROLL, group, 0)

    @pl.when(b > 0)
    def _emit():
        base = pl.multiple_of(e * tm * SUBLANES, SUBLANES)
        moe = jnp.concatenate([acc_ref[1 - cur, pl.ds(base + c, tm, stride=SUBLANES), :]
                               for c in range(d // LANES)], axis=1)
        o_ref[...] = _rms(x1_ref[...] + moe, g_ref[...])


def _combine(idx_l, gates, y, x1, g_final, batch, seq):
    ne = y.shape[0]
    t, d = x1.shape
    cap = idx_l.shape[-1]
    tm = seq // ne
    assert cap % RMW_UNROLL == 0 and d // LANES == SUBLANES and tm * ne == seq and tm % SUBLANES == 0
    scattered = lambda b: jnp.minimum(b, batch - 1)
    rows = lambda b, e: (jnp.maximum(b - 1, 0) * ne + jnp.where(b > 0, e, 0), 0)
    smem = pl.BlockSpec((1, 1, cap), lambda b, e: (scattered(b) * ne + e, 0, 0), memory_space=pltpu.SMEM)
    return pl.pallas_call(
        functools.partial(_combine_kernel, batch=batch),
        grid=(batch + 1, ne),
        in_specs=[smem, smem, pl.BlockSpec((None, cap, d), lambda b, e: (e, scattered(b), 0)),
                  pl.BlockSpec((tm, d), rows), pl.BlockSpec((1, d), lambda b, e: (0, 0))],
        out_specs=pl.BlockSpec((tm, d), rows),
        out_shape=jax.ShapeDtypeStruct((t, d), F32),
        scratch_shapes=[pltpu.VMEM((2, seq * SUBLANES, LANES), F32), pltpu.VMEM((cap * SUBLANES, LANES), F32)],
        compiler_params=_cparams(("arbitrary", "arbitrary"), VMEM_LIMIT),
        name="combine",
    )(idx_l.reshape(batch * ne, 1, cap), gates.reshape(batch * ne, 1, cap), y, x1, g_final.reshape(1, d))


def _permute_a_heads(w_in, g_out_a, w_out):
    cols = jnp.asarray([h * HEAD_DIM + i for h in A_HEAD_ORDER for i in range(HEAD_DIM)], jnp.int32)
    w_in = jnp.concatenate([w_in[:, :A_Q][:, cols], w_in[:, A_Q:]], axis=1)
    w_out = jnp.concatenate([w_out[:A_Q][cols], w_out[A_Q:]], axis=0)
    return w_in, g_out_a[cols], w_out


def kernel(x, g_mix, w_in, a_sink, g_out_a, g_out_b, w_out, g_ffn, w_router, w_gate, w_up, w_down, g_final):
    batch, seq, d = x.shape
    depth = w_in.shape[0]
    ne = w_router.shape[-1]
    cap = CAPACITY_FACTOR * seq // ne
    assert depth == 1, "the last kernel fuses the MoE residual with the output norm of a single layer"
    l = 0
    x2 = x.reshape(batch * seq, d)
    w_in_p, g_out_a_p, w_out_p = _permute_a_heads(w_in[l], g_out_a[l], w_out[l])
    qa, ka, va, qb, kb, vb = _proj(x2, g_mix[l], w_in_p, seq)
    oa = _attn_a(qa, ka, va, a_sink[l], batch, seq)
    ob = _attn_b(qb, kb, vb, batch, seq)
    x1, h2_rows, aff_t = _mix(x2, oa, ob, g_out_a_p, g_out_b[l], w_out_p, g_ffn[l], w_router[l], batch, seq)
    idx_g, gates = _route(aff_t, cap)
    idx_e = idx_g.transpose(1, 0, 2).reshape(ne, 1, batch * cap)
    y = _experts(idx_e, h2_rows, w_gate[l], w_up[l], w_down[l])
    idx_l = idx_g - (jnp.arange(batch, dtype=jnp.int32) * seq)[:, None, None]
    out = _combine(idx_l, gates, y, x1, g_final, batch, seq)
    return out.reshape(batch, seq, d)
```

```python
import functools

import jax
import jax.numpy as jnp
from jax import lax
from jax.experimental import pallas as pl
from jax.experimental.pallas import tpu as pltpu

F32 = jnp.float32
BF16 = jnp.bfloat16

LANES = 128
SUBLANES = 8
BF16_ROWS = 16
VMEM_LIMIT = 56 * 1024 * 1024

HEAD_DIM = 64
A_HEADS = 8
A_KV_HEADS = 2
A_GROUP = A_HEADS // A_KV_HEADS
A_HALF_WINDOW = 128
B_HEADS = 8
DILATED_PATTERNS = ((128, 1), (512, 4), (2048, 16))
A_Q = A_HEADS * HEAD_DIM
A_KV = A_KV_HEADS * HEAD_DIM
B_W = B_HEADS * HEAD_DIM
ROPE_THETA = 500000.0
ROT_DIM = HEAD_DIM // 4
N_EXPERTS = 16
CAPACITY_FACTOR = 2
NORM_EPS = 1e-6
NEG_INF = -1e30
LOG2_E = 1.4426950408889634

PROJ_TM = 1024
ATT_QB = 128
ATT_A_SLOTS = 2
ATT_B_SLOTS = 4
MIX_TM = 1024
FFN_TF = 512
FFN_TM = 512
RMW_UNROLL = 8

A_HEAD_ORDER = tuple(h for j in range(A_GROUP) for h in (j, A_GROUP + j))


def _cparams(sem, vmem=None):
    return pltpu.CompilerParams(dimension_semantics=sem, vmem_limit_bytes=vmem)


def _rope_tables(seq_len):
    half = ROT_DIM // 2
    inv_freq = ROPE_THETA ** (-jnp.arange(0, ROT_DIM, 2, dtype=F32) / ROT_DIM)
    ang = jnp.arange(seq_len, dtype=F32)[:, None] * inv_freq[None, :]
    cos, sin = jnp.cos(ang), jnp.sin(ang)
    ones = jnp.ones((seq_len, HEAD_DIM - ROT_DIM), F32)
    zeros = jnp.zeros((seq_len, HEAD_DIM - ROT_DIM), F32)
    zh = jnp.zeros((seq_len, half), F32)
    cos_t = jnp.concatenate([cos, cos, ones], axis=1)
    sin_a = jnp.concatenate([-sin, zh, zeros], axis=1)
    sin_b = jnp.concatenate([zh, sin, zeros], axis=1)
    rep = LANES // HEAD_DIM
    return tuple(jnp.tile(t, (1, rep)) for t in (cos_t, sin_a, sin_b))


def _rms(x, g):
    return x * lax.rsqrt(jnp.mean(x * x, axis=-1, keepdims=True) + NORM_EPS) * g


def _proj_kernel(x_ref, g_ref, w_ref, cos_ref, sa_ref, sb_ref,
                 qa_ref, ka_ref, va_ref, qb_ref, kb_ref, vb_ref):
    h = _rms(x_ref[...], g_ref[...]).astype(BF16)
    cos, sa, sb = cos_ref[...], sa_ref[...], sb_ref[...]
    half = ROT_DIM // 2

    def rope(t):
        return t * cos + pltpu.roll(t, LANES - half, 1) * sa + pltpu.roll(t, half, 1) * sb

    def section(col0, width):
        return jnp.dot(h, w_ref[:, col0:col0 + width], preferred_element_type=F32)

    scale = HEAD_DIM ** -0.5 * LOG2_E
    qa = section(0, A_Q)
    for c in range(A_Q // LANES):
        qa_ref[c] = rope(qa[:, c * LANES:(c + 1) * LANES]) * scale
    kv = section(A_Q, 2 * A_KV)
    ka_ref[...] = rope(kv[:, :A_KV])
    va_ref[...] = kv[:, A_KV:]
    qb = section(A_Q + 2 * A_KV, B_W)
    for c in range(B_W // LANES):
        qb_ref[c] = rope(qb[:, c * LANES:(c + 1) * LANES]) * scale
    kb = section(A_Q + 2 * A_KV + B_W, B_W)
    for c in range(B_W // LANES):
        kb_ref[c] = rope(kb[:, c * LANES:(c + 1) * LANES])
    vb = section(A_Q + 2 * A_KV + 2 * B_W, B_W)
    for c in range(B_W // LANES):
        vb_ref[c] = vb[:, c * LANES:(c + 1) * LANES]


def _proj(x2, g_mix, w_in, seq):
    t, d = x2.shape
    tm = PROJ_TM
    assert t % tm == 0 and seq % tm == 0 and A_KV == LANES
    nseq = seq // tm
    tables = _rope_tables(seq)
    row = lambda i: (i, 0)
    slab = lambda i: (0, i, 0)
    const = lambda i: (0, 0)
    tab = lambda i: (i % nseq, 0)
    widths = (A_Q, A_KV, A_KV, B_W, B_W, B_W)
    return pl.pallas_call(
        _proj_kernel,
        grid=(t // tm,),
        in_specs=[pl.BlockSpec((tm, d), row), pl.BlockSpec((1, d), const),
                  pl.BlockSpec(w_in.shape, const)] + [pl.BlockSpec((tm, LANES), tab)] * 3,
        out_specs=[pl.BlockSpec((w // LANES, tm, LANES), slab) if w > LANES else pl.BlockSpec((tm, w), row)
                   for w in widths],
        out_shape=[jax.ShapeDtypeStruct((w // LANES, t, LANES) if w > LANES else (t, w), F32) for w in widths],
        compiler_params=_cparams(("parallel",), VMEM_LIMIT),
        name="proj",
    )(x2, g_mix.reshape(1, d), w_in.astype(BF16), *tables)


def _band_bias(bias_ref, qb, kw, hw):
    kk = lax.broadcasted_iota(jnp.int32, (kw, qb), 0)
    qq = lax.broadcasted_iota(jnp.int32, (kw, qb), 1)
    for t, delta in enumerate((0, -hw, -2 * hw)):
        bias_ref[t] = jnp.where(jnp.abs(kk + delta - qq) <= hw, 0.0, NEG_INF)


def _band_scratch(qb, kw, slots):
    n = 2 * slots
    return ([pltpu.VMEM((kw, 2 * qb), F32)] * n + [pltpu.VMEM((kw, 2 * qb), BF16)] * n
            + [pltpu.VMEM((SUBLANES, 2 * qb), F32)] * n + [pltpu.VMEM((LANES, kw), BF16)] * n
            + [pltpu.VMEM((SUBLANES, 2 * qb), F32)] * n)


def _band_pipeline(q_ref, k_ref, v_ref, bias_ref, o_ref, lse_ref, bufs, *, nblocks, geometry, stage_vt,
                   sink_row=None):
    qb = ATT_QB
    n = len(bufs) // 5
    slots = n // 2
    s_bufs, p_bufs, ml_bufs, vt_bufs, mx_bufs = (bufs[i * n:(i + 1) * n] for i in range(5))
    kw = s_bufs[0].shape[0]
    for s_buf, p_buf, ml_buf, vt_buf, mx_buf in zip(s_bufs, p_bufs, ml_bufs, vt_bufs, mx_bufs):
        mx_buf[...] = jnp.zeros_like(mx_buf)
        s_buf[...] = jnp.zeros_like(s_buf)
        p_buf[...] = jnp.zeros_like(p_buf)
        ml_buf[...] = jnp.ones_like(ml_buf)
        vt_buf[...] = jnp.zeros_like(vt_buf)

    def rows(block):
        q0, k0, kind = geometry(jnp.clip(block, 0, nblocks - 1))
        return pl.ds(pl.multiple_of(q0, BF16_ROWS), qb), pl.ds(pl.multiple_of(k0, BF16_ROWS), kw), kind

    def scores(block, s_buf, mx_buf):
        q_rows, k_rows, kind = rows(block)
        q = q_ref[q_rows, :]
        lo = lax.broadcasted_iota(jnp.int32, q.shape, 1) < HEAD_DIM
        q2 = jnp.concatenate([jnp.where(lo, q, 0.0), jnp.where(lo, 0.0, q)], axis=0).astype(BF16)
        s = lax.dot_general(k_ref[k_rows, :].astype(BF16), q2, (((1,), (1,)), ((), ())),
                            preferred_element_type=F32)
        bias = bias_ref[kind]
        for hh in range(2):
            s_h = s[:, hh * qb:(hh + 1) * qb] + bias
            s_buf[:, hh * qb:(hh + 1) * qb] = s_h
            mx_buf[0:1, hh * qb:(hh + 1) * qb] = jnp.max(s_h, axis=0, keepdims=True)

    def softmax(block, s_buf, mx_buf, p_buf, ml_buf, vt_buf):
        if stage_vt:
            _, k_rows, _ = rows(block)
            vt_buf[...] = v_ref[k_rows, :].astype(F32).T.astype(BF16)
        m = mx_buf[0:1, :]
        if sink_row is not None:
            m = jnp.maximum(m, sink_row)
        p = jnp.exp2(s_buf[...] - m)
        l = jnp.sum(p, axis=0, keepdims=True)
        if sink_row is not None:
            l = l + jnp.exp2(sink_row - m)
        p_buf[...] = p.astype(BF16)
        ml_buf[0:1, :] = m
        ml_buf[1:2, :] = l

    def output(block, p_buf, ml_buf, vt_buf):
        q_rows, k_rows, _ = rows(block)
        if stage_vt:
            o_t = jnp.dot(vt_buf[...], p_buf[...], preferred_element_type=F32)
        else:
            o_t = lax.dot_general(v_ref[k_rows, :].astype(BF16), p_buf[...], (((0,), (0,)), ((), ())),
                                  preferred_element_type=F32)
        o_tc = jnp.concatenate([o_t[:HEAD_DIM, :qb], o_t[HEAD_DIM:, qb:]], axis=0)
        top = lax.broadcasted_iota(jnp.int32, (LANES, qb), 0) < HEAD_DIM
        m, l = ml_buf[0:1, :], ml_buf[1:2, :]
        inv = 1.0 / l
        o_ref[q_rows, :] = (o_tc * jnp.where(top, inv[:, :qb], inv[:, qb:])).T
        if lse_ref is not None:
            lse = m + jnp.log2(l)
            lse_ref[q_rows, :] = jnp.where(top, lse[:, :qb], lse[:, qb:]).T

    def trip_pair(jj, carry):
        for parity in range(2):
            for slot in range(slots):
                t = (2 * jj + parity) * slots + slot
                rd, wr = (1 - parity) * slots + slot, parity * slots + slot
                output(t - 2 * slots, p_bufs[rd], ml_bufs[rd], vt_bufs[rd])
                softmax(t - slots, s_bufs[rd], mx_bufs[rd], p_bufs[wr], ml_bufs[wr], vt_bufs[wr])
                scores(t, s_bufs[wr], mx_bufs[wr])
        return carry

    trips = nblocks // slots + 2
    assert nblocks % slots == 0 and trips % 2 == 0
    lax.fori_loop(0, trips // 2, trip_pair, 0)


def _segment_geometry(block, n, hw, base=0):
    qb = ATT_QB
    nblk = n // qb
    seg, jb = block // nblk, block % nblk
    qs = jb * qb
    ks = jnp.clip(qs - hw, 0, n - (qb + 2 * hw))
    kind = jnp.where(jb == 0, 0, jnp.where(jb == nblk - 1, 2, 1))
    return base + seg * n + qs, base + seg * n + ks, kind


def _attn_a_kernel(sink_ref, q_ref, k_ref, v_ref, o_ref, bias_ref, *bufs, seq):
    j = pl.program_id(1)
    qb, hw = ATT_QB, A_HALF_WINDOW
    _band_bias(bias_ref, qb, qb + 2 * hw, hw)
    col = lax.broadcasted_iota(jnp.int32, (1, 2 * qb), 1)
    sink_row = jnp.where(col < qb, sink_ref[j], sink_ref[A_GROUP + j]) * LOG2_E
    _band_pipeline(q_ref, k_ref, v_ref, bias_ref, o_ref, None, bufs, nblocks=seq // qb,
                   geometry=functools.partial(_segment_geometry, n=seq, hw=hw), stage_vt=False,
                   sink_row=sink_row)


def _attn_a(qa, ka, va, sink, batch, seq):
    qb, hw = ATT_QB, A_HALF_WINDOW
    assert seq % qb == 0 and seq >= qb + 2 * hw and A_KV == LANES
    qspec = pl.BlockSpec((None, seq, LANES), lambda b, j: (j, b, 0))
    kvspec = pl.BlockSpec((seq, LANES), lambda b, j: (b, 0))
    return pl.pallas_call(
        functools.partial(_attn_a_kernel, seq=seq),
        grid=(batch, A_Q // LANES),
        in_specs=[pl.BlockSpec(memory_space=pltpu.SMEM), qspec, kvspec, kvspec],
        out_specs=qspec,
        out_shape=jax.ShapeDtypeStruct(qa.shape, F32),
        scratch_shapes=[pltpu.VMEM((3, qb + 2 * hw, qb), F32)] + _band_scratch(qb, qb + 2 * hw, ATT_A_SLOTS),
        compiler_params=_cparams(("parallel", "parallel"), VMEM_LIMIT),
        name="attn_a",
    )(sink, qa, ka, va)


def _attn_b_kernel(q_ref, k_ref, v_ref, out_ref, bias_ref, tmp_ref, qs_ref, ks_ref, vs_ref, os_ref, ls_ref,
                   *bufs, seq):
    (w1, d1), (w4, d4), (w16, d16) = DILATED_PATTERNS
    hw = w1 // (2 * d1)
    qb = ATT_QB
    _band_bias(bias_ref, qb, qb + 2 * hw, hw)
    n4, n16 = seq // d4, seq // d16
    step = d16 // d4
    chunk = 512
    for src, dst in ((q_ref, qs_ref), (k_ref, ks_ref), (v_ref, vs_ref)):
        for c in range(seq // chunk):
            dst[c * chunk:(c + 1) * chunk, :] = src[c * chunk:(c + 1) * chunk, :].astype(dst.dtype)
        for r in range(d4):
            tmp_ref[r * n4:(r + 1) * n4, :] = src[pl.ds(r, n4, stride=d4), :]
        for c in range(seq // chunk):
            dst[seq + c * chunk:seq + (c + 1) * chunk, :] = tmp_ref[c * chunk:(c + 1) * chunk, :].astype(dst.dtype)
        for r in range(d16):
            dst[2 * seq + r * n16:2 * seq + (r + 1) * n16, :] = tmp_ref[
                pl.ds((r % d4) * n4 + r // d4, n16, stride=step), :].astype(dst.dtype)

    bps = seq // qb
    lg = [(seq // d // qb).bit_length() - 1 for (_, d) in DILATED_PATTERNS]

    def geometry(block):
        pat, within = block // bps, block % bps
        sh = jnp.where(pat == 0, lg[0], jnp.where(pat == 1, lg[1], lg[2]))
        seg, jb = within >> sh, within & ((1 << sh) - 1)
        n = qb << sh
        qs = jb * qb
        ks = jnp.clip(qs - hw, 0, n - (qb + 2 * hw))
        kind = jnp.where(jb == 0, 0, jnp.where(jb == (1 << sh) - 1, 2, 1))
        base = pat * seq + seg * n
        return base + qs, base + ks, kind

    _band_pipeline(qs_ref, ks_ref, vs_ref, bias_ref, os_ref, ls_ref, bufs,
                   nblocks=len(DILATED_PATTERNS) * bps, geometry=geometry, stage_vt=True)

    def merge(r, carry):
        rows16 = pl.ds(pl.multiple_of(2 * seq + r * n16, SUBLANES), n16)
        rows4 = pl.ds(seq + (r % d4) * n4 + r // d4, n16, stride=step)
        rows1 = pl.ds(r, n16, stride=d16)
        la, lb, lc = ls_ref[rows1, :], ls_ref[rows4, :], ls_ref[rows16, :]
        top = jnp.maximum(jnp.maximum(la, lb), lc)
        wa, wb, wc = jnp.exp2(la - top), jnp.exp2(lb - top), jnp.exp2(lc - top)
        num = wa * os_ref[rows1, :] + wb * os_ref[rows4, :] + wc * os_ref[rows16, :]
        out_ref[rows1, :] = num * (1.0 / (wa + wb + wc))
        return carry

    lax.fori_loop(0, d16, merge, 0)


def _attn_b(qb_, kb, vb, batch, seq):
    (w1, d1), (w4, d4), (w16, d16) = DILATED_PATTERNS
    hw = w1 // (2 * d1)
    qb = ATT_QB
    assert d1 == 1 and d16 % d4 == 0 and w4 // (2 * d4) == hw and w16 // (2 * d16) == hw
    assert seq % 512 == 0 and seq // d16 >= qb + 2 * hw
    for (_, d) in DILATED_PATTERNS:
        nblk = seq // d // qb
        assert nblk * d * qb == seq and nblk & (nblk - 1) == 0
    spec = pl.BlockSpec((None, seq, LANES), lambda b, p: (p, b, 0))
    npat = len(DILATED_PATTERNS)
    return pl.pallas_call(
        functools.partial(_attn_b_kernel, seq=seq),
        grid=(batch, B_W // LANES),
        in_specs=[spec] * 3,
        out_specs=spec,
        out_shape=jax.ShapeDtypeStruct(qb_.shape, F32),
        scratch_shapes=[pltpu.VMEM((3, qb + 2 * hw, qb), F32), pltpu.VMEM((seq, LANES), F32),
                        pltpu.VMEM((npat * seq, LANES), F32), pltpu.VMEM((npat * seq, LANES), BF16),
                        pltpu.VMEM((npat * seq, LANES), BF16), pltpu.VMEM((npat * seq, LANES), F32),
                        pltpu.VMEM((npat * seq, LANES), F32)] + _band_scratch(qb, qb + 2 * hw, ATT_B_SLOTS),
        compiler_params=_cparams(("parallel", "parallel"), VMEM_LIMIT),
        name="attn_b",
    )(qb_, kb, vb)


def _mix_kernel(x_ref, oa_ref, ob_ref, ga_ref, gb_ref, wo_ref, gf_ref, wr_ref,
                x1_ref, h2_ref, aff_ref):
    oa = jnp.concatenate([oa_ref[c] for c in range(oa_ref.shape[0])], axis=1)
    ob = jnp.concatenate([ob_ref[c] for c in range(ob_ref.shape[0])], axis=1)
    na = _rms(oa, ga_ref[...]).astype(BF16)
    nb = _rms(ob, gb_ref[...]).astype(BF16)
    x1 = (x_ref[...] + jnp.dot(na, wo_ref[:A_Q, :], preferred_element_type=F32)
          + jnp.dot(nb, wo_ref[A_Q:, :], preferred_element_type=F32))
    x1_ref[...] = x1
    h2 = _rms(x1, gf_ref[...])
    tm = h2.shape[0]
    for c in range(h2.shape[1] // LANES):
        h2_ref[pl.ds(c, tm, stride=SUBLANES), :] = h2[:, c * LANES:(c + 1) * LANES]
    ne = aff_ref.shape[0]
    dn = (((1,), (1,)), ((), ()))
    h_hi = h2.astype(BF16)
    h_mid = (h2 - h_hi.astype(F32)).astype(BF16)
    r = lax.dot_general(wr_ref[...], h_hi, dn, preferred_element_type=F32)
    logits = r[:ne] + r[ne:] + lax.dot_general(wr_ref[:ne, :], h_mid, dn, preferred_element_type=F32)
    e = jnp.exp(logits - jnp.max(logits, axis=0, keepdims=True))
    aff_ref[...] = e / jnp.sum(e, axis=0, keepdims=True)


def _mix(x2, oa, ob, g_out_a, g_out_b, w_out, g_ffn, w_router, batch, seq):
    t, d = x2.shape
    tm = MIX_TM
    assert seq % tm == 0 and d % LANES == 0 and d // LANES == SUBLANES
    ns = seq // tm
    row = lambda i: (i, 0)
    const = lambda i: (0, 0)
    ne = w_router.shape[1]
    w_hi = w_router.T.astype(BF16)
    w_mid = (w_router.T - w_hi.astype(F32)).astype(BF16)
    wr = jnp.concatenate([w_hi, w_mid], axis=0)
    return pl.pallas_call(
        _mix_kernel,
        grid=(t // tm,),
        in_specs=[pl.BlockSpec((tm, d), row), pl.BlockSpec((A_Q // LANES, tm, LANES), lambda i: (0, i, 0)),
                  pl.BlockSpec((B_W // LANES, tm, LANES), lambda i: (0, i, 0)),
                  pl.BlockSpec((1, A_Q), const), pl.BlockSpec((1, B_W), const),
                  pl.BlockSpec(w_out.shape, const), pl.BlockSpec((1, d), const),
                  pl.BlockSpec((2 * ne, d), const)],
        out_specs=[pl.BlockSpec((tm, d), row), pl.BlockSpec((tm * SUBLANES, LANES), row),
                   pl.BlockSpec((None, ne, tm), lambda i: (i // ns, 0, i % ns))],
        out_shape=[jax.ShapeDtypeStruct((t, d), F32),
                   jax.ShapeDtypeStruct((t * SUBLANES, LANES), F32),
                   jax.ShapeDtypeStruct((batch, ne, seq), F32)],
        compiler_params=_cparams(("parallel",), VMEM_LIMIT),
        name="mix",
    )(x2, oa, ob, g_out_a.reshape(1, -1), g_out_b.reshape(1, -1), w_out.astype(BF16),
      g_ffn.reshape(1, d), wr)


def _lane_cumsum(mask_f, tri):
    rows, s = mask_f.shape
    carry = jnp.zeros((rows, 1), F32)
    out = []
    for c in range(s // LANES):
        inc = jnp.dot(mask_f[:, c * LANES:(c + 1) * LANES].astype(BF16), tri,
                      preferred_element_type=F32) + carry
        out.append(inc)
        carry = inc[:, LANES - 1:LANES]
    return jnp.concatenate(out, axis=1)


def _route_kernel(aff_ref, idx_ref, gate_ref, *, cap, seq):
    b = pl.program_id(0)
    aff = aff_ref[...]
    ne = aff.shape[0]

    def search(i, thr):
        cand = thr | (jnp.int32(1) << (30 - i))
        cnt = jnp.sum((aff >= lax.bitcast_convert_type(cand, F32)).astype(F32), axis=1, keepdims=True)
        return jnp.where(cnt >= cap, cand, thr)

    thr = lax.bitcast_convert_type(lax.fori_loop(0, 31, search, jnp.zeros((ne, 1), jnp.int32)), F32)
    gt = aff > thr
    eq = aff == thr
    n_gt = jnp.sum(gt.astype(F32), axis=1, keepdims=True)
    ri = lax.broadcasted_iota(jnp.int32, (LANES, LANES), 0)
    ci = lax.broadcasted_iota(jnp.int32, (LANES, LANES), 1)
    tri = (ri <= ci).astype(BF16)
    eq_f = eq.astype(F32)
    eq_rank = _lane_cumsum(eq_f, tri) - eq_f
    sel = gt | (eq & (eq_rank < cap - n_gt))
    sel_f = sel.astype(F32)
    slot = _lane_cumsum(sel_f, tri) - sel_f
    key_t = jnp.where(sel, slot, -1.0).T
    tok = lax.broadcasted_iota(jnp.int32, (SUBLANES, seq), 1)
    piece = lax.broadcasted_iota(jnp.int32, (SUBLANES, seq), 0)
    tok_rows = jnp.where(piece == 0, tok // 64, jnp.where(piece == 1, tok % 64, 0)).astype(F32)
    p_iota = lax.broadcasted_iota(jnp.int32, (1, cap), 1).astype(F32)
    for e in range(ne):
        onehot = (key_t[:, e:e + 1] == p_iota).astype(BF16)
        g = aff[e:e + 1, :]
        g_hi = g.astype(BF16).astype(F32)
        g_mid = (g - g_hi).astype(BF16).astype(F32)
        g_lo = g - g_hi - g_mid
        lhs = jnp.where(piece == 2, g_hi, jnp.where(piece == 3, g_mid, jnp.where(piece == 4, g_lo, tok_rows)))
        res = jnp.dot(lhs.astype(BF16), onehot, preferred_element_type=F32)
        idx_ref[e:e + 1, :] = (res[0:1] * 64.0 + res[1:2]).astype(jnp.int32) + b * seq
        gate_ref[e:e + 1, :] = res[2:3] + res[3:4] + res[4:5]


def _route(aff_t, cap):
    batch, ne, seq = aff_t.shape
    assert seq % LANES == 0 and seq <= 64 * 256
    return pl.pallas_call(
        functools.partial(_route_kernel, cap=cap, seq=seq),
        grid=(batch,),
        in_specs=[pl.BlockSpec((None, ne, seq), lambda b: (b, 0, 0))],
        out_specs=[pl.BlockSpec((None, ne, cap), lambda b: (b, 0, 0))] * 2,
        out_shape=[jax.ShapeDtypeStruct((batch, ne, cap), jnp.int32),
                   jax.ShapeDtypeStruct((batch, ne, cap), F32)],
        compiler_params=_cparams(("parallel",), VMEM_LIMIT),
        name="route",
    )(aff_t)


def _experts_kernel(idx_ref, idx_next_ref, h_hbm, wg_ref, wu_ref, wd_ref, y_ref, rows_ref, xe_ref, sems,
                    *, ne, nf):
    e, f = pl.program_id(0), pl.program_id(1)
    ntok, d = xe_ref.shape
    slot = e % 2
    chunks = ntok // FFN_TM
    per_chunk = ntok // (nf * chunks)

    def row_copy(idx_smem, i, buf):
        src = pl.multiple_of(idx_smem[0, 0, i] * SUBLANES, SUBLANES)
        dst = pl.multiple_of(i * SUBLANES, SUBLANES)
        return pltpu.make_async_copy(h_hbm.at[pl.ds(src, SUBLANES), :],
                                     rows_ref.at[buf, pl.ds(dst, SUBLANES), :], sems.at[buf])

    def wait_rows(buf):
        pltpu.make_async_copy(h_hbm.at[pl.ds(0, ntok * SUBLANES), :], rows_ref.at[buf], sems.at[buf]).wait()

    @pl.when((e == 0) & (f == 0))
    def _first_gather():
        def issue(i, carry):
            row_copy(idx_ref, i, 0).start()
            return carry

        lax.fori_loop(0, ntok, issue, 0, unroll=8)

    @pl.when(f == 0)
    def _repack():
        wait_rows(slot)
        for m in range(chunks):
            xe_ref[m * FFN_TM:(m + 1) * FFN_TM, :] = jnp.concatenate(
                [rows_ref[slot, pl.ds(m * FFN_TM * SUBLANES + c, FFN_TM, stride=SUBLANES), :]
                 for c in range(d // LANES)], axis=1).astype(BF16)
        y_ref[...] = jnp.zeros_like(y_ref)

    wg = wg_ref[...].astype(BF16)
    wu = wu_ref[...].astype(BF16)
    wd = wd_ref[...].astype(BF16)
    for m in range(chunks):
        xe = xe_ref[m * FFN_TM:(m + 1) * FFN_TM, :]
        a = jnp.dot(xe, wg, preferred_element_type=F32)
        u = jnp.dot(xe, wu, preferred_element_type=F32)
        hmid = (a * jax.nn.sigmoid(a) * u).astype(BF16)
        y_ref[m * FFN_TM:(m + 1) * FFN_TM, :] += jnp.dot(hmid, wd, preferred_element_type=F32)
        for i in range(per_chunk):
            row_copy(idx_next_ref, (f * chunks + m) * per_chunk + i, 1 - slot).start()

    @pl.when((e == ne - 1) & (f == nf - 1))
    def _drain():
        wait_rows(1 - slot)


def _experts(idx_e, h2_rows, w_gate, w_up, w_down):
    ne, d, ff = w_gate.shape
    ntok = idx_e.shape[-1]
    tf = FFN_TF
    nf = ff // tf
    assert ff % tf == 0 and ntok % (FFN_TM * nf) == 0 and d // LANES == SUBLANES
    return pl.pallas_call(
        functools.partial(_experts_kernel, ne=ne, nf=nf),
        grid=(ne, nf),
        in_specs=[pl.BlockSpec((1, 1, ntok), lambda e, f: (e, 0, 0), memory_space=pltpu.SMEM),
                  pl.BlockSpec((1, 1, ntok), lambda e, f: (jnp.minimum(e + 1, ne - 1), 0, 0),
                               memory_space=pltpu.SMEM),
                  pl.BlockSpec(memory_space=pl.ANY),
                  pl.BlockSpec((None, d, tf), lambda e, f: (e, 0, f)),
                  pl.BlockSpec((None, d, tf), lambda e, f: (e, 0, f)),
                  pl.BlockSpec((None, tf, d), lambda e, f: (e, f, 0))],
        out_specs=pl.BlockSpec((None, ntok, d), lambda e, f: (e, 0, 0)),
        out_shape=jax.ShapeDtypeStruct((ne, ntok, d), F32),
        scratch_shapes=[pltpu.VMEM((2, ntok * SUBLANES, LANES), F32), pltpu.VMEM((ntok, d), BF16),
                        pltpu.SemaphoreType.DMA((2,))],
        compiler_params=_cparams(("arbitrary", "arbitrary"), VMEM_LIMIT),
        name="experts",
    )(idx_e, idx_e, h2_rows, w_gate, w_up, w_down)


def _combine_kernel(idx_ref, gate_ref, y_ref, x1_ref, g_ref, o_ref, acc_ref, ybuf_ref, *, batch):
    b, e = pl.program_id(0), pl.program_id(1)
    cap, d = y_ref.shape
    tm = x1_ref.shape[0]
    cur = b % 2

    @pl.when(b < batch)
    def _scatter():
        @pl.when(e == 0)
        def _init():
            acc_ref[cur] = jnp.zeros(acc_ref.shape[1:], F32)

        for c in range(d // LANES):
            ybuf_ref[pl.ds(c, cap, stride=SUBLANES), :] = y_ref[:, c * LANES:(c + 1) * LANES]

        def group(k, carry):
            rows, sums = [], []
            for u in range(RMW_UNROLL):
                i = k * RMW_UNROLL + u
                t = pl.ds(pl.multiple_of(idx_ref[0, 0, i] * SUBLANES, SUBLANES), SUBLANES)
                yi = ybuf_ref[pl.ds(pl.multiple_of(i * SUBLANES, SUBLANES), SUBLANES), :]
                rows.append(t)
                sums.append(acc_ref[cur, t, :] + gate_ref[0, 0, i] * yi)
            for t, s in zip(rows, sums):
                acc_ref[cur, t, :] = s
            return carry

        lax.fori_loop(0, cap // RMW_UNROLL, group, 0)

    @pl.when(b > 0)
    def _emit():
        base = pl.multiple_of(e * tm * SUBLANES, SUBLANES)
        moe = jnp.concatenate([acc_ref[1 - cur, pl.ds(base + c, tm, stride=SUBLANES), :]
                               for c in range(d // LANES)], axis=1)
        o_ref[...] = _rms(x1_ref[...] + moe, g_ref[...])


def _combine(idx_l, gates, y, x1, g_final, batch, seq):
    ne = y.shape[0]
    t, d = x1.shape
    cap = idx_l.shape[-1]
    tm = seq // ne
    assert cap % RMW_UNROLL == 0 and d // LANES == SUBLANES and tm * ne == seq and tm % SUBLANES == 0
    scattered = lambda b: jnp.minimum(b, batch - 1)
    rows = lambda b, e: (jnp.maximum(b - 1, 0) * ne + jnp.where(b > 0, e, 0), 0)
    smem = pl.BlockSpec((1, 1, cap), lambda b, e: (scattered(b) * ne + e, 0, 0), memory_space=pltpu.SMEM)
    return pl.pallas_call(
        functools.partial(_combine_kernel, batch=batch),
        grid=(batch + 1, ne),
        in_specs=[smem, smem, pl.BlockSpec((None, cap, d), lambda b, e: (e, scattered(b), 0)),
                  pl.BlockSpec((tm, d), rows), pl.BlockSpec((1, d), lambda b, e: (0, 0))],
        out_specs=pl.BlockSpec((tm, d), rows),
        out_shape=jax.ShapeDtypeStruct((t, d), F32),
        scratch_shapes=[pltpu.VMEM((2, seq * SUBLANES, LANES), F32), pltpu.VMEM((cap * SUBLANES, LANES), F32)],
        compiler_params=_cparams(("arbitrary", "arbitrary"), VMEM_LIMIT),
        name="combine",
    )(idx_l.reshape(batch * ne, 1, cap), gates.reshape(batch * ne, 1, cap), y, x1, g_final.reshape(1, d))


def _permute_a_heads(w_in, g_out_a, w_out):
    cols = jnp.asarray([h * HEAD_DIM + i for h in A_HEAD_ORDER for i in range(HEAD_DIM)], jnp.int32)
    w_in = jnp.concatenate([w_in[:, :A_Q][:, cols], w_in[:, A_Q:]], axis=1)
    w_out = jnp.concatenate([w_out[:A_Q][cols], w_out[A_Q:]], axis=0)
    return w_in, g_out_a[cols], w_out


def kernel(x, g_mix, w_in, a_sink, g_out_a, g_out_b, w_out, g_ffn, w_router, w_gate, w_up, w_down, g_final):
    batch, seq, d = x.shape
    depth = w_in.shape[0]
    ne = w_router.shape[-1]
    cap = CAPACITY_FACTOR * seq // ne
    assert depth == 1, "the last kernel fuses the MoE residual with the output norm of a single layer"
    l = 0
    x2 = x.reshape(batch * seq, d)
    w_in_p, g_out_a_p, w_out_p = _permute_a_heads(w_in[l], g_out_a[l], w_out[l])
    qa, ka, va, qb, kb, vb = _proj(x2, g_mix[l], w_in_p, seq)
    oa = _attn_a(qa, ka, va, a_sink[l], batch, seq)
    ob = _attn_b(qb, kb, vb, batch, seq)
    x1, h2_rows, aff_t = _mix(x2, oa, ob, g_out_a_p, g_out_b[l], w_out_p, g_ffn[l], w_router[l], batch, seq)
    idx_g, gates = _route(aff_t, cap)
    idx_e = idx_g.transpose(1, 0, 2).reshape(ne, 1, batch * cap)
    y = _experts(idx_e, h2_rows, w_gate[l], w_up[l], w_down[l])
    idx_l = idx_g - (jnp.arange(batch, dtype=jnp.int32) * seq)[:, None, None]
    out = _combine(idx_l, gates, y, x1, g_final, batch, seq)
    return out.reshape(batch, seq, d)
```

```python
import functools

import jax
import jax.numpy as jnp
from jax import lax
from jax.experimental import pallas as pl
from jax.experimental.pallas import tpu as pltpu

F32 = jnp.float32
BF16 = jnp.bfloat16

LANES = 128
SUBLANES = 8
BF16_ROWS = 16
VMEM_LIMIT = 56 * 1024 * 1024

HEAD_DIM = 64
A_HEADS = 8
A_KV_HEADS = 2
A_GROUP = A_HEADS // A_KV_HEADS
A_HALF_WINDOW = 128
B_HEADS = 8
DILATED_PATTERNS = ((128, 1), (512, 4), (2048, 16))
A_Q = A_HEADS * HEAD_DIM
A_KV = A_KV_HEADS * HEAD_DIM
B_W = B_HEADS * HEAD_DIM
ROPE_THETA = 500000.0
ROT_DIM = HEAD_DIM // 4
N_EXPERTS = 16
CAPACITY_FACTOR = 2
NORM_EPS = 1e-6
NEG_INF = -1e30
LOG2_E = 1.4426950408889634

PROJ_TM = 1024
ATT_QB = 128
ATT_A_SLOTS = 2
ATT_B_SLOTS = 4
MIX_TM = 1024
FFN_TF = 512
FFN_TM = 1024
RMW_UNROLL = 8

A_HEAD_ORDER = tuple(h for j in range(A_GROUP) for h in (j, A_GROUP + j))


def _cparams(sem, vmem=None):
    return pltpu.CompilerParams(dimension_semantics=sem, vmem_limit_bytes=vmem)


def _rope_tables(seq_len):
    half = ROT_DIM // 2
    inv_freq = ROPE_THETA ** (-jnp.arange(0, ROT_DIM, 2, dtype=F32) / ROT_DIM)
    ang = jnp.arange(seq_len, dtype=F32)[:, None] * inv_freq[None, :]
    cos, sin = jnp.cos(ang), jnp.sin(ang)
    ones = jnp.ones((seq_len, HEAD_DIM - ROT_DIM), F32)
    zeros = jnp.zeros((seq_len, HEAD_DIM - ROT_DIM), F32)
    zh = jnp.zeros((seq_len, half), F32)
    cos_t = jnp.concatenate([cos, cos, ones], axis=1)
    sin_a = jnp.concatenate([-sin, zh, zeros], axis=1)
    sin_b = jnp.concatenate([zh, sin, zeros], axis=1)
    rep = LANES // HEAD_DIM
    return tuple(jnp.tile(t, (1, rep)) for t in (cos_t, sin_a, sin_b))


def _rms(x, g):
    return x * lax.rsqrt(jnp.mean(x * x, axis=-1, keepdims=True) + NORM_EPS) * g


def _proj_kernel(x_ref, g_ref, w_ref, cos_ref, sa_ref, sb_ref,
                 qa_ref, ka_ref, va_ref, qb_ref, kb_ref, vb_ref):
    h = _rms(x_ref[...], g_ref[...]).astype(BF16)
    cos, sa, sb = cos_ref[...], sa_ref[...], sb_ref[...]
    half = ROT_DIM // 2

    def rope(t):
        return t * cos + pltpu.roll(t, LANES - half, 1) * sa + pltpu.roll(t, half, 1) * sb

    def section(col0, width):
        return jnp.dot(h, w_ref[:, col0:col0 + width], preferred_element_type=F32)

    scale = HEAD_DIM ** -0.5 * LOG2_E
    qa = section(0, A_Q)
    for c in range(A_Q // LANES):
        qa_ref[c] = (rope(qa[:, c * LANES:(c + 1) * LANES]) * scale).astype(qa_ref.dtype)
    kv = section(A_Q, 2 * A_KV)
    ka_ref[...] = rope(kv[:, :A_KV]).astype(ka_ref.dtype)
    va_ref[...] = kv[:, A_KV:].astype(va_ref.dtype)
    qb = section(A_Q + 2 * A_KV, B_W)
    for c in range(B_W // LANES):
        qb_ref[c] = rope(qb[:, c * LANES:(c + 1) * LANES]) * scale
    kb = section(A_Q + 2 * A_KV + B_W, B_W)
    for c in range(B_W // LANES):
        kb_ref[c] = rope(kb[:, c * LANES:(c + 1) * LANES])
    vb = section(A_Q + 2 * A_KV + 2 * B_W, B_W)
    for c in range(B_W // LANES):
        vb_ref[c] = vb[:, c * LANES:(c + 1) * LANES]


def _proj(x2, g_mix, w_in, seq):
    t, d = x2.shape
    tm = PROJ_TM
    assert t % tm == 0 and seq % tm == 0 and A_KV == LANES
    nseq = seq // tm
    tables = _rope_tables(seq)
    row = lambda i: (i, 0)
    slab = lambda i: (0, i, 0)
    const = lambda i: (0, 0)
    tab = lambda i: (i % nseq, 0)
    widths = (A_Q, A_KV, A_KV, B_W, B_W, B_W)
    return pl.pallas_call(
        _proj_kernel,
        grid=(t // tm,),
        in_specs=[pl.BlockSpec((tm, d), row), pl.BlockSpec((1, d), const),
                  pl.BlockSpec(w_in.shape, const)] + [pl.BlockSpec((tm, LANES), tab)] * 3,
        out_specs=[pl.BlockSpec((w // LANES, tm, LANES), slab) if w > LANES else pl.BlockSpec((tm, w), row)
                   for w in widths],
        out_shape=[jax.ShapeDtypeStruct((w // LANES, t, LANES) if w > LANES else (t, w), dt)
                   for w, dt in zip(widths, (BF16, BF16, BF16, F32, F32, F32))],
        compiler_params=_cparams(("parallel",), VMEM_LIMIT),
        name="proj",
    )(x2, g_mix.reshape(1, d), w_in.astype(BF16), *tables)


def _band_bias(bias_ref, qb, kw, hw):
    kk = lax.broadcasted_iota(jnp.int32, (kw, qb), 0)
    qq = lax.broadcasted_iota(jnp.int32, (kw, qb), 1)
    for t, delta in enumerate((0, -hw, -2 * hw)):
        bias_ref[t] = jnp.where(jnp.abs(kk + delta - qq) <= hw, 0.0, NEG_INF)


def _band_scratch(qb, kw, slots):
    n = 2 * slots
    return ([pltpu.VMEM((kw, 2 * qb), F32)] * n + [pltpu.VMEM((kw, 2 * qb), BF16)] * n
            + [pltpu.VMEM((SUBLANES, 2 * qb), F32)] * n + [pltpu.VMEM((LANES, kw), BF16)] * n
            + [pltpu.VMEM((SUBLANES, 2 * qb), F32)] * n)


def _band_pipeline(q_ref, k_ref, v_ref, bias_ref, o_ref, lse_ref, bufs, *, nblocks, geometry, stage_vt,
                   sink_row=None):
    qb = ATT_QB
    n = len(bufs) // 5
    slots = n // 2
    s_bufs, p_bufs, ml_bufs, vt_bufs, mx_bufs = (bufs[i * n:(i + 1) * n] for i in range(5))
    kw = s_bufs[0].shape[0]
    for s_buf, p_buf, ml_buf, vt_buf, mx_buf in zip(s_bufs, p_bufs, ml_bufs, vt_bufs, mx_bufs):
        mx_buf[...] = jnp.zeros_like(mx_buf)
        s_buf[...] = jnp.zeros_like(s_buf)
        p_buf[...] = jnp.zeros_like(p_buf)
        ml_buf[...] = jnp.ones_like(ml_buf)
        vt_buf[...] = jnp.zeros_like(vt_buf)

    def rows(block):
        q0, k0, kind = geometry(jnp.clip(block, 0, nblocks - 1))
        return pl.ds(pl.multiple_of(q0, BF16_ROWS), qb), pl.ds(pl.multiple_of(k0, BF16_ROWS), kw), kind

    def scores(block, s_buf, mx_buf):
        q_rows, k_rows, kind = rows(block)
        q = q_ref[q_rows, :]
        lo = lax.broadcasted_iota(jnp.int32, q.shape, 1) < HEAD_DIM
        q2 = jnp.concatenate([jnp.where(lo, q, 0.0), jnp.where(lo, 0.0, q)], axis=0).astype(BF16)
        s = lax.dot_general(k_ref[k_rows, :].astype(BF16), q2, (((1,), (1,)), ((), ())),
                            preferred_element_type=F32)
        bias = bias_ref[kind]
        for hh in range(2):
            s_h = s[:, hh * qb:(hh + 1) * qb] + bias
            s_buf[:, hh * qb:(hh + 1) * qb] = s_h
            mx_buf[0:1, hh * qb:(hh + 1) * qb] = jnp.max(s_h, axis=0, keepdims=True)

    def softmax(block, s_buf, mx_buf, p_buf, ml_buf, vt_buf):
        if stage_vt:
            _, k_rows, _ = rows(block)
            vt_buf[...] = v_ref[k_rows, :].astype(F32).T.astype(BF16)
        m = mx_buf[0:1, :]
        if sink_row is not None:
            m = jnp.maximum(m, sink_row)
        p = jnp.exp2(s_buf[...] - m)
        l = jnp.sum(p, axis=0, keepdims=True)
        if sink_row is not None:
            l = l + jnp.exp2(sink_row - m)
        p_buf[...] = p.astype(BF16)
        ml_buf[0:1, :] = m
        ml_buf[1:2, :] = l

    def output(block, p_buf, ml_buf, vt_buf):
        q_rows, k_rows, _ = rows(block)
        if stage_vt:
            o_t = jnp.dot(vt_buf[...], p_buf[...], preferred_element_type=F32)
        else:
            o_t = lax.dot_general(v_ref[k_rows, :].astype(BF16), p_buf[...], (((0,), (0,)), ((), ())),
                                  preferred_element_type=F32)
        o_tc = jnp.concatenate([o_t[:HEAD_DIM, :qb], o_t[HEAD_DIM:, qb:]], axis=0)
        top = lax.broadcasted_iota(jnp.int32, (LANES, qb), 0) < HEAD_DIM
        m, l = ml_buf[0:1, :], ml_buf[1:2, :]
        inv = 1.0 / l
        o_ref[q_rows, :] = (o_tc * jnp.where(top, inv[:, :qb], inv[:, qb:])).T
        if lse_ref is not None:
            lse = m + jnp.log2(l)
            lse_ref[q_rows, :] = jnp.where(top, lse[:, :qb], lse[:, qb:]).T

    def trip_pair(jj, carry):
        for parity in range(2):
            for slot in range(slots):
                t = (2 * jj + parity) * slots + slot
                rd, wr = (1 - parity) * slots + slot, parity * slots + slot
                output(t - 2 * slots, p_bufs[rd], ml_bufs[rd], vt_bufs[rd])
                softmax(t - slots, s_bufs[rd], mx_bufs[rd], p_bufs[wr], ml_bufs[wr], vt_bufs[wr])
                scores(t, s_bufs[wr], mx_bufs[wr])
        return carry

    trips = nblocks // slots + 2
    assert nblocks % slots == 0 and trips % 2 == 0
    lax.fori_loop(0, trips // 2, trip_pair, 0)


def _segment_geometry(block, n, hw, base=0):
    qb = ATT_QB
    nblk = n // qb
    seg, jb = block // nblk, block % nblk
    qs = jb * qb
    ks = jnp.clip(qs - hw, 0, n - (qb + 2 * hw))
    kind = jnp.where(jb == 0, 0, jnp.where(jb == nblk - 1, 2, 1))
    return base + seg * n + qs, base + seg * n + ks, kind


def _attn_a_kernel(sink_ref, q_ref, k_ref, v_ref, o_ref, bias_ref, *bufs, seq):
    j = pl.program_id(1)
    qb, hw = ATT_QB, A_HALF_WINDOW
    _band_bias(bias_ref, qb, qb + 2 * hw, hw)
    col = lax.broadcasted_iota(jnp.int32, (1, 2 * qb), 1)
    sink_row = jnp.where(col < qb, sink_ref[j], sink_ref[A_GROUP + j]) * LOG2_E
    _band_pipeline(q_ref, k_ref, v_ref, bias_ref, o_ref, None, bufs, nblocks=seq // qb,
                   geometry=functools.partial(_segment_geometry, n=seq, hw=hw), stage_vt=False,
                   sink_row=sink_row)


def _attn_a(qa, ka, va, sink, batch, seq):
    qb, hw = ATT_QB, A_HALF_WINDOW
    assert seq % qb == 0 and seq >= qb + 2 * hw and A_KV == LANES
    qspec = pl.BlockSpec((None, seq, LANES), lambda b, j: (j, b, 0))
    kvspec = pl.BlockSpec((seq, LANES), lambda b, j: (b, 0))
    return pl.pallas_call(
        functools.partial(_attn_a_kernel, seq=seq),
        grid=(batch, A_Q // LANES),
        in_specs=[pl.BlockSpec(memory_space=pltpu.SMEM), qspec, kvspec, kvspec],
        out_specs=qspec,
        out_shape=jax.ShapeDtypeStruct(qa.shape, F32),
        scratch_shapes=[pltpu.VMEM((3, qb + 2 * hw, qb), F32)] + _band_scratch(qb, qb + 2 * hw, ATT_A_SLOTS),
        compiler_params=_cparams(("parallel", "parallel"), VMEM_LIMIT),
        name="attn_a",
    )(sink, qa, ka, va)


def _attn_b_kernel(q_ref, k_ref, v_ref, out_ref, bias_ref, tmp_ref, qs_ref, ks_ref, vs_ref, os_ref, ls_ref,
                   *bufs, seq):
    (w1, d1), (w4, d4), (w16, d16) = DILATED_PATTERNS
    hw = w1 // (2 * d1)
    qb = ATT_QB
    _band_bias(bias_ref, qb, qb + 2 * hw, hw)
    n4, n16 = seq // d4, seq // d16
    step = d16 // d4
    chunk = 512
    for src, dst in ((q_ref, qs_ref), (k_ref, ks_ref), (v_ref, vs_ref)):
        for c in range(seq // chunk):
            dst[c * chunk:(c + 1) * chunk, :] = src[c * chunk:(c + 1) * chunk, :].astype(dst.dtype)
        for r in range(d4):
            tmp_ref[r * n4:(r + 1) * n4, :] = src[pl.ds(r, n4, stride=d4), :]
        for c in range(seq // chunk):
            dst[seq + c * chunk:seq + (c + 1) * chunk, :] = tmp_ref[c * chunk:(c + 1) * chunk, :].astype(dst.dtype)
        for r in range(d16):
            dst[2 * seq + r * n16:2 * seq + (r + 1) * n16, :] = tmp_ref[
                pl.ds((r % d4) * n4 + r // d4, n16, stride=step), :].astype(dst.dtype)

    bps = seq // qb
    lg = [(seq // d // qb).bit_length() - 1 for (_, d) in DILATED_PATTERNS]

    def geometry(block):
        pat, within = block // bps, block % bps
        sh = jnp.where(pat == 0, lg[0], jnp.where(pat == 1, lg[1], lg[2]))
        seg, jb = within >> sh, within & ((1 << sh) - 1)
        n = qb << sh
        qs = jb * qb
        ks = jnp.clip(qs - hw, 0, n - (qb + 2 * hw))
        kind = jnp.where(jb == 0, 0, jnp.where(jb == (1 << sh) - 1, 2, 1))
        base = pat * seq + seg * n
        return base + qs, base + ks, kind

    _band_pipeline(qs_ref, ks_ref, vs_ref, bias_ref, os_ref, ls_ref, bufs,
                   nblocks=len(DILATED_PATTERNS) * bps, geometry=geometry, stage_vt=True)

    def merge(r, carry):
        rows16 = pl.ds(pl.multiple_of(2 * seq + r * n16, SUBLANES), n16)
        rows4 = pl.ds(seq + (r % d4) * n4 + r // d4, n16, stride=step)
        rows1 = pl.ds(r, n16, stride=d16)
        la, lb, lc = ls_ref[rows1, :], ls_ref[rows4, :], ls_ref[rows16, :]
        top = jnp.maximum(jnp.maximum(la, lb), lc)
        wa, wb, wc = jnp.exp2(la - top), jnp.exp2(lb - top), jnp.exp2(lc - top)
        num = wa * os_ref[rows1, :] + wb * os_ref[rows4, :] + wc * os_ref[rows16, :]
        out_ref[rows1, :] = num * (1.0 / (wa + wb + wc))
        return carry

    lax.fori_loop(0, d16, merge, 0)


def _attn_b(qb_, kb, vb, batch, seq):
    (w1, d1), (w4, d4), (w16, d16) = DILATED_PATTERNS
    hw = w1 // (2 * d1)
    qb = ATT_QB
    assert d1 == 1 and d16 % d4 == 0 and w4 // (2 * d4) == hw and w16 // (2 * d16) == hw
    assert seq % 512 == 0 and seq // d16 >= qb + 2 * hw
    for (_, d) in DILATED_PATTERNS:
        nblk = seq // d // qb
        assert nblk * d * qb == seq and nblk & (nblk - 1) == 0
    spec = pl.BlockSpec((None, seq, LANES), lambda b, p: (p, b, 0))
    npat = len(DILATED_PATTERNS)
    return pl.pallas_call(
        functools.partial(_attn_b_kernel, seq=seq),
        grid=(batch, B_W // LANES),
        in_specs=[spec] * 3,
        out_specs=spec,
        out_shape=jax.ShapeDtypeStruct(qb_.shape, F32),
        scratch_shapes=[pltpu.VMEM((3, qb + 2 * hw, qb), F32), pltpu.VMEM((seq, LANES), F32),
                        pltpu.VMEM((npat * seq, LANES), F32), pltpu.VMEM((npat * seq, LANES), BF16),
                        pltpu.VMEM((npat * seq, LANES), BF16), pltpu.VMEM((npat * seq, LANES), F32),
                        pltpu.VMEM((npat * seq, LANES), F32)] + _band_scratch(qb, qb + 2 * hw, ATT_B_SLOTS),
        compiler_params=_cparams(("parallel", "parallel"), VMEM_LIMIT),
        name="attn_b",
    )(qb_, kb, vb)


def _mix_kernel(x_ref, oa_ref, ob_ref, ga_ref, gb_ref, wo_ref, gf_ref, wr_ref,
                x1_ref, h2_ref, aff_ref):
    oa = jnp.concatenate([oa_ref[c] for c in range(oa_ref.shape[0])], axis=1)
    ob = jnp.concatenate([ob_ref[c] for c in range(ob_ref.shape[0])], axis=1)
    na = _rms(oa, ga_ref[...]).astype(BF16)
    nb = _rms(ob, gb_ref[...]).astype(BF16)
    x1 = (x_ref[...] + jnp.dot(na, wo_ref[:A_Q, :], preferred_element_type=F32)
          + jnp.dot(nb, wo_ref[A_Q:, :], preferred_element_type=F32))
    x1_ref[...] = x1
    h2 = _rms(x1, gf_ref[...])
    tm = h2.shape[0]
    for c in range(h2.shape[1] // LANES):
        h2_ref[pl.ds(c, tm, stride=SUBLANES), :] = h2[:, c * LANES:(c + 1) * LANES]
    ne = aff_ref.shape[0]
    dn = (((1,), (1,)), ((), ()))
    h_hi = h2.astype(BF16)
    h_mid = (h2 - h_hi.astype(F32)).astype(BF16)
    r = lax.dot_general(wr_ref[...], h_hi, dn, preferred_element_type=F32)
    logits = r[:ne] + r[ne:] + lax.dot_general(wr_ref[:ne, :], h_mid, dn, preferred_element_type=F32)
    e = jnp.exp(logits - jnp.max(logits, axis=0, keepdims=True))
    aff_ref[...] = e / jnp.sum(e, axis=0, keepdims=True)


def _mix(x2, oa, ob, g_out_a, g_out_b, w_out, g_ffn, w_router, batch, seq):
    t, d = x2.shape
    tm = MIX_TM
    assert seq % tm == 0 and d % LANES == 0 and d // LANES == SUBLANES
    ns = seq // tm
    row = lambda i: (i, 0)
    const = lambda i: (0, 0)
    ne = w_router.shape[1]
    w_hi = w_router.T.astype(BF16)
    w_mid = (w_router.T - w_hi.astype(F32)).astype(BF16)
    wr = jnp.concatenate([w_hi, w_mid], axis=0)
    return pl.pallas_call(
        _mix_kernel,
        grid=(t // tm,),
        in_specs=[pl.BlockSpec((tm, d), row), pl.BlockSpec((A_Q // LANES, tm, LANES), lambda i: (0, i, 0)),
                  pl.BlockSpec((B_W // LANES, tm, LANES), lambda i: (0, i, 0)),
                  pl.BlockSpec((1, A_Q), const), pl.BlockSpec((1, B_W), const),
                  pl.BlockSpec(w_out.shape, const), pl.BlockSpec((1, d), const),
                  pl.BlockSpec((2 * ne, d), const)],
        out_specs=[pl.BlockSpec((tm, d), row), pl.BlockSpec((tm * SUBLANES, LANES), row),
                   pl.BlockSpec((None, ne, tm), lambda i: (i // ns, 0, i % ns))],
        out_shape=[jax.ShapeDtypeStruct((t, d), F32),
                   jax.ShapeDtypeStruct((t * SUBLANES, LANES), F32),
                   jax.ShapeDtypeStruct((batch, ne, seq), F32)],
        compiler_params=_cparams(("parallel",), VMEM_LIMIT),
        name="mix",
    )(x2, oa, ob, g_out_a.reshape(1, -1), g_out_b.reshape(1, -1), w_out.astype(BF16),
      g_ffn.reshape(1, d), wr)


def _lane_cumsum(mask_f, tri):
    rows, s = mask_f.shape
    carry = jnp.zeros((rows, 1), F32)
    out = []
    for c in range(s // LANES):
        inc = jnp.dot(mask_f[:, c * LANES:(c + 1) * LANES].astype(BF16), tri,
                      preferred_element_type=F32) + carry
        out.append(inc)
        carry = inc[:, LANES - 1:LANES]
    return jnp.concatenate(out, axis=1)


def _route_kernel(aff_ref, idx_ref, gate_ref, *, cap, seq):
    b = pl.program_id(0)
    aff = aff_ref[...]
    ne = aff.shape[0]

    def search(i, thr):
        cand = thr | (jnp.int32(1) << (30 - i))
        cnt = jnp.sum((aff >= lax.bitcast_convert_type(cand, F32)).astype(F32), axis=1, keepdims=True)
        return jnp.where(cnt >= cap, cand, thr)

    thr = lax.bitcast_convert_type(lax.fori_loop(0, 31, search, jnp.zeros((ne, 1), jnp.int32)), F32)
    gt = aff > thr
    eq = aff == thr
    n_gt = jnp.sum(gt.astype(F32), axis=1, keepdims=True)
    ri = lax.broadcasted_iota(jnp.int32, (LANES, LANES), 0)
    ci = lax.broadcasted_iota(jnp.int32, (LANES, LANES), 1)
    tri = (ri <= ci).astype(BF16)
    eq_f = eq.astype(F32)
    eq_rank = _lane_cumsum(eq_f, tri) - eq_f
    sel = gt | (eq & (eq_rank < cap - n_gt))
    sel_f = sel.astype(F32)
    slot = _lane_cumsum(sel_f, tri) - sel_f
    key_t = jnp.where(sel, slot, -1.0).T
    tok = lax.broadcasted_iota(jnp.int32, (SUBLANES, seq), 1)
    piece = lax.broadcasted_iota(jnp.int32, (SUBLANES, seq), 0)
    tok_rows = jnp.where(piece == 0, tok // 64, jnp.where(piece == 1, tok % 64, 0)).astype(F32)
    p_iota = lax.broadcasted_iota(jnp.int32, (1, cap), 1).astype(F32)
    for e in range(ne):
        onehot = (key_t[:, e:e + 1] == p_iota).astype(BF16)
        g = aff[e:e + 1, :]
        g_hi = g.astype(BF16).astype(F32)
        g_mid = (g - g_hi).astype(BF16).astype(F32)
        g_lo = g - g_hi - g_mid
        lhs = jnp.where(piece == 2, g_hi, jnp.where(piece == 3, g_mid, jnp.where(piece == 4, g_lo, tok_rows)))
        res = jnp.dot(lhs.astype(BF16), onehot, preferred_element_type=F32)
        idx_ref[e:e + 1, :] = ((res[0:1] * 64.0 + res[1:2]).astype(jnp.int32) + b * seq) * SUBLANES
        gate_ref[e:e + 1, :] = res[2:3] + res[3:4] + res[4:5]


def _route(aff_t, cap):
    batch, ne, seq = aff_t.shape
    assert seq % LANES == 0 and seq <= 64 * 256
    return pl.pallas_call(
        functools.partial(_route_kernel, cap=cap, seq=seq),
        grid=(batch,),
        in_specs=[pl.BlockSpec((None, ne, seq), lambda b: (b, 0, 0))],
        out_specs=[pl.BlockSpec((None, ne, cap), lambda b: (b, 0, 0))] * 2,
        out_shape=[jax.ShapeDtypeStruct((batch, ne, cap), jnp.int32),
                   jax.ShapeDtypeStruct((batch, ne, cap), F32)],
        compiler_params=_cparams(("parallel",), VMEM_LIMIT),
        name="route",
    )(aff_t)


def _experts_kernel(idx_ref, idx_next_ref, h_hbm, wg_ref, wu_ref, wd_ref, y_ref, rows_ref, xe_ref, sems,
                    *, ne, nf):
    e, f = pl.program_id(0), pl.program_id(1)
    ntok, d = xe_ref.shape
    slot = e % 2
    chunks = ntok // FFN_TM
    per_chunk = ntok // (nf * chunks)

    def row_copy(idx_smem, i, buf):
        src = pl.multiple_of(idx_smem[0, 0, i], SUBLANES)
        dst = pl.multiple_of(i * SUBLANES, SUBLANES)
        return pltpu.make_async_copy(h_hbm.at[pl.ds(src, SUBLANES), :],
                                     rows_ref.at[buf, pl.ds(dst, SUBLANES), :], sems.at[buf])

    def wait_rows(buf):
        pltpu.make_async_copy(h_hbm.at[pl.ds(0, ntok * SUBLANES), :], rows_ref.at[buf], sems.at[buf]).wait()

    @pl.when((e == 0) & (f == 0))
    def _first_gather():
        def issue(i, carry):
            row_copy(idx_ref, i, 0).start()
            return carry

        lax.fori_loop(0, ntok, issue, 0, unroll=8)

    @pl.when(f == 0)
    def _repack():
        wait_rows(slot)
        for m in range(chunks):
            xe_ref[m * FFN_TM:(m + 1) * FFN_TM, :] = jnp.concatenate(
                [rows_ref[slot, pl.ds(m * FFN_TM * SUBLANES + c, FFN_TM, stride=SUBLANES), :]
                 for c in range(d // LANES)], axis=1).astype(BF16)
        y_ref[...] = jnp.zeros_like(y_ref)

    wg = wg_ref[...].astype(BF16)
    wu = wu_ref[...].astype(BF16)
    wd = wd_ref[...].astype(BF16)
    for m in range(chunks):
        xe = xe_ref[m * FFN_TM:(m + 1) * FFN_TM, :]
        a = jnp.dot(xe, wg, preferred_element_type=F32)
        u = jnp.dot(xe, wu, preferred_element_type=F32)
        hmid = (a * jax.nn.sigmoid(a) * u).astype(BF16)
        y_ref[m * FFN_TM:(m + 1) * FFN_TM, :] += jnp.dot(hmid, wd, preferred_element_type=F32)
        for i in range(per_chunk):
            row_copy(idx_next_ref, (f * chunks + m) * per_chunk + i, 1 - slot).start()

    @pl.when((e == ne - 1) & (f == nf - 1))
    def _drain():
        wait_rows(1 - slot)


def _experts(idx_e, h2_rows, w_gate, w_up, w_down):
    ne, d, ff = w_gate.shape
    ntok = idx_e.shape[-1]
    tf = FFN_TF
    nf = ff // tf
    assert ff % tf == 0 and ntok % FFN_TM == 0 and ntok % (nf * (ntok // FFN_TM)) == 0 and d // LANES == SUBLANES
    return pl.pallas_call(
        functools.partial(_experts_kernel, ne=ne, nf=nf),
        grid=(ne, nf),
        in_specs=[pl.BlockSpec((1, 1, ntok), lambda e, f: (e, 0, 0), memory_space=pltpu.SMEM),
                  pl.BlockSpec((1, 1, ntok), lambda e, f: (jnp.minimum(e + 1, ne - 1), 0, 0),
                               memory_space=pltpu.SMEM),
                  pl.BlockSpec(memory_space=pl.ANY),
                  pl.BlockSpec((None, d, tf), lambda e, f: (e, 0, f)),
                  pl.BlockSpec((None, d, tf), lambda e, f: (e, 0, f)),
                  pl.BlockSpec((None, tf, d), lambda e, f: (e, f, 0))],
        out_specs=pl.BlockSpec((None, ntok, d), lambda e, f: (e, 0, 0)),
        out_shape=jax.ShapeDtypeStruct((ne, ntok, d), F32),
        scratch_shapes=[pltpu.VMEM((2, ntok * SUBLANES, LANES), F32), pltpu.VMEM((ntok, d), BF16),
                        pltpu.SemaphoreType.DMA((2,))],
        compiler_params=_cparams(("arbitrary", "arbitrary"), VMEM_LIMIT),
        name="experts",
    )(idx_e, idx_e, h2_rows, w_gate, w_up, w_down)


def _combine_kernel(idx_ref, gate_ref, y_ref, x1_ref, g_ref, o_ref, acc_ref, ybuf_ref, *, batch):
    b, e = pl.program_id(0), pl.program_id(1)
    cap, d = y_ref.shape
    tm = x1_ref.shape[0]
    cur = b % 2

    @pl.when(b < batch)
    def _scatter():
        @pl.when(e == 0)
        def _init():
            acc_ref[cur] = jnp.zeros(acc_ref.shape[1:], F32)

        for c in range(d // LANES):
            ybuf_ref[pl.ds(c, cap, stride=SUBLANES), :] = y_ref[:, c * LANES:(c + 1) * LANES]

        def group(k, carry):
            rows, sums = [], []
            for u in range(RMW_UNROLL):
                i = k * RMW_UNROLL + u
                t = pl.ds(pl.multiple_of(idx_ref[0, 0, i], SUBLANES), SUBLANES)
                yi = ybuf_ref[pl.ds(pl.multiple_of(i * SUBLANES, SUBLANES), SUBLANES), :]
                rows.append(t)
                sums.append(acc_ref[cur, t, :] + gate_ref[0, 0, i] * yi)
            for t, s in zip(rows, sums):
                acc_ref[cur, t, :] = s
            return carry

        lax.fori_loop(0, cap // RMW_UNROLL, group, 0)

    @pl.when(b > 0)
    def _emit():
        base = pl.multiple_of(e * tm * SUBLANES, SUBLANES)
        moe = jnp.concatenate([acc_ref[1 - cur, pl.ds(base + c, tm, stride=SUBLANES), :]
                               for c in range(d // LANES)], axis=1)
        o_ref[...] = _rms(x1_ref[...] + moe, g_ref[...])


def _combine(idx_l, gates, y, x1, g_final, batch, seq):
    ne = y.shape[0]
    t, d = x1.shape
    cap = idx_l.shape[-1]
    tm = seq // ne
    assert cap % RMW_UNROLL == 0 and d // LANES == SUBLANES and tm * ne == seq and tm % SUBLANES == 0
    scattered = lambda b: jnp.minimum(b, batch - 1)
    rows = lambda b, e: (jnp.maximum(b - 1, 0) * ne + jnp.where(b > 0, e, 0), 0)
    smem = pl.BlockSpec((1, 1, cap), lambda b, e: (scattered(b) * ne + e, 0, 0), memory_space=pltpu.SMEM)
    return pl.pallas_call(
        functools.partial(_combine_kernel, batch=batch),
        grid=(batch + 1, ne),
        in_specs=[smem, smem, pl.BlockSpec((None, cap, d), lambda b, e: (e, scattered(b), 0)),
                  pl.BlockSpec((tm, d), rows), pl.BlockSpec((1, d), lambda b, e: (0, 0))],
        out_specs=pl.BlockSpec((tm, d), rows),
        out_shape=jax.ShapeDtypeStruct((t, d), F32),
        scratch_shapes=[pltpu.VMEM((2, seq * SUBLANES, LANES), F32), pltpu.VMEM((cap * SUBLANES, LANES), F32)],
        compiler_params=_cparams(("arbitrary", "arbitrary"), VMEM_LIMIT),
        name="combine",
    )(idx_l.reshape(batch * ne, 1, cap), gates.reshape(batch * ne, 1, cap), y, x1, g_final.reshape(1, d))


def _permute_a_heads(w_in, g_out_a, w_out):
    cols = jnp.asarray([h * HEAD_DIM + i for h in A_HEAD_ORDER for i in range(HEAD_DIM)], jnp.int32)
    w_in = jnp.concatenate([w_in[:, :A_Q][:, cols], w_in[:, A_Q:]], axis=1)
    w_out = jnp.concatenate([w_out[:A_Q][cols], w_out[A_Q:]], axis=0)
    return w_in, g_out_a[cols], w_out


def kernel(x, g_mix, w_in, a_sink, g_out_a, g_out_b, w_out, g_ffn, w_router, w_gate, w_up, w_down, g_final):
    batch, seq, d = x.shape
    depth = w_in.shape[0]
    ne = w_router.shape[-1]
    cap = CAPACITY_FACTOR * seq // ne
    assert depth == 1, "the last kernel fuses the MoE residual with the output norm of a single layer"
    l = 0
    x2 = x.reshape(batch * seq, d)
    w_in_p, g_out_a_p, w_out_p = _permute_a_heads(w_in[l], g_out_a[l], w_out[l])
    qa, ka, va, qb, kb, vb = _proj(x2, g_mix[l], w_in_p, seq)
    oa = _attn_a(qa, ka, va, a_sink[l], batch, seq)
    ob = _attn_b(qb, kb, vb, batch, seq)
    x1, h2_rows, aff_t = _mix(x2, oa, ob, g_out_a_p, g_out_b[l], w_out_p, g_ffn[l], w_router[l], batch, seq)
    idx_g, gates = _route(aff_t, cap)
    idx_e = idx_g.transpose(1, 0, 2).reshape(ne, 1, batch * cap)
    y = _experts(idx_e, h2_rows, w_gate[l], w_up[l], w_down[l])
    idx_l = idx_g - (jnp.arange(batch, dtype=jnp.int32) * (seq * SUBLANES))[:, None, None]
    out = _combine(idx_l, gates, y, x1, g_final, batch, seq)
    return out.reshape(batch, seq, d)
```

```python
import functools

import jax
import jax.numpy as jnp
from jax import lax
from jax.experimental import pallas as pl
from jax.experimental.pallas import tpu as pltpu

F32 = jnp.float32
BF16 = jnp.bfloat16

LANES = 128
SUBLANES = 8
BF16_ROWS = 16
VMEM_LIMIT = 56 * 1024 * 1024

HEAD_DIM = 64
A_HEADS = 8
A_KV_HEADS = 2
A_GROUP = A_HEADS // A_KV_HEADS
A_HALF_WINDOW = 128
B_HEADS = 8
DILATED_PATTERNS = ((128, 1), (512, 4), (2048, 16))
A_Q = A_HEADS * HEAD_DIM
A_KV = A_KV_HEADS * HEAD_DIM
B_W = B_HEADS * HEAD_DIM
ROPE_THETA = 500000.0
ROT_DIM = HEAD_DIM // 4
CAPACITY_FACTOR = 2
NORM_EPS = 1e-6
NEG_INF = -1e30
LOG2_E = 1.4426950408889634

PROJ_TM = 1024
ATT_QB = 128
ATT_A_SLOTS = 2
ATT_B_SLOTS = 4
MIX_TM = 1024
FFN_TF = 512
FFN_TM = 1024
RMW_UNROLL = 8


def _cparams(sem, vmem=None):
    return pltpu.CompilerParams(dimension_semantics=sem, vmem_limit_bytes=vmem)


def _rope_tables(seq_len):
    half = ROT_DIM // 2
    inv_freq = ROPE_THETA ** (-jnp.arange(0, ROT_DIM, 2, dtype=F32) / ROT_DIM)
    ang = jnp.arange(seq_len, dtype=F32)[:, None] * inv_freq[None, :]
    cos = jnp.tile(jnp.cos(ang), (1, LANES // half))
    sin = jnp.tile(jnp.sin(ang), (1, LANES // half))
    j = jnp.arange(LANES) % HEAD_DIM
    cos_t = jnp.where(j < ROT_DIM, cos, 1.0)
    sin_a = jnp.where(j < half, -sin, 0.0)
    sin_b = jnp.where((j >= half) & (j < ROT_DIM), sin, 0.0)
    return cos_t, sin_a, sin_b


def _rms(x, g):
    return x * lax.rsqrt(jnp.mean(x * x, axis=-1, keepdims=True) + NORM_EPS) * g


def _proj_kernel(x_ref, g_ref, w_ref, cos_ref, sa_ref, sb_ref,
                 qa_ref, ka_ref, va_ref, qb_ref, kb_ref, vb_ref, wbf_ref):
    @pl.when(pl.program_id(0) == 0)
    def _cast_weights():
        wbf_ref[...] = w_ref[...].astype(BF16)

    h = _rms(x_ref[...], g_ref[...]).astype(BF16)
    cos, sa, sb = cos_ref[...], sa_ref[...], sb_ref[...]
    half = ROT_DIM // 2

    def rope(t):
        return t * cos + pltpu.roll(t, LANES - half, 1) * sa + pltpu.roll(t, half, 1) * sb

    def section(col0, width):
        return jnp.dot(h, wbf_ref[:, col0:col0 + width], preferred_element_type=F32)

    scale = HEAD_DIM ** -0.5 * LOG2_E
    qa = section(0, A_Q)
    for c in range(A_Q // LANES):
        qa_ref[c] = (rope(qa[:, c * LANES:(c + 1) * LANES]) * scale).astype(qa_ref.dtype)
    kv = section(A_Q, 2 * A_KV)
    first = lax.broadcasted_iota(jnp.int32, (kv.shape[0], LANES), 1) < HEAD_DIM
    for ref, t in ((ka_ref, rope(kv[:, :A_KV])), (va_ref, kv[:, A_KV:])):
        swapped = pltpu.roll(t, HEAD_DIM, 1)
        ref[0] = jnp.where(first, t, swapped).astype(ref.dtype)
        ref[1] = jnp.where(first, swapped, t).astype(ref.dtype)
    qb = section(A_Q + 2 * A_KV, B_W)
    for c in range(B_W // LANES):
        qb_ref[c] = rope(qb[:, c * LANES:(c + 1) * LANES]) * scale
    kb = section(A_Q + 2 * A_KV + B_W, B_W)
    for c in range(B_W // LANES):
        kb_ref[c] = rope(kb[:, c * LANES:(c + 1) * LANES])
    vb = section(A_Q + 2 * A_KV + 2 * B_W, B_W)
    for c in range(B_W // LANES):
        vb_ref[c] = vb[:, c * LANES:(c + 1) * LANES]


def _proj(x2, g_mix, w_in, seq):
    t, d = x2.shape
    tm = PROJ_TM
    assert t % tm == 0 and seq % tm == 0 and A_KV == LANES and A_KV_HEADS == 2
    nseq = seq // tm
    tables = _rope_tables(seq)
    slab = lambda i: (0, i, 0)
    const = lambda i: (0, 0)
    tab = lambda i: (i % nseq, 0)
    widths = (A_Q, A_KV_HEADS * LANES, A_KV_HEADS * LANES, B_W, B_W, B_W)
    return pl.pallas_call(
        _proj_kernel,
        grid=(t // tm,),
        in_specs=[pl.BlockSpec((tm, d), lambda i: (i, 0)), pl.BlockSpec((1, d), const),
                  pl.BlockSpec(w_in.shape, const)] + [pl.BlockSpec((tm, LANES), tab)] * 3,
        out_specs=[pl.BlockSpec((w // LANES, tm, LANES), slab) for w in widths],
        out_shape=[jax.ShapeDtypeStruct((w // LANES, t, LANES), dt)
                   for w, dt in zip(widths, (BF16, BF16, BF16, F32, F32, F32))],
        scratch_shapes=[pltpu.VMEM(w_in.shape, BF16)],
        compiler_params=_cparams(("arbitrary",), VMEM_LIMIT),
        name="proj",
    )(x2, g_mix.reshape(1, d), w_in, *tables)


def _band_bias(bias_ref, qb, kw, hw):
    kk = lax.broadcasted_iota(jnp.int32, (kw, qb), 0)
    qq = lax.broadcasted_iota(jnp.int32, (kw, qb), 1)
    for t, delta in enumerate((0, -hw, -2 * hw)):
        bias_ref[t] = jnp.where(jnp.abs(kk + delta - qq) <= hw, 0.0, NEG_INF)


def _band_scratch(qb, kw, slots):
    n = 2 * slots
    return ([pltpu.VMEM((kw, 2 * qb), F32)] * n + [pltpu.VMEM((kw, 2 * qb), BF16)] * n
            + [pltpu.VMEM((SUBLANES, 2 * qb), F32)] * n + [pltpu.VMEM((LANES, kw), BF16)] * n
            + [pltpu.VMEM((SUBLANES, 2 * qb), F32)] * n)


def _band_pipeline(q_ref, k_ref, v_ref, bias_ref, o_ref, lse_ref, bufs, *, nblocks, geometry, stage_vt,
                   sink_row=None):
    qb = ATT_QB
    n = len(bufs) // 5
    slots = n // 2
    s_bufs, p_bufs, ml_bufs, vt_bufs, mx_bufs = (bufs[i * n:(i + 1) * n] for i in range(5))
    kw = s_bufs[0].shape[0]
    for s_buf, p_buf, ml_buf, vt_buf, mx_buf in zip(s_bufs, p_bufs, ml_bufs, vt_bufs, mx_bufs):
        mx_buf[...] = jnp.zeros_like(mx_buf)
        s_buf[...] = jnp.zeros_like(s_buf)
        p_buf[...] = jnp.zeros_like(p_buf)
        ml_buf[...] = jnp.ones_like(ml_buf)
        vt_buf[...] = jnp.zeros_like(vt_buf)

    def rows(block):
        q0, k0, kind = geometry(jnp.clip(block, 0, nblocks - 1))
        return pl.ds(pl.multiple_of(q0, BF16_ROWS), qb), pl.ds(pl.multiple_of(k0, BF16_ROWS), kw), kind

    def scores(block, s_buf, mx_buf):
        q_rows, k_rows, kind = rows(block)
        q = q_ref[q_rows, :]
        lo = lax.broadcasted_iota(jnp.int32, q.shape, 1) < HEAD_DIM
        q2 = jnp.concatenate([jnp.where(lo, q, 0.0), jnp.where(lo, 0.0, q)], axis=0).astype(BF16)
        s = lax.dot_general(k_ref[k_rows, :].astype(BF16), q2, (((1,), (1,)), ((), ())),
                            preferred_element_type=F32)
        bias = bias_ref[kind]
        for hh in range(2):
            s_h = s[:, hh * qb:(hh + 1) * qb] + bias
            s_buf[:, hh * qb:(hh + 1) * qb] = s_h
            mx_buf[0:1, hh * qb:(hh + 1) * qb] = jnp.max(s_h, axis=0, keepdims=True)

    def softmax(block, s_buf, mx_buf, p_buf, ml_buf, vt_buf):
        if stage_vt:
            _, k_rows, _ = rows(block)
            vt_buf[...] = v_ref[k_rows, :].astype(F32).T.astype(BF16)
        m = mx_buf[0:1, :]
        if sink_row is not None:
            m = jnp.maximum(m, sink_row)
        p = jnp.exp2(s_buf[...] - m)
        l = jnp.sum(p, axis=0, keepdims=True)
        if sink_row is not None:
            l = l + jnp.exp2(sink_row - m)
        p_buf[...] = p.astype(BF16)
        ml_buf[0:1, :] = m
        ml_buf[1:2, :] = l

    def output(block, p_buf, ml_buf, vt_buf):
        q_rows, k_rows, _ = rows(block)
        if stage_vt:
            o_t = jnp.dot(vt_buf[...], p_buf[...], preferred_element_type=F32)
        else:
            o_t = lax.dot_general(v_ref[k_rows, :].astype(BF16), p_buf[...], (((0,), (0,)), ((), ())),
                                  preferred_element_type=F32)
        o_tc = jnp.concatenate([o_t[:HEAD_DIM, :qb], o_t[HEAD_DIM:, qb:]], axis=0)
        top = lax.broadcasted_iota(jnp.int32, (LANES, qb), 0) < HEAD_DIM
        m, l = ml_buf[0:1, :], ml_buf[1:2, :]
        inv = 1.0 / l
        o_ref[q_rows, :] = (o_tc * jnp.where(top, inv[:, :qb], inv[:, qb:])).T
        if lse_ref is not None:
            lse = m + jnp.log2(l)
            lse_ref[q_rows, :] = jnp.where(top, lse[:, :qb], lse[:, qb:]).T

    def trip_pair(jj, carry):
        for parity in range(2):
            for slot in range(slots):
                t = (2 * jj + parity) * slots + slot
                rd, wr = (1 - parity) * slots + slot, parity * slots + slot
                output(t - 2 * slots, p_bufs[rd], ml_bufs[rd], vt_bufs[rd])
                softmax(t - slots, s_bufs[rd], mx_bufs[rd], p_bufs[wr], ml_bufs[wr], vt_bufs[wr])
                scores(t, s_bufs[wr], mx_bufs[wr])
        return carry

    trips = nblocks // slots + 2
    assert nblocks % slots == 0 and trips % 2 == 0
    lax.fori_loop(0, trips // 2, trip_pair, 0)


def _segment_geometry(block, n, hw, base=0):
    qb = ATT_QB
    nblk = n // qb
    seg, jb = block // nblk, block % nblk
    qs = jb * qb
    ks = jnp.clip(qs - hw, 0, n - (qb + 2 * hw))
    kind = jnp.where(jb == 0, 0, jnp.where(jb == nblk - 1, 2, 1))
    return base + seg * n + qs, base + seg * n + ks, kind


def _attn_a_kernel(sink_ref, q_ref, k_ref, v_ref, o_ref, bias_ref, *bufs, seq):
    j = pl.program_id(1)
    qb, hw = ATT_QB, A_HALF_WINDOW
    _band_bias(bias_ref, qb, qb + 2 * hw, hw)
    col = lax.broadcasted_iota(jnp.int32, (1, 2 * qb), 1)
    sink_row = jnp.where(col < qb, sink_ref[2 * j], sink_ref[2 * j + 1]) * LOG2_E
    _band_pipeline(q_ref, k_ref, v_ref, bias_ref, o_ref, None, bufs, nblocks=seq // qb,
                   geometry=functools.partial(_segment_geometry, n=seq, hw=hw), stage_vt=False,
                   sink_row=sink_row)


def _attn_a(qa, ka, va, sink, batch, seq):
    qb, hw = ATT_QB, A_HALF_WINDOW
    assert seq % qb == 0 and seq >= qb + 2 * hw and A_KV == LANES
    qspec = pl.BlockSpec((None, seq, LANES), lambda b, j: (j, b, 0))
    per_kv = A_GROUP * HEAD_DIM // LANES
    kvspec = pl.BlockSpec((None, seq, LANES), lambda b, j: (j // per_kv, b, 0))
    return pl.pallas_call(
        functools.partial(_attn_a_kernel, seq=seq),
        grid=(batch, A_Q // LANES),
        in_specs=[pl.BlockSpec(memory_space=pltpu.SMEM), qspec, kvspec, kvspec],
        out_specs=qspec,
        out_shape=jax.ShapeDtypeStruct(qa.shape, F32),
        scratch_shapes=[pltpu.VMEM((3, qb + 2 * hw, qb), F32)] + _band_scratch(qb, qb + 2 * hw, ATT_A_SLOTS),
        compiler_params=_cparams(("parallel", "parallel"), VMEM_LIMIT),
        name="attn_a",
    )(sink, qa, ka, va)


def _attn_b_kernel(q_ref, k_ref, v_ref, out_ref, bias_ref, tmp_ref, qs_ref, ks_ref, vs_ref, os_ref, ls_ref,
                   *bufs, seq):
    (w1, d1), (w4, d4), (w16, d16) = DILATED_PATTERNS
    hw = w1 // (2 * d1)
    qb = ATT_QB
    _band_bias(bias_ref, qb, qb + 2 * hw, hw)
    n4, n16 = seq // d4, seq // d16
    step = d16 // d4
    chunk = 512
    for src, dst in ((q_ref, qs_ref), (k_ref, ks_ref), (v_ref, vs_ref)):
        for c in range(seq // chunk):
            dst[c * chunk:(c + 1) * chunk, :] = src[c * chunk:(c + 1) * chunk, :].astype(dst.dtype)
        for r in range(d4):
            tmp_ref[r * n4:(r + 1) * n4, :] = src[pl.ds(r, n4, stride=d4), :]
        for c in range(seq // chunk):
            dst[seq + c * chunk:seq + (c + 1) * chunk, :] = tmp_ref[c * chunk:(c + 1) * chunk, :].astype(dst.dtype)
        for r in range(d16):
            dst[2 * seq + r * n16:2 * seq + (r + 1) * n16, :] = tmp_ref[
                pl.ds((r % d4) * n4 + r // d4, n16, stride=step), :].astype(dst.dtype)

    bps = seq // qb
    lg = [(seq // d // qb).bit_length() - 1 for (_, d) in DILATED_PATTERNS]

    def geometry(block):
        pat, within = block // bps, block % bps
        sh = jnp.where(pat == 0, lg[0], jnp.where(pat == 1, lg[1], lg[2]))
        seg, jb = within >> sh, within & ((1 << sh) - 1)
        n = qb << sh
        qs = jb * qb
        ks = jnp.clip(qs - hw, 0, n - (qb + 2 * hw))
        kind = jnp.where(jb == 0, 0, jnp.where(jb == (1 << sh) - 1, 2, 1))
        base = pat * seq + seg * n
        return base + qs, base + ks, kind

    _band_pipeline(qs_ref, ks_ref, vs_ref, bias_ref, os_ref, ls_ref, bufs,
                   nblocks=len(DILATED_PATTERNS) * bps, geometry=geometry, stage_vt=True)

    def merge(r, carry):
        rows16 = pl.ds(pl.multiple_of(2 * seq + r * n16, SUBLANES), n16)
        rows4 = pl.ds(seq + (r % d4) * n4 + r // d4, n16, stride=step)
        rows1 = pl.ds(r, n16, stride=d16)
        la, lb, lc = ls_ref[rows1, :], ls_ref[rows4, :], ls_ref[rows16, :]
        top = jnp.maximum(jnp.maximum(la, lb), lc)
        wa, wb, wc = jnp.exp2(la - top), jnp.exp2(lb - top), jnp.exp2(lc - top)
        num = wa * os_ref[rows1, :] + wb * os_ref[rows4, :] + wc * os_ref[rows16, :]
        out_ref[rows1, :] = num * (1.0 / (wa + wb + wc))
        return carry

    lax.fori_loop(0, d16, merge, 0)


def _attn_b(qb_, kb, vb, batch, seq):
    (w1, d1), (w4, d4), (w16, d16) = DILATED_PATTERNS
    hw = w1 // (2 * d1)
    qb = ATT_QB
    assert d1 == 1 and d16 % d4 == 0 and w4 // (2 * d4) == hw and w16 // (2 * d16) == hw
    assert seq % 512 == 0 and seq // d16 >= qb + 2 * hw
    for (_, d) in DILATED_PATTERNS:
        nblk = seq // d // qb
        assert nblk * d * qb == seq and nblk & (nblk - 1) == 0
    spec = pl.BlockSpec((None, seq, LANES), lambda b, p: (p, b, 0))
    npat = len(DILATED_PATTERNS)
    return pl.pallas_call(
        functools.partial(_attn_b_kernel, seq=seq),
        grid=(batch, B_W // LANES),
        in_specs=[spec] * 3,
        out_specs=spec,
        out_shape=jax.ShapeDtypeStruct(qb_.shape, F32),
        scratch_shapes=[pltpu.VMEM((3, qb + 2 * hw, qb), F32), pltpu.VMEM((seq, LANES), F32),
                        pltpu.VMEM((npat * seq, LANES), F32), pltpu.VMEM((npat * seq, LANES), BF16),
                        pltpu.VMEM((npat * seq, LANES), BF16), pltpu.VMEM((npat * seq, LANES), F32),
                        pltpu.VMEM((npat * seq, LANES), F32)] + _band_scratch(qb, qb + 2 * hw, ATT_B_SLOTS),
        compiler_params=_cparams(("parallel", "parallel"), VMEM_LIMIT),
        name="attn_b",
    )(qb_, kb, vb)


def _mix_kernel(x_ref, oa_ref, ob_ref, ga_ref, gb_ref, wo_f32_ref, gf_ref, wr_ref,
                x1_ref, h2_ref, aff_ref, wo_ref):
    @pl.when(pl.program_id(0) == 0)
    def _cast_weights():
        wo_ref[...] = wo_f32_ref[...].astype(BF16)

    oa = jnp.concatenate([oa_ref[c] for c in range(oa_ref.shape[0])], axis=1)
    ob = jnp.concatenate([ob_ref[c] for c in range(ob_ref.shape[0])], axis=1)
    na = _rms(oa, ga_ref[...]).astype(BF16)
    nb = _rms(ob, gb_ref[...]).astype(BF16)
    x1 = (x_ref[...] + jnp.dot(na, wo_ref[:A_Q, :], preferred_element_type=F32)
          + jnp.dot(nb, wo_ref[A_Q:, :], preferred_element_type=F32))
    x1_ref[...] = x1
    h2 = _rms(x1, gf_ref[...])
    tm = h2.shape[0]
    for c in range(h2.shape[1] // LANES):
        h2_ref[pl.ds(c, tm, stride=SUBLANES), :] = h2[:, c * LANES:(c + 1) * LANES]
    ne = aff_ref.shape[0]
    dn = (((1,), (1,)), ((), ()))
    h_hi = h2.astype(BF16)
    h_mid = (h2 - h_hi.astype(F32)).astype(BF16)
    r = lax.dot_general(wr_ref[...], h_hi, dn, preferred_element_type=F32)
    logits = r[:ne] + r[ne:] + lax.dot_general(wr_ref[:ne, :], h_mid, dn, preferred_element_type=F32)
    e = jnp.exp(logits - jnp.max(logits, axis=0, keepdims=True))
    aff_ref[...] = e / jnp.sum(e, axis=0, keepdims=True)


def _mix(x2, oa, ob, g_out_a, g_out_b, w_out, g_ffn, w_router, batch, seq):
    t, d = x2.shape
    tm = MIX_TM
    assert seq % tm == 0 and d % LANES == 0 and d // LANES == SUBLANES
    ns = seq // tm
    row = lambda i: (i, 0)
    const = lambda i: (0, 0)
    ne = w_router.shape[1]
    w_hi = w_router.T.astype(BF16)
    w_mid = (w_router.T - w_hi.astype(F32)).astype(BF16)
    wr = jnp.concatenate([w_hi, w_mid], axis=0)
    return pl.pallas_call(
        _mix_kernel,
        grid=(t // tm,),
        in_specs=[pl.BlockSpec((tm, d), row), pl.BlockSpec((A_Q // LANES, tm, LANES), lambda i: (0, i, 0)),
                  pl.BlockSpec((B_W // LANES, tm, LANES), lambda i: (0, i, 0)),
                  pl.BlockSpec((1, A_Q), const), pl.BlockSpec((1, B_W), const),
                  pl.BlockSpec(w_out.shape, const), pl.BlockSpec((1, d), const),
                  pl.BlockSpec((2 * ne, d), const)],
        out_specs=[pl.BlockSpec((tm, d), row), pl.BlockSpec((tm * SUBLANES, LANES), row),
                   pl.BlockSpec((None, ne, tm), lambda i: (i // ns, 0, i % ns))],
        out_shape=[jax.ShapeDtypeStruct((t, d), F32),
                   jax.ShapeDtypeStruct((t * SUBLANES, LANES), F32),
                   jax.ShapeDtypeStruct((batch, ne, seq), F32)],
        scratch_shapes=[pltpu.VMEM(w_out.shape, BF16)],
        compiler_params=_cparams(("arbitrary",), VMEM_LIMIT),
        name="mix",
    )(x2, oa, ob, g_out_a.reshape(1, -1), g_out_b.reshape(1, -1), w_out, g_ffn.reshape(1, d), wr)


def _lane_cumsum(mask_f, tri):
    rows, s = mask_f.shape
    carry = jnp.zeros((rows, 1), F32)
    out = []
    for c in range(s // LANES):
        inc = jnp.dot(mask_f[:, c * LANES:(c + 1) * LANES].astype(BF16), tri,
                      preferred_element_type=F32) + carry
        out.append(inc)
        carry = inc[:, LANES - 1:LANES]
    return jnp.concatenate(out, axis=1)


def _route_kernel(aff_ref, idx_e_ref, idx_l_ref, gate_ref, *, cap, seq):
    b = pl.program_id(0)
    aff = aff_ref[...]
    ne = aff.shape[0]

    def search(i, thr):
        cand = thr | (jnp.int32(1) << (30 - i))
        cnt = jnp.sum((aff >= lax.bitcast_convert_type(cand, F32)).astype(F32), axis=1, keepdims=True)
        return jnp.where(cnt >= cap, cand, thr)

    thr = lax.bitcast_convert_type(lax.fori_loop(0, 31, search, jnp.zeros((ne, 1), jnp.int32)), F32)
    gt = aff > thr
    eq = aff == thr
    n_gt = jnp.sum(gt.astype(F32), axis=1, keepdims=True)
    ri = lax.broadcasted_iota(jnp.int32, (LANES, LANES), 0)
    ci = lax.broadcasted_iota(jnp.int32, (LANES, LANES), 1)
    tri = (ri <= ci).astype(BF16)
    eq_f = eq.astype(F32)
    eq_rank = _lane_cumsum(eq_f, tri) - eq_f
    sel = gt | (eq & (eq_rank < cap - n_gt))
    sel_f = sel.astype(F32)
    slot = _lane_cumsum(sel_f, tri) - sel_f
    key_t = jnp.where(sel, slot, -1.0).T
    tok = lax.broadcasted_iota(jnp.int32, (SUBLANES, seq), 1)
    piece = lax.broadcasted_iota(jnp.int32, (SUBLANES, seq), 0)
    tok_rows = jnp.where(piece == 0, tok // 64, jnp.where(piece == 1, tok % 64, 0)).astype(F32)
    p_iota = lax.broadcasted_iota(jnp.int32, (1, cap), 1).astype(F32)
    for e in range(ne):
        onehot = (key_t[:, e:e + 1] == p_iota).astype(BF16)
        g = aff[e:e + 1, :]
        g_hi = g.astype(BF16).astype(F32)
        g_mid = (g - g_hi).astype(BF16).astype(F32)
        g_lo = g - g_hi - g_mid
        lhs = jnp.where(piece == 2, g_hi, jnp.where(piece == 3, g_mid, jnp.where(piece == 4, g_lo, tok_rows)))
        res = jnp.dot(lhs.astype(BF16), onehot, preferred_element_type=F32)
        tile_row = (res[0:1] * 64.0 + res[1:2]).astype(jnp.int32) * SUBLANES
        idx_l_ref[e] = tile_row
        idx_e_ref[e] = tile_row + b * (seq * SUBLANES)
        gate_ref[e] = res[2:3] + res[3:4] + res[4:5]


def _route(aff_t, cap):
    batch, ne, seq = aff_t.shape
    assert seq % LANES == 0 and seq <= 64 * 256
    return pl.pallas_call(
        functools.partial(_route_kernel, cap=cap, seq=seq),
        grid=(batch,),
        in_specs=[pl.BlockSpec((None, ne, seq), lambda b: (b, 0, 0))],
        out_specs=[pl.BlockSpec((ne, 1, cap), lambda b: (0, 0, b)), pl.BlockSpec((ne, 1, cap), lambda b: (b, 0, 0)),
                   pl.BlockSpec((ne, 1, cap), lambda b: (b, 0, 0))],
        out_shape=[jax.ShapeDtypeStruct((ne, 1, batch * cap), jnp.int32),
                   jax.ShapeDtypeStruct((batch * ne, 1, cap), jnp.int32),
                   jax.ShapeDtypeStruct((batch * ne, 1, cap), F32)],
        compiler_params=_cparams(("parallel",), VMEM_LIMIT),
        name="route",
    )(aff_t)


def _experts_kernel(idx_ref, idx_next_ref, h_hbm, wg_ref, wu_ref, wd_ref, y_ref, rows_ref, xe_ref, sems,
                    *, ne, nf):
    e, f = pl.program_id(0), pl.program_id(1)
    ntok, d = xe_ref.shape
    slot = e % 2
    chunks = ntok // FFN_TM
    per_chunk = ntok // (nf * chunks)

    def row_copy(idx_smem, i, buf):
        src = pl.multiple_of(idx_smem[0, 0, i], SUBLANES)
        dst = pl.multiple_of(i * SUBLANES, SUBLANES)
        return pltpu.make_async_copy(h_hbm.at[pl.ds(src, SUBLANES), :],
                                     rows_ref.at[buf, pl.ds(dst, SUBLANES), :], sems.at[buf])

    def wait_rows(buf):
        pltpu.make_async_copy(h_hbm.at[pl.ds(0, ntok * SUBLANES), :], rows_ref.at[buf], sems.at[buf]).wait()

    @pl.when((e == 0) & (f == 0))
    def _first_gather():
        def issue(i, carry):
            row_copy(idx_ref, i, 0).start()
            return carry

        lax.fori_loop(0, ntok, issue, 0, unroll=8)

    @pl.when(f == 0)
    def _repack():
        wait_rows(slot)
        for m in range(chunks):
            xe_ref[m * FFN_TM:(m + 1) * FFN_TM, :] = jnp.concatenate(
                [rows_ref[slot, pl.ds(m * FFN_TM * SUBLANES + c, FFN_TM, stride=SUBLANES), :]
                 for c in range(d // LANES)], axis=1).astype(BF16)
        y_ref[...] = jnp.zeros_like(y_ref)

    wg = wg_ref[...].astype(BF16)
    wu = wu_ref[...].astype(BF16)
    wd = wd_ref[...].astype(BF16)
    for m in range(chunks):
        xe = xe_ref[m * FFN_TM:(m + 1) * FFN_TM, :]
        a = jnp.dot(xe, wg, preferred_element_type=F32)
        u = jnp.dot(xe, wu, preferred_element_type=F32)
        hmid = (a * jax.nn.sigmoid(a) * u).astype(BF16)
        y_ref[m * FFN_TM:(m + 1) * FFN_TM, :] += jnp.dot(hmid, wd, preferred_element_type=F32)
        for i in range(per_chunk):
            row_copy(idx_next_ref, (f * chunks + m) * per_chunk + i, 1 - slot).start()

    @pl.when((e == ne - 1) & (f == nf - 1))
    def _drain():
        wait_rows(1 - slot)


def _experts(idx_e, h2_rows, w_gate, w_up, w_down):
    ne, d, ff = w_gate.shape
    ntok = idx_e.shape[-1]
    tf = FFN_TF
    nf = ff // tf
    assert ff % tf == 0 and ntok % FFN_TM == 0 and ntok % (nf * (ntok // FFN_TM)) == 0 and d // LANES == SUBLANES
    return pl.pallas_call(
        functools.partial(_experts_kernel, ne=ne, nf=nf),
        grid=(ne, nf),
        in_specs=[pl.BlockSpec((1, 1, ntok), lambda e, f: (e, 0, 0), memory_space=pltpu.SMEM),
                  pl.BlockSpec((1, 1, ntok), lambda e, f: (jnp.minimum(e + 1, ne - 1), 0, 0),
                               memory_space=pltpu.SMEM),
                  pl.BlockSpec(memory_space=pl.ANY),
                  pl.BlockSpec((None, d, tf), lambda e, f: (e, 0, f)),
                  pl.BlockSpec((None, d, tf), lambda e, f: (e, 0, f)),
                  pl.BlockSpec((None, tf, d), lambda e, f: (e, f, 0))],
        out_specs=pl.BlockSpec((None, ntok, d), lambda e, f: (e, 0, 0)),
        out_shape=jax.ShapeDtypeStruct((ne, ntok, d), F32),
        scratch_shapes=[pltpu.VMEM((2, ntok * SUBLANES, LANES), F32), pltpu.VMEM((ntok, d), BF16),
                        pltpu.SemaphoreType.DMA((2,))],
        compiler_params=_cparams(("arbitrary", "arbitrary"), VMEM_LIMIT),
        name="experts",
    )(idx_e, idx_e, h2_rows, w_gate, w_up, w_down)


def _combine_kernel(idx_ref, gate_ref, y_ref, x1_ref, g_ref, o_ref, acc_ref, ybuf_ref, *, batch):
    b, e = pl.program_id(0), pl.program_id(1)
    cap, d = y_ref.shape
    tm = x1_ref.shape[0]
    cur = b % 2

    @pl.when(b < batch)
    def _scatter():
        @pl.when(e == 0)
        def _init():
            acc_ref[cur] = jnp.zeros(acc_ref.shape[1:], F32)

        for c in range(d // LANES):
            ybuf_ref[pl.ds(c, cap, stride=SUBLANES), :] = y_ref[:, c * LANES:(c + 1) * LANES]

        def group(k, carry):
            rows, sums = [], []
            for u in range(RMW_UNROLL):
                i = k * RMW_UNROLL + u
                t = pl.ds(pl.multiple_of(idx_ref[0, 0, i], SUBLANES), SUBLANES)
                yi = ybuf_ref[pl.ds(pl.multiple_of(i * SUBLANES, SUBLANES), SUBLANES), :]
                rows.append(t)
                sums.append(acc_ref[cur, t, :] + gate_ref[0, 0, i] * yi)
            for t, s in zip(rows, sums):
                acc_ref[cur, t, :] = s
            return carry

        lax.fori_loop(0, cap // RMW_UNROLL, group, 0)

    @pl.when(b > 0)
    def _emit():
        base = pl.multiple_of(e * tm * SUBLANES, SUBLANES)
        moe = jnp.concatenate([acc_ref[1 - cur, pl.ds(base + c, tm, stride=SUBLANES), :]
                               for c in range(d // LANES)], axis=1)
        o_ref[...] = _rms(x1_ref[...] + moe, g_ref[...])


def _combine(idx_l, gates, y, x1, g_final, batch, seq):
    ne = y.shape[0]
    t, d = x1.shape
    cap = idx_l.shape[-1]
    tm = seq // ne
    assert cap % RMW_UNROLL == 0 and d // LANES == SUBLANES and tm * ne == seq and tm % SUBLANES == 0
    scattered = lambda b: jnp.minimum(b, batch - 1)
    rows = lambda b, e: (jnp.maximum(b - 1, 0) * ne + jnp.where(b > 0, e, 0), 0)
    smem = pl.BlockSpec((1, 1, cap), lambda b, e: (scattered(b) * ne + e, 0, 0), memory_space=pltpu.SMEM)
    return pl.pallas_call(
        functools.partial(_combine_kernel, batch=batch),
        grid=(batch + 1, ne),
        in_specs=[smem, smem, pl.BlockSpec((None, cap, d), lambda b, e: (e, scattered(b), 0)),
                  pl.BlockSpec((tm, d), rows), pl.BlockSpec((1, d), lambda b, e: (0, 0))],
        out_specs=pl.BlockSpec((tm, d), rows),
        out_shape=jax.ShapeDtypeStruct((t, d), F32),
        scratch_shapes=[pltpu.VMEM((2, seq * SUBLANES, LANES), F32), pltpu.VMEM((cap * SUBLANES, LANES), F32)],
        compiler_params=_cparams(("arbitrary", "arbitrary"), VMEM_LIMIT),
        name="combine",
    )(idx_l, gates, y, x1, g_final.reshape(1, d))


def kernel(x, g_mix, w_in, a_sink, g_out_a, g_out_b, w_out, g_ffn, w_router, w_gate, w_up, w_down, g_final):
    batch, seq, d = x.shape
    depth = w_in.shape[0]
    ne = w_router.shape[-1]
    cap = CAPACITY_FACTOR * seq // ne
    assert depth == 1, "the last kernel fuses the MoE residual with the output norm of a single layer"
    l = 0
    x2 = x.reshape(batch * seq, d)
    qa, ka, va, qb, kb, vb = _proj(x2, g_mix[l], w_in[l], seq)
    oa = _attn_a(qa, ka, va, a_sink[l], batch, seq)
    ob = _attn_b(qb, kb, vb, batch, seq)
    x1, h2_rows, aff_t = _mix(x2, oa, ob, g_out_a[l], g_out_b[l], w_out[l], g_ffn[l], w_router[l], batch, seq)
    idx_e, idx_l, gates = _route(aff_t, cap)
    y = _experts(idx_e, h2_rows, w_gate[l], w_up[l], w_down[l])
    out = _combine(idx_l, gates, y, x1, g_final, batch, seq)
    return out.reshape(batch, seq, d)
```

```python
import functools

import jax
import jax.numpy as jnp
from jax import lax
from jax.experimental import pallas as pl
from jax.experimental.pallas import tpu as pltpu

F32 = jnp.float32
BF16 = jnp.bfloat16

LANES = 128
SUBLANES = 8
BF16_ROWS = 16
VMEM_LIMIT = 56 * 1024 * 1024

HEAD_DIM = 64
A_HEADS = 8
A_KV_HEADS = 2
A_GROUP = A_HEADS // A_KV_HEADS
A_HALF_WINDOW = 128
B_HEADS = 8
DILATED_PATTERNS = ((128, 1), (512, 4), (2048, 16))
A_Q = A_HEADS * HEAD_DIM
A_KV = A_KV_HEADS * HEAD_DIM
B_W = B_HEADS * HEAD_DIM
ROPE_THETA = 500000.0
ROT_DIM = HEAD_DIM // 4
CAPACITY_FACTOR = 2
NORM_EPS = 1e-6
NEG_INF = -1e30
LOG2_E = 1.4426950408889634

PROJ_TM = 1024
ATT_QB = 128
ATT_A_SLOTS = 2
ATT_B_SLOTS = 4
MIX_TM = 1024
FFN_TF = 512
FFN_TM = 1024
RMW_UNROLL = 8


def _cparams(sem, vmem=None):
    return pltpu.CompilerParams(dimension_semantics=sem, vmem_limit_bytes=vmem)


def _rope_tables(seq_len):
    half = ROT_DIM // 2
    inv_freq = ROPE_THETA ** (-jnp.arange(0, ROT_DIM, 2, dtype=F32) / ROT_DIM)
    ang = jnp.arange(seq_len, dtype=F32)[:, None] * inv_freq[None, :]
    cos = jnp.tile(jnp.cos(ang), (1, LANES // half))
    sin = jnp.tile(jnp.sin(ang), (1, LANES // half))
    j = jnp.arange(LANES) % HEAD_DIM
    cos_t = jnp.where(j < ROT_DIM, cos, 1.0)
    sin_a = jnp.where(j < half, -sin, 0.0)
    sin_b = jnp.where((j >= half) & (j < ROT_DIM), sin, 0.0)
    return cos_t, sin_a, sin_b


def _rms(x, g):
    return x * lax.rsqrt(jnp.mean(x * x, axis=-1, keepdims=True) + NORM_EPS) * g


def _proj_kernel(x_ref, g_ref, w_ref, cos_ref, sa_ref, sb_ref,
                 qa_ref, ka_ref, va_ref, qb_ref, kb_ref, vb_ref, wbf_ref):
    @pl.when(pl.program_id(0) == 0)
    def _cast_weights():
        wbf_ref[...] = w_ref[...].astype(BF16)

    h = _rms(x_ref[...], g_ref[...]).astype(BF16)
    cos, sa, sb = cos_ref[...], sa_ref[...], sb_ref[...]
    half = ROT_DIM // 2

    def rope(t):
        return t * cos + pltpu.roll(t, LANES - half, 1) * sa + pltpu.roll(t, half, 1) * sb

    def section(col0, width):
        return jnp.dot(h, wbf_ref[:, col0:col0 + width], preferred_element_type=F32)

    scale = HEAD_DIM ** -0.5 * LOG2_E
    qa = section(0, A_Q)
    for c in range(A_Q // LANES):
        qa_ref[c] = (rope(qa[:, c * LANES:(c + 1) * LANES]) * scale).astype(qa_ref.dtype)
    kv = section(A_Q, 2 * A_KV)
    first = lax.broadcasted_iota(jnp.int32, (kv.shape[0], LANES), 1) < HEAD_DIM
    for ref, t in ((ka_ref, rope(kv[:, :A_KV])), (va_ref, kv[:, A_KV:])):
        swapped = pltpu.roll(t, HEAD_DIM, 1)
        ref[0] = jnp.where(first, t, swapped).astype(ref.dtype)
        ref[1] = jnp.where(first, swapped, t).astype(ref.dtype)
    qb = section(A_Q + 2 * A_KV, B_W)
    for c in range(B_W // LANES):
        qb_ref[c] = rope(qb[:, c * LANES:(c + 1) * LANES]) * scale
    kb = section(A_Q + 2 * A_KV + B_W, B_W)
    for c in range(B_W // LANES):
        kb_ref[c] = rope(kb[:, c * LANES:(c + 1) * LANES])
    vb = section(A_Q + 2 * A_KV + 2 * B_W, B_W)
    for c in range(B_W // LANES):
        vb_ref[c] = vb[:, c * LANES:(c + 1) * LANES]


def _proj(x2, g_mix, w_in, seq):
    t, d = x2.shape
    tm = PROJ_TM
    assert t % tm == 0 and seq % tm == 0 and A_KV == LANES and A_KV_HEADS == 2
    nseq = seq // tm
    tables = _rope_tables(seq)
    slab = lambda i: (0, i, 0)
    const = lambda i: (0, 0)
    tab = lambda i: (i % nseq, 0)
    widths = (A_Q, A_KV_HEADS * LANES, A_KV_HEADS * LANES, B_W, B_W, B_W)
    return pl.pallas_call(
        _proj_kernel,
        grid=(t // tm,),
        in_specs=[pl.BlockSpec((tm, d), lambda i: (i, 0)), pl.BlockSpec((1, d), const),
                  pl.BlockSpec(w_in.shape, const)] + [pl.BlockSpec((tm, LANES), tab)] * 3,
        out_specs=[pl.BlockSpec((w // LANES, tm, LANES), slab) for w in widths],
        out_shape=[jax.ShapeDtypeStruct((w // LANES, t, LANES), dt)
                   for w, dt in zip(widths, (BF16, BF16, BF16, F32, F32, F32))],
        scratch_shapes=[pltpu.VMEM(w_in.shape, BF16)],
        compiler_params=_cparams(("arbitrary",), VMEM_LIMIT),
        name="proj",
    )(x2, g_mix.reshape(1, d), w_in, *tables)


def _band_bias(bias_ref, qb, kw, hw):
    kk = lax.broadcasted_iota(jnp.int32, (kw, qb), 0)
    qq = lax.broadcasted_iota(jnp.int32, (kw, qb), 1)
    for t, delta in enumerate((0, -hw, -2 * hw)):
        bias_ref[t] = jnp.where(jnp.abs(kk + delta - qq) <= hw, 0.0, NEG_INF)


def _band_scratch(qb, kw, slots):
    n = 2 * slots
    return ([pltpu.VMEM((kw, 2 * qb), F32)] * n + [pltpu.VMEM((kw, 2 * qb), BF16)] * n
            + [pltpu.VMEM((SUBLANES, 2 * qb), F32)] * n + [pltpu.VMEM((LANES, kw), BF16)] * n
            + [pltpu.VMEM((SUBLANES, 2 * qb), F32)] * n)


def _band_pipeline(q_ref, k_ref, v_ref, bias_ref, o_ref, lse_ref, bufs, *, nblocks, geometry, stage_vt,
                   sink_row=None):
    qb = ATT_QB
    n = len(bufs) // 5
    slots = n // 2
    s_bufs, p_bufs, ml_bufs, vt_bufs, mx_bufs = (bufs[i * n:(i + 1) * n] for i in range(5))
    kw = s_bufs[0].shape[0]
    for s_buf, p_buf, ml_buf, vt_buf, mx_buf in zip(s_bufs, p_bufs, ml_bufs, vt_bufs, mx_bufs):
        mx_buf[...] = jnp.zeros_like(mx_buf)
        s_buf[...] = jnp.zeros_like(s_buf)
        p_buf[...] = jnp.zeros_like(p_buf)
        ml_buf[...] = jnp.ones_like(ml_buf)
        vt_buf[...] = jnp.zeros_like(vt_buf)

    def rows(block):
        q0, k0, kind = geometry(jnp.clip(block, 0, nblocks - 1))
        return pl.ds(pl.multiple_of(q0, BF16_ROWS), qb), pl.ds(pl.multiple_of(k0, BF16_ROWS), kw), kind

    def scores(block, s_buf, mx_buf):
        q_rows, k_rows, kind = rows(block)
        q = q_ref[q_rows, :]
        lo = lax.broadcasted_iota(jnp.int32, q.shape, 1) < HEAD_DIM
        q2 = jnp.concatenate([jnp.where(lo, q, 0.0), jnp.where(lo, 0.0, q)], axis=0).astype(BF16)
        s = lax.dot_general(k_ref[k_rows, :].astype(BF16), q2, (((1,), (1,)), ((), ())),
                            preferred_element_type=F32)
        bias = bias_ref[kind]
        for hh in range(2):
            s_h = s[:, hh * qb:(hh + 1) * qb] + bias
            s_buf[:, hh * qb:(hh + 1) * qb] = s_h
            mx_buf[0:1, hh * qb:(hh + 1) * qb] = jnp.max(s_h, axis=0, keepdims=True)

    def softmax(block, s_buf, mx_buf, p_buf, ml_buf, vt_buf):
        if stage_vt:
            _, k_rows, _ = rows(block)
            vt_buf[...] = v_ref[k_rows, :].astype(F32).T.astype(BF16)
        m = mx_buf[0:1, :]
        if sink_row is not None:
            m = jnp.maximum(m, sink_row)
        p = jnp.exp2(s_buf[...] - m)
        l = jnp.sum(p, axis=0, keepdims=True)
        if sink_row is not None:
            l = l + jnp.exp2(sink_row - m)
        p_buf[...] = p.astype(BF16)
        ml_buf[0:1, :] = m
        ml_buf[1:2, :] = l

    def output(block, p_buf, ml_buf, vt_buf):
        q_rows, k_rows, _ = rows(block)
        if stage_vt:
            o_t = jnp.dot(vt_buf[...], p_buf[...], preferred_element_type=F32)
        else:
            o_t = lax.dot_general(v_ref[k_rows, :].astype(BF16), p_buf[...], (((0,), (0,)), ((), ())),
                                  preferred_element_type=F32)
        o_tc = jnp.concatenate([o_t[:HEAD_DIM, :qb], o_t[HEAD_DIM:, qb:]], axis=0)
        top = lax.broadcasted_iota(jnp.int32, (LANES, qb), 0) < HEAD_DIM
        m, l = ml_buf[0:1, :], ml_buf[1:2, :]
        inv = 1.0 / l
        o_ref[q_rows, :] = (o_tc * jnp.where(top, inv[:, :qb], inv[:, qb:])).T
        if lse_ref is not None:
            lse = m + jnp.log2(l)
            lse_ref[q_rows, :] = jnp.where(top, lse[:, :qb], lse[:, qb:]).T

    def trip_pair(jj, carry):
        for parity in range(2):
            for slot in range(slots):
                t = (2 * jj + parity) * slots + slot
                rd, wr = (1 - parity) * slots + slot, parity * slots + slot
                output(t - 2 * slots, p_bufs[rd], ml_bufs[rd], vt_bufs[rd])
                softmax(t - slots, s_bufs[rd], mx_bufs[rd], p_bufs[wr], ml_bufs[wr], vt_bufs[wr])
                scores(t, s_bufs[wr], mx_bufs[wr])
        return carry

    trips = nblocks // slots + 2
    assert nblocks % slots == 0 and trips % 2 == 0
    lax.fori_loop(0, trips // 2, trip_pair, 0)


def _segment_geometry(block, n, hw, base=0):
    qb = ATT_QB
    nblk = n // qb
    seg, jb = block // nblk, block % nblk
    qs = jb * qb
    ks = jnp.clip(qs - hw, 0, n - (qb + 2 * hw))
    kind = jnp.where(jb == 0, 0, jnp.where(jb == nblk - 1, 2, 1))
    return base + seg * n + qs, base + seg * n + ks, kind


def _attn_a_kernel(sink_ref, q_ref, k_ref, v_ref, o_ref, bias_ref, *bufs, seq):
    j = pl.program_id(1)
    qb, hw = ATT_QB, A_HALF_WINDOW
    _band_bias(bias_ref, qb, qb + 2 * hw, hw)
    col = lax.broadcasted_iota(jnp.int32, (1, 2 * qb), 1)
    sink_row = jnp.where(col < qb, sink_ref[2 * j], sink_ref[2 * j + 1]) * LOG2_E
    _band_pipeline(q_ref, k_ref, v_ref, bias_ref, o_ref, None, bufs, nblocks=seq // qb,
                   geometry=functools.partial(_segment_geometry, n=seq, hw=hw), stage_vt=False,
                   sink_row=sink_row)


def _attn_a(qa, ka, va, sink, batch, seq):
    qb, hw = ATT_QB, A_HALF_WINDOW
    assert seq % qb == 0 and seq >= qb + 2 * hw and A_KV == LANES
    qspec = pl.BlockSpec((None, seq, LANES), lambda b, j: (j, b, 0))
    per_kv = A_GROUP * HEAD_DIM // LANES
    kvspec = pl.BlockSpec((None, seq, LANES), lambda b, j: (j // per_kv, b, 0))
    return pl.pallas_call(
        functools.partial(_attn_a_kernel, seq=seq),
        grid=(batch, A_Q // LANES),
        in_specs=[pl.BlockSpec(memory_space=pltpu.SMEM), qspec, kvspec, kvspec],
        out_specs=qspec,
        out_shape=jax.ShapeDtypeStruct(qa.shape, F32),
        scratch_shapes=[pltpu.VMEM((3, qb + 2 * hw, qb), F32)] + _band_scratch(qb, qb + 2 * hw, ATT_A_SLOTS),
        compiler_params=_cparams(("parallel", "parallel"), VMEM_LIMIT),
        name="attn_a",
    )(sink, qa, ka, va)


def _attn_b_kernel(q_ref, k_ref, v_ref, out_ref, bias_ref, tmp_ref, qs_ref, ks_ref, vs_ref, os_ref, ls_ref,
                   *bufs, seq):
    (w1, d1), (w4, d4), (w16, d16) = DILATED_PATTERNS
    hw = w1 // (2 * d1)
    qb = ATT_QB
    _band_bias(bias_ref, qb, qb + 2 * hw, hw)
    n4, n16 = seq // d4, seq // d16
    step = d16 // d4
    chunk = 512
    for src, dst in ((q_ref, qs_ref), (k_ref, ks_ref), (v_ref, vs_ref)):
        for c in range(seq // chunk):
            dst[c * chunk:(c + 1) * chunk, :] = src[c * chunk:(c + 1) * chunk, :].astype(dst.dtype)
        for r in range(d4):
            tmp_ref[r * n4:(r + 1) * n4, :] = src[pl.ds(r, n4, stride=d4), :]
        for c in range(seq // chunk):
            dst[seq + c * chunk:seq + (c + 1) * chunk, :] = tmp_ref[c * chunk:(c + 1) * chunk, :].astype(dst.dtype)
        for r in range(d16):
            dst[2 * seq + r * n16:2 * seq + (r + 1) * n16, :] = tmp_ref[
                pl.ds((r % d4) * n4 + r // d4, n16, stride=step), :].astype(dst.dtype)

    bps = seq // qb
    lg = [(seq // d // qb).bit_length() - 1 for (_, d) in DILATED_PATTERNS]

    def geometry(block):
        pat, within = block // bps, block % bps
        sh = jnp.where(pat == 0, lg[0], jnp.where(pat == 1, lg[1], lg[2]))
        seg, jb = within >> sh, within & ((1 << sh) - 1)
        n = qb << sh
        qs = jb * qb
        ks = jnp.clip(qs - hw, 0, n - (qb + 2 * hw))
        kind = jnp.where(jb == 0, 0, jnp.where(jb == (1 << sh) - 1, 2, 1))
        base = pat * seq + seg * n
        return base + qs, base + ks, kind

    _band_pipeline(qs_ref, ks_ref, vs_ref, bias_ref, os_ref, ls_ref, bufs,
                   nblocks=len(DILATED_PATTERNS) * bps, geometry=geometry, stage_vt=True)

    def merge(r, carry):
        rows16 = pl.ds(pl.multiple_of(2 * seq + r * n16, SUBLANES), n16)
        rows4 = pl.ds(seq + (r % d4) * n4 + r // d4, n16, stride=step)
        rows1 = pl.ds(r, n16, stride=d16)
        la, lb, lc = ls_ref[rows1, :], ls_ref[rows4, :], ls_ref[rows16, :]
        top = jnp.maximum(jnp.maximum(la, lb), lc)
        wa, wb, wc = jnp.exp2(la - top), jnp.exp2(lb - top), jnp.exp2(lc - top)
        num = wa * os_ref[rows1, :] + wb * os_ref[rows4, :] + wc * os_ref[rows16, :]
        out_ref[rows1, :] = num * (1.0 / (wa + wb + wc))
        return carry

    lax.fori_loop(0, d16, merge, 0)


def _attn_b(qb_, kb, vb, batch, seq):
    (w1, d1), (w4, d4), (w16, d16) = DILATED_PATTERNS
    hw = w1 // (2 * d1)
    qb = ATT_QB
    assert d1 == 1 and d16 % d4 == 0 and w4 // (2 * d4) == hw and w16 // (2 * d16) == hw
    assert seq % 512 == 0 and seq // d16 >= qb + 2 * hw
    for (_, d) in DILATED_PATTERNS:
        nblk = seq // d // qb
        assert nblk * d * qb == seq and nblk & (nblk - 1) == 0
    spec = pl.BlockSpec((None, seq, LANES), lambda b, p: (p, b, 0))
    npat = len(DILATED_PATTERNS)
    return pl.pallas_call(
        functools.partial(_attn_b_kernel, seq=seq),
        grid=(batch, B_W // LANES),
        in_specs=[spec] * 3,
        out_specs=spec,
        out_shape=jax.ShapeDtypeStruct(qb_.shape, F32),
        scratch_shapes=[pltpu.VMEM((3, qb + 2 * hw, qb), F32), pltpu.VMEM((seq, LANES), F32),
                        pltpu.VMEM((npat * seq, LANES), F32), pltpu.VMEM((npat * seq, LANES), BF16),
                        pltpu.VMEM((npat * seq, LANES), BF16), pltpu.VMEM((npat * seq, LANES), F32),
                        pltpu.VMEM((npat * seq, LANES), F32)] + _band_scratch(qb, qb + 2 * hw, ATT_B_SLOTS),
        compiler_params=_cparams(("parallel", "parallel"), VMEM_LIMIT),
        name="attn_b",
    )(qb_, kb, vb)


def _mix_kernel(x_ref, oa_ref, ob_ref, ga_ref, gb_ref, wo_f32_ref, gf_ref, wr_ref,
                x1_ref, h2_ref, aff_ref, wo_ref):
    @pl.when(pl.program_id(0) == 0)
    def _cast_weights():
        wo_ref[...] = wo_f32_ref[...].astype(BF16)

    oa = jnp.concatenate([oa_ref[c] for c in range(oa_ref.shape[0])], axis=1)
    ob = jnp.concatenate([ob_ref[c] for c in range(ob_ref.shape[0])], axis=1)
    na = _rms(oa, ga_ref[...]).astype(BF16)
    nb = _rms(ob, gb_ref[...]).astype(BF16)
    x1 = (x_ref[...] + jnp.dot(na, wo_ref[:A_Q, :], preferred_element_type=F32)
          + jnp.dot(nb, wo_ref[A_Q:, :], preferred_element_type=F32))
    x1_ref[...] = x1
    h2 = _rms(x1, gf_ref[...])
    tm = h2.shape[0]
    for c in range(h2.shape[1] // LANES):
        h2_ref[pl.ds(c, tm, stride=SUBLANES), :] = h2[:, c * LANES:(c + 1) * LANES]
    ne = aff_ref.shape[0]
    dn = (((1,), (1,)), ((), ()))
    h_hi = h2.astype(BF16)
    h_mid = (h2 - h_hi.astype(F32)).astype(BF16)
    r = lax.dot_general(wr_ref[...], h_hi, dn, preferred_element_type=F32)
    logits = r[:ne] + r[ne:] + lax.dot_general(wr_ref[:ne, :], h_mid, dn, preferred_element_type=F32)
    e = jnp.exp(logits - jnp.max(logits, axis=0, keepdims=True))
    aff_ref[...] = e / jnp.sum(e, axis=0, keepdims=True)


def _mix(x2, oa, ob, g_out_a, g_out_b, w_out, g_ffn, w_router, batch, seq):
    t, d = x2.shape
    tm = MIX_TM
    assert seq % tm == 0 and d % LANES == 0 and d // LANES == SUBLANES
    ns = seq // tm
    row = lambda i: (i, 0)
    const = lambda i: (0, 0)
    ne = w_router.shape[1]
    w_hi = w_router.T.astype(BF16)
    w_mid = (w_router.T - w_hi.astype(F32)).astype(BF16)
    wr = jnp.concatenate([w_hi, w_mid], axis=0)
    return pl.pallas_call(
        _mix_kernel,
        grid=(t // tm,),
        in_specs=[pl.BlockSpec((tm, d), row), pl.BlockSpec((A_Q // LANES, tm, LANES), lambda i: (0, i, 0)),
                  pl.BlockSpec((B_W // LANES, tm, LANES), lambda i: (0, i, 0)),
                  pl.BlockSpec((1, A_Q), const), pl.BlockSpec((1, B_W), const),
                  pl.BlockSpec(w_out.shape, const), pl.BlockSpec((1, d), const),
                  pl.BlockSpec((2 * ne, d), const)],
        out_specs=[pl.BlockSpec((tm, d), row), pl.BlockSpec((tm * SUBLANES, LANES), row),
                   pl.BlockSpec((None, ne, tm), lambda i: (i // ns, 0, i % ns))],
        out_shape=[jax.ShapeDtypeStruct((t, d), F32),
                   jax.ShapeDtypeStruct((t * SUBLANES, LANES), F32),
                   jax.ShapeDtypeStruct((batch, ne, seq), F32)],
        scratch_shapes=[pltpu.VMEM(w_out.shape, BF16)],
        compiler_params=_cparams(("arbitrary",), VMEM_LIMIT),
        name="mix",
    )(x2, oa, ob, g_out_a.reshape(1, -1), g_out_b.reshape(1, -1), w_out, g_ffn.reshape(1, d), wr)


def _lane_cumsum(mask_f, tri):
    rows, s = mask_f.shape
    carry = jnp.zeros((rows, 1), F32)
    out = []
    for c in range(s // LANES):
        inc = jnp.dot(mask_f[:, c * LANES:(c + 1) * LANES].astype(BF16), tri,
                      preferred_element_type=F32) + carry
        out.append(inc)
        carry = inc[:, LANES - 1:LANES]
    return jnp.concatenate(out, axis=1)


def _route_kernel(aff_ref, idx_e_ref, idx_l_ref, gate_ref, *, cap, seq):
    b = pl.program_id(0)
    aff = aff_ref[...]
    ne = aff.shape[0]

    def search(i, thr):
        lo = 28 - 2 * i
        for digit in (1, 2, 3):
            cand = thr | (jnp.int32(digit) << lo)
            cnt = jnp.sum((aff >= lax.bitcast_convert_type(cand, F32)).astype(F32), axis=1, keepdims=True)
            best = jnp.where(cnt >= cap, cand, thr if digit == 1 else best)
        return best

    thr = lax.bitcast_convert_type(lax.fori_loop(0, 15, search, jnp.zeros((ne, 1), jnp.int32)), F32)
    gt = aff > thr
    eq = aff == thr
    n_gt = jnp.sum(gt.astype(F32), axis=1, keepdims=True)
    ri = lax.broadcasted_iota(jnp.int32, (LANES, LANES), 0)
    ci = lax.broadcasted_iota(jnp.int32, (LANES, LANES), 1)
    tri = (ri <= ci).astype(BF16)
    eq_f = eq.astype(F32)
    eq_rank = _lane_cumsum(eq_f, tri) - eq_f
    sel = gt | (eq & (eq_rank < cap - n_gt))
    sel_f = sel.astype(F32)
    slot = _lane_cumsum(sel_f, tri) - sel_f
    key_t = jnp.where(sel, slot, -1.0).T
    tok = lax.broadcasted_iota(jnp.int32, (SUBLANES, seq), 1)
    piece = lax.broadcasted_iota(jnp.int32, (SUBLANES, seq), 0)
    tok_rows = jnp.where(piece == 0, tok // 64, jnp.where(piece == 1, tok % 64, 0)).astype(F32)
    p_iota = lax.broadcasted_iota(jnp.int32, (1, cap), 1).astype(F32)
    for e in range(ne):
        onehot = (key_t[:, e:e + 1] == p_iota).astype(BF16)
        g = aff[e:e + 1, :]
        g_hi = g.astype(BF16).astype(F32)
        g_mid = (g - g_hi).astype(BF16).astype(F32)
        g_lo = g - g_hi - g_mid
        lhs = jnp.where(piece == 2, g_hi, jnp.where(piece == 3, g_mid, jnp.where(piece == 4, g_lo, tok_rows)))
        res = jnp.dot(lhs.astype(BF16), onehot, preferred_element_type=F32)
        tile_row = (res[0:1] * 64.0 + res[1:2]).astype(jnp.int32) * SUBLANES
        idx_l_ref[e] = tile_row
        idx_e_ref[e] = tile_row + b * (seq * SUBLANES)
        gate_ref[e] = res[2:3] + res[3:4] + res[4:5]


def _route(aff_t, cap):
    batch, ne, seq = aff_t.shape
    assert seq % LANES == 0 and seq <= 64 * 256
    return pl.pallas_call(
        functools.partial(_route_kernel, cap=cap, seq=seq),
        grid=(batch,),
        in_specs=[pl.BlockSpec((None, ne, seq), lambda b: (b, 0, 0))],
        out_specs=[pl.BlockSpec((ne, 1, cap), lambda b: (0, 0, b)), pl.BlockSpec((ne, 1, cap), lambda b: (b, 0, 0)),
                   pl.BlockSpec((ne, 1, cap), lambda b: (b, 0, 0))],
        out_shape=[jax.ShapeDtypeStruct((ne, 1, batch * cap), jnp.int32),
                   jax.ShapeDtypeStruct((batch * ne, 1, cap), jnp.int32),
                   jax.ShapeDtypeStruct((batch * ne, 1, cap), F32)],
        compiler_params=_cparams(("parallel",), VMEM_LIMIT),
        name="route",
    )(aff_t)


def _experts_kernel(idx_ref, idx_next_ref, h_hbm, wg_ref, wu_ref, wd_ref, y_ref, rows_ref, xe_ref, sems,
                    *, ne, nf):
    e, f = pl.program_id(0), pl.program_id(1)
    ntok, d = xe_ref.shape
    slot = e % 2
    chunks = ntok // FFN_TM
    per_chunk = ntok // (nf * chunks)

    def row_copy(idx_smem, i, buf):
        src = pl.multiple_of(idx_smem[0, 0, i], SUBLANES)
        dst = pl.multiple_of(i * SUBLANES, SUBLANES)
        return pltpu.make_async_copy(h_hbm.at[pl.ds(src, SUBLANES), :],
                                     rows_ref.at[buf, pl.ds(dst, SUBLANES), :], sems.at[buf])

    def wait_rows(buf):
        pltpu.make_async_copy(h_hbm.at[pl.ds(0, ntok * SUBLANES), :], rows_ref.at[buf], sems.at[buf]).wait()

    @pl.when((e == 0) & (f == 0))
    def _first_gather():
        def issue(i, carry):
            row_copy(idx_ref, i, 0).start()
            return carry

        lax.fori_loop(0, ntok, issue, 0, unroll=8)

    @pl.when(f == 0)
    def _repack():
        wait_rows(slot)
        for m in range(chunks):
            xe_ref[m * FFN_TM:(m + 1) * FFN_TM, :] = jnp.concatenate(
                [rows_ref[slot, pl.ds(m * FFN_TM * SUBLANES + c, FFN_TM, stride=SUBLANES), :]
                 for c in range(d // LANES)], axis=1).astype(BF16)
        y_ref[...] = jnp.zeros_like(y_ref)

    wg = wg_ref[...].astype(BF16)
    wu = wu_ref[...].astype(BF16)
    wd = wd_ref[...].astype(BF16)
    for m in range(chunks):
        xe = xe_ref[m * FFN_TM:(m + 1) * FFN_TM, :]
        a = jnp.dot(xe, wg, preferred_element_type=F32)
        u = jnp.dot(xe, wu, preferred_element_type=F32)
        hmid = (a * jax.nn.sigmoid(a) * u).astype(BF16)
        y_ref[m * FFN_TM:(m + 1) * FFN_TM, :] += jnp.dot(hmid, wd, preferred_element_type=F32)
        for i in range(per_chunk):
            row_copy(idx_next_ref, (f * chunks + m) * per_chunk + i, 1 - slot).start()

    @pl.when((e == ne - 1) & (f == nf - 1))
    def _drain():
        wait_rows(1 - slot)


def _experts(idx_e, h2_rows, w_gate, w_up, w_down):
    ne, d, ff = w_gate.shape
    ntok = idx_e.shape[-1]
    tf = FFN_TF
    nf = ff // tf
    assert ff % tf == 0 and ntok % FFN_TM == 0 and ntok % (nf * (ntok // FFN_TM)) == 0 and d // LANES == SUBLANES
    return pl.pallas_call(
        functools.partial(_experts_kernel, ne=ne, nf=nf),
        grid=(ne, nf),
        in_specs=[pl.BlockSpec((1, 1, ntok), lambda e, f: (e, 0, 0), memory_space=pltpu.SMEM),
                  pl.BlockSpec((1, 1, ntok), lambda e, f: (jnp.minimum(e + 1, ne - 1), 0, 0),
                               memory_space=pltpu.SMEM),
                  pl.BlockSpec(memory_space=pl.ANY),
                  pl.BlockSpec((None, d, tf), lambda e, f: (e, 0, f)),
                  pl.BlockSpec((None, d, tf), lambda e, f: (e, 0, f)),
                  pl.BlockSpec((None, tf, d), lambda e, f: (e, f, 0))],
        out_specs=pl.BlockSpec((None, ntok, d), lambda e, f: (e, 0, 0)),
        out_shape=jax.ShapeDtypeStruct((ne, ntok, d), F32),
        scratch_shapes=[pltpu.VMEM((2, ntok * SUBLANES, LANES), F32), pltpu.VMEM((ntok, d), BF16),
                        pltpu.SemaphoreType.DMA((2,))],
        compiler_params=_cparams(("arbitrary", "arbitrary"), VMEM_LIMIT),
        name="experts",
    )(idx_e, idx_e, h2_rows, w_gate, w_up, w_down)


def _combine_kernel(idx_ref, gate_ref, y_ref, x1_ref, g_ref, o_ref, acc_ref, ybuf_ref, *, batch):
    b, e = pl.program_id(0), pl.program_id(1)
    cap, d = y_ref.shape
    tm = x1_ref.shape[0]
    cur = b % 2

    @pl.when(b < batch)
    def _scatter():
        @pl.when(e == 0)
        def _init():
            acc_ref[cur] = jnp.zeros(acc_ref.shape[1:], F32)

        gate_col = jnp.broadcast_to(gate_ref[0], (LANES, cap)).T
        for c in range(d // LANES):
            ybuf_ref[pl.ds(c, cap, stride=SUBLANES), :] = y_ref[:, c * LANES:(c + 1) * LANES] * gate_col

        def group(k, carry):
            rows, sums = [], []
            for u in range(RMW_UNROLL):
                i = k * RMW_UNROLL + u
                t = pl.ds(pl.multiple_of(idx_ref[0, 0, i], SUBLANES), SUBLANES)
                yi = ybuf_ref[pl.ds(pl.multiple_of(i * SUBLANES, SUBLANES), SUBLANES), :]
                rows.append(t)
                sums.append(acc_ref[cur, t, :] + yi)
            for t, s in zip(rows, sums):
                acc_ref[cur, t, :] = s
            return carry

        lax.fori_loop(0, cap // RMW_UNROLL, group, 0)

    @pl.when(b > 0)
    def _emit():
        base = pl.multiple_of(e * tm * SUBLANES, SUBLANES)
        moe = jnp.concatenate([acc_ref[1 - cur, pl.ds(base + c, tm, stride=SUBLANES), :]
                               for c in range(d // LANES)], axis=1)
        o_ref[...] = _rms(x1_ref[...] + moe, g_ref[...])


def _combine(idx_l, gates, y, x1, g_final, batch, seq):
    ne = y.shape[0]
    t, d = x1.shape
    cap = idx_l.shape[-1]
    tm = seq // ne
    assert cap % RMW_UNROLL == 0 and d // LANES == SUBLANES and tm * ne == seq and tm % SUBLANES == 0
    scattered = lambda b: jnp.minimum(b, batch - 1)
    rows = lambda b, e: (jnp.maximum(b - 1, 0) * ne + jnp.where(b > 0, e, 0), 0)
    slots = lambda b, e: (scattered(b) * ne + e, 0, 0)
    return pl.pallas_call(
        functools.partial(_combine_kernel, batch=batch),
        grid=(batch + 1, ne),
        in_specs=[pl.BlockSpec((1, 1, cap), slots, memory_space=pltpu.SMEM), pl.BlockSpec((1, 1, cap), slots),
                  pl.BlockSpec((None, cap, d), lambda b, e: (e, scattered(b), 0)),
                  pl.BlockSpec((tm, d), rows), pl.BlockSpec((1, d), lambda b, e: (0, 0))],
        out_specs=pl.BlockSpec((tm, d), rows),
        out_shape=jax.ShapeDtypeStruct((t, d), F32),
        scratch_shapes=[pltpu.VMEM((2, seq * SUBLANES, LANES), F32), pltpu.VMEM((cap * SUBLANES, LANES), F32)],
        compiler_params=_cparams(("arbitrary", "arbitrary"), VMEM_LIMIT),
        name="combine",
    )(idx_l, gates, y, x1, g_final.reshape(1, d))


def kernel(x, g_mix, w_in, a_sink, g_out_a, g_out_b, w_out, g_ffn, w_router, w_gate, w_up, w_down, g_final):
    batch, seq, d = x.shape
    depth = w_in.shape[0]
    ne = w_router.shape[-1]
    cap = CAPACITY_FACTOR * seq // ne
    assert depth == 1, "the last kernel fuses the MoE residual with the output norm of a single layer"
    l = 0
    x2 = x.reshape(batch * seq, d)
    qa, ka, va, qb, kb, vb = _proj(x2, g_mix[l], w_in[l], seq)
    oa = _attn_a(qa, ka, va, a_sink[l], batch, seq)
    ob = _attn_b(qb, kb, vb, batch, seq)
    x1, h2_rows, aff_t = _mix(x2, oa, ob, g_out_a[l], g_out_b[l], w_out[l], g_ffn[l], w_router[l], batch, seq)
    idx_e, idx_l, gates = _route(aff_t, cap)
    y = _experts(idx_e, h2_rows, w_gate[l], w_up[l], w_down[l])
    out = _combine(idx_l, gates, y, x1, g_final, batch, seq)
    return out.reshape(batch, seq, d)
```

```python
import functools

import jax
import jax.numpy as jnp
from jax import lax
from jax.experimental import pallas as pl
from jax.experimental.pallas import tpu as pltpu

F32 = jnp.float32
BF16 = jnp.bfloat16

LANES = 128
MXU_COLS = 256
SUBLANES = 8
BF16_ROWS = 16
VMEM_LIMIT = 56 * 1024 * 1024

HEAD_DIM = 64
A_HEADS = 8
A_KV_HEADS = 2
A_GROUP = A_HEADS // A_KV_HEADS
A_HALF_WINDOW = 128
B_HEADS = 8
DILATED_PATTERNS = ((128, 1), (512, 4), (2048, 16))
A_Q = A_HEADS * HEAD_DIM
A_KV = A_KV_HEADS * HEAD_DIM
B_W = B_HEADS * HEAD_DIM
ROPE_THETA = 500000.0
ROT_DIM = HEAD_DIM // 4
CAPACITY_FACTOR = 2
NORM_EPS = 1e-6
NEG_INF = -1e30
LOG2_E = 1.4426950408889634

PROJ_TM = 1024
ATT_QB = 128
ATT_A_SLOTS = 2
ATT_B_SLOTS = 4
MIX_TM = 1024
FFN_TF = 512
FFN_TM = 1024
RMW_UNROLL = 8


def _cparams(sem, vmem=None):
    return pltpu.CompilerParams(dimension_semantics=sem, vmem_limit_bytes=vmem)


def _rope_tables(seq_len):
    half = ROT_DIM // 2
    inv_freq = ROPE_THETA ** (-jnp.arange(0, ROT_DIM, 2, dtype=F32) / ROT_DIM)
    ang = jnp.arange(seq_len, dtype=F32)[:, None] * inv_freq[None, :]
    cos = jnp.tile(jnp.cos(ang), (1, LANES // half))
    sin = jnp.tile(jnp.sin(ang), (1, LANES // half))
    j = jnp.arange(LANES) % HEAD_DIM
    cos_t = jnp.where(j < ROT_DIM, cos, 1.0)
    sin_a = jnp.where(j < half, -sin, 0.0)
    sin_b = jnp.where((j >= half) & (j < ROT_DIM), sin, 0.0)
    return cos_t, sin_a, sin_b


def _rms(x, g):
    return x * lax.rsqrt(jnp.mean(x * x, axis=-1, keepdims=True) + NORM_EPS) * g


def _proj_kernel(x_ref, g_ref, w_ref, cos_ref, sa_ref, sb_ref,
                 qa_ref, ka_ref, va_ref, qb_ref, kb_ref, vb_ref, wbf_ref):
    @pl.when(pl.program_id(0) == 0)
    def _cast_weights():
        wbf_ref[...] = w_ref[...].astype(BF16)

    h = _rms(x_ref[...], g_ref[...]).astype(BF16)
    cos, sa, sb = cos_ref[...], sa_ref[...], sb_ref[...]
    half = ROT_DIM // 2

    def rope(t):
        return t * cos + pltpu.roll(t, LANES - half, 1) * sa + pltpu.roll(t, half, 1) * sb

    def blocks(col0, width):
        for n in range(width // MXU_COLS):
            c0 = col0 + n * MXU_COLS
            yield 2 * n, jnp.dot(h, wbf_ref[:, c0:c0 + MXU_COLS], preferred_element_type=F32)

    scale = HEAD_DIM ** -0.5 * LOG2_E
    for c, p in blocks(0, A_Q):
        for i in range(2):
            qa_ref[c + i] = (rope(p[:, i * LANES:(i + 1) * LANES]) * scale).astype(qa_ref.dtype)
    (_, kv), = blocks(A_Q, 2 * A_KV)
    first = lax.broadcasted_iota(jnp.int32, (kv.shape[0], LANES), 1) < HEAD_DIM
    for ref, t in ((ka_ref, rope(kv[:, :A_KV])), (va_ref, kv[:, A_KV:])):
        swapped = pltpu.roll(t, HEAD_DIM, 1)
        ref[0] = jnp.where(first, t, swapped).astype(ref.dtype)
        ref[1] = jnp.where(first, swapped, t).astype(ref.dtype)
    for c, p in blocks(A_Q + 2 * A_KV, B_W):
        for i in range(2):
            qb_ref[c + i] = rope(p[:, i * LANES:(i + 1) * LANES]) * scale
    for c, p in blocks(A_Q + 2 * A_KV + B_W, B_W):
        for i in range(2):
            kb_ref[c + i] = rope(p[:, i * LANES:(i + 1) * LANES])
    for c, p in blocks(A_Q + 2 * A_KV + 2 * B_W, B_W):
        for i in range(2):
            vb_ref[c + i] = p[:, i * LANES:(i + 1) * LANES]


def _proj(x2, g_mix, w_in, seq):
    t, d = x2.shape
    tm = PROJ_TM
    assert t % tm == 0 and seq % tm == 0 and A_KV == LANES and A_KV_HEADS == 2
    nseq = seq // tm
    tables = _rope_tables(seq)
    slab = lambda i: (0, i, 0)
    const = lambda i: (0, 0)
    tab = lambda i: (i % nseq, 0)
    widths = (A_Q, A_KV_HEADS * LANES, A_KV_HEADS * LANES, B_W, B_W, B_W)
    return pl.pallas_call(
        _proj_kernel,
        grid=(t // tm,),
        in_specs=[pl.BlockSpec((tm, d), lambda i: (i, 0)), pl.BlockSpec((1, d), const),
                  pl.BlockSpec(w_in.shape, const)] + [pl.BlockSpec((tm, LANES), tab)] * 3,
        out_specs=[pl.BlockSpec((w // LANES, tm, LANES), slab) for w in widths],
        out_shape=[jax.ShapeDtypeStruct((w // LANES, t, LANES), dt)
                   for w, dt in zip(widths, (BF16, BF16, BF16, F32, F32, F32))],
        scratch_shapes=[pltpu.VMEM(w_in.shape, BF16)],
        compiler_params=_cparams(("arbitrary",), VMEM_LIMIT),
        name="proj",
    )(x2, g_mix.reshape(1, d), w_in, *tables)


def _band_bias(bias_ref, qb, kw, hw):
    kk = lax.broadcasted_iota(jnp.int32, (kw, qb), 0)
    qq = lax.broadcasted_iota(jnp.int32, (kw, qb), 1)
    for t, delta in enumerate((0, -hw, -2 * hw)):
        bias_ref[t] = jnp.where(jnp.abs(kk + delta - qq) <= hw, 0.0, NEG_INF)


def _band_scratch(qb, kw, slots):
    n = 2 * slots
    return ([pltpu.VMEM((kw, 2 * qb), F32)] * n + [pltpu.VMEM((kw, 2 * qb), BF16)] * n
            + [pltpu.VMEM((SUBLANES, 2 * qb), F32)] * n + [pltpu.VMEM((LANES, kw), BF16)] * n
            + [pltpu.VMEM((SUBLANES, 2 * qb), F32)] * n)


def _band_pipeline(q_ref, k_ref, v_ref, bias_ref, o_ref, lse_ref, bufs, *, nblocks, geometry, stage_vt,
                   sink_row=None):
    qb = ATT_QB
    n = len(bufs) // 5
    slots = n // 2
    s_bufs, p_bufs, ml_bufs, vt_bufs, mx_bufs = (bufs[i * n:(i + 1) * n] for i in range(5))
    kw = s_bufs[0].shape[0]
    for s_buf, p_buf, ml_buf, vt_buf, mx_buf in zip(s_bufs, p_bufs, ml_bufs, vt_bufs, mx_bufs):
        mx_buf[...] = jnp.zeros_like(mx_buf)
        s_buf[...] = jnp.zeros_like(s_buf)
        p_buf[...] = jnp.zeros_like(p_buf)
        ml_buf[...] = jnp.ones_like(ml_buf)
        vt_buf[...] = jnp.zeros_like(vt_buf)

    def rows(block):
        q0, k0, kind = geometry(jnp.clip(block, 0, nblocks - 1))
        return pl.ds(pl.multiple_of(q0, BF16_ROWS), qb), pl.ds(pl.multiple_of(k0, BF16_ROWS), kw), kind

    def scores(block, s_buf, mx_buf):
        q_rows, k_rows, kind = rows(block)
        q = q_ref[q_rows, :]
        lo = lax.broadcasted_iota(jnp.int32, q.shape, 1) < HEAD_DIM
        q2 = jnp.concatenate([jnp.where(lo, q, 0.0), jnp.where(lo, 0.0, q)], axis=0).astype(BF16)
        s = lax.dot_general(k_ref[k_rows, :].astype(BF16), q2, (((1,), (1,)), ((), ())),
                            preferred_element_type=F32)
        bias = bias_ref[kind]
        for hh in range(2):
            s_h = s[:, hh * qb:(hh + 1) * qb] + bias
            s_buf[:, hh * qb:(hh + 1) * qb] = s_h
            mx_buf[0:1, hh * qb:(hh + 1) * qb] = jnp.max(s_h, axis=0, keepdims=True)

    def softmax(block, s_buf, mx_buf, p_buf, ml_buf, vt_buf):
        if stage_vt:
            _, k_rows, _ = rows(block)
            vt_buf[...] = v_ref[k_rows, :].astype(F32).T.astype(BF16)
        m = mx_buf[0:1, :]
        if sink_row is not None:
            m = jnp.maximum(m, sink_row)
        p = jnp.exp2(s_buf[...] - m)
        l = jnp.sum(p, axis=0, keepdims=True)
        if sink_row is not None:
            l = l + jnp.exp2(sink_row - m)
        p_buf[...] = p.astype(BF16)
        ml_buf[0:1, :] = m
        ml_buf[1:2, :] = l

    def output(block, p_buf, ml_buf, vt_buf):
        q_rows, k_rows, _ = rows(block)
        if stage_vt:
            o_t = jnp.dot(vt_buf[...], p_buf[...], preferred_element_type=F32)
        else:
            o_t = lax.dot_general(v_ref[k_rows, :].astype(BF16), p_buf[...], (((0,), (0,)), ((), ())),
                                  preferred_element_type=F32)
        o_tc = jnp.concatenate([o_t[:HEAD_DIM, :qb], o_t[HEAD_DIM:, qb:]], axis=0)
        top = lax.broadcasted_iota(jnp.int32, (LANES, qb), 0) < HEAD_DIM
        m, l = ml_buf[0:1, :], ml_buf[1:2, :]
        inv = 1.0 / l
        o_ref[q_rows, :] = (o_tc * jnp.where(top, inv[:, :qb], inv[:, qb:])).T
        if lse_ref is not None:
            lse = m + jnp.log2(l)
            lse_ref[q_rows, :] = jnp.where(top, lse[:, :qb], lse[:, qb:]).T

    def trip_pair(jj, carry):
        for parity in range(2):
            for slot in range(slots):
                t = (2 * jj + parity) * slots + slot
                rd, wr = (1 - parity) * slots + slot, parity * slots + slot
                output(t - 2 * slots, p_bufs[rd], ml_bufs[rd], vt_bufs[rd])
                softmax(t - slots, s_bufs[rd], mx_bufs[rd], p_bufs[wr], ml_bufs[wr], vt_bufs[wr])
                scores(t, s_bufs[wr], mx_bufs[wr])
        return carry

    trips = nblocks // slots + 2
    assert nblocks % slots == 0 and trips % 2 == 0
    lax.fori_loop(0, trips // 2, trip_pair, 0)


def _segment_geometry(block, n, hw, base=0):
    qb = ATT_QB
    nblk = n // qb
    seg, jb = block // nblk, block % nblk
    qs = jb * qb
    ks = jnp.clip(qs - hw, 0, n - (qb + 2 * hw))
    kind = jnp.where(jb == 0, 0, jnp.where(jb == nblk - 1, 2, 1))
    return base + seg * n + qs, base + seg * n + ks, kind


def _attn_a_kernel(sink_ref, q_ref, k_ref, v_ref, o_ref, bias_ref, *bufs, seq):
    j = pl.program_id(1)
    qb, hw = ATT_QB, A_HALF_WINDOW
    _band_bias(bias_ref, qb, qb + 2 * hw, hw)
    col = lax.broadcasted_iota(jnp.int32, (1, 2 * qb), 1)
    sink_row = jnp.where(col < qb, sink_ref[2 * j], sink_ref[2 * j + 1]) * LOG2_E
    _band_pipeline(q_ref, k_ref, v_ref, bias_ref, o_ref, None, bufs, nblocks=seq // qb,
                   geometry=functools.partial(_segment_geometry, n=seq, hw=hw), stage_vt=False,
                   sink_row=sink_row)


def _attn_a(qa, ka, va, sink, batch, seq):
    qb, hw = ATT_QB, A_HALF_WINDOW
    assert seq % qb == 0 and seq >= qb + 2 * hw and A_KV == LANES
    qspec = pl.BlockSpec((None, seq, LANES), lambda b, j: (j, b, 0))
    per_kv = A_GROUP * HEAD_DIM // LANES
    kvspec = pl.BlockSpec((None, seq, LANES), lambda b, j: (j // per_kv, b, 0))
    return pl.pallas_call(
        functools.partial(_attn_a_kernel, seq=seq),
        grid=(batch, A_Q // LANES),
        in_specs=[pl.BlockSpec(memory_space=pltpu.SMEM), qspec, kvspec, kvspec],
        out_specs=qspec,
        out_shape=jax.ShapeDtypeStruct(qa.shape, F32),
        scratch_shapes=[pltpu.VMEM((3, qb + 2 * hw, qb), F32)] + _band_scratch(qb, qb + 2 * hw, ATT_A_SLOTS),
        compiler_params=_cparams(("parallel", "parallel"), VMEM_LIMIT),
        name="attn_a",
    )(sink, qa, ka, va)


def _attn_b_kernel(q_ref, k_ref, v_ref, out_ref, bias_ref, tmp_ref, qs_ref, ks_ref, vs_ref, os_ref, ls_ref,
                   *bufs, seq):
    (w1, d1), (w4, d4), (w16, d16) = DILATED_PATTERNS
    hw = w1 // (2 * d1)
    qb = ATT_QB
    _band_bias(bias_ref, qb, qb + 2 * hw, hw)
    n4, n16 = seq // d4, seq // d16
    step = d16 // d4
    chunk = 512
    for src, dst in ((q_ref, qs_ref), (k_ref, ks_ref), (v_ref, vs_ref)):
        for c in range(seq // chunk):
            dst[c * chunk:(c + 1) * chunk, :] = src[c * chunk:(c + 1) * chunk, :].astype(dst.dtype)
        for r in range(d4):
            tmp_ref[r * n4:(r + 1) * n4, :] = src[pl.ds(r, n4, stride=d4), :]
        for c in range(seq // chunk):
            dst[seq + c * chunk:seq + (c + 1) * chunk, :] = tmp_ref[c * chunk:(c + 1) * chunk, :].astype(dst.dtype)
        for r in range(d16):
            dst[2 * seq + r * n16:2 * seq + (r + 1) * n16, :] = tmp_ref[
                pl.ds((r % d4) * n4 + r // d4, n16, stride=step), :].astype(dst.dtype)

    bps = seq // qb
    lg = [(seq // d // qb).bit_length() - 1 for (_, d) in DILATED_PATTERNS]

    def geometry(block):
        pat, within = block // bps, block % bps
        sh = jnp.where(pat == 0, lg[0], jnp.where(pat == 1, lg[1], lg[2]))
        seg, jb = within >> sh, within & ((1 << sh) - 1)
        n = qb << sh
        qs = jb * qb
        ks = jnp.clip(qs - hw, 0, n - (qb + 2 * hw))
        kind = jnp.where(jb == 0, 0, jnp.where(jb == (1 << sh) - 1, 2, 1))
        base = pat * seq + seg * n
        return base + qs, base + ks, kind

    _band_pipeline(qs_ref, ks_ref, vs_ref, bias_ref, os_ref, ls_ref, bufs,
                   nblocks=len(DILATED_PATTERNS) * bps, geometry=geometry, stage_vt=True)

    def merge(r, carry):
        rows16 = pl.ds(pl.multiple_of(2 * seq + r * n16, SUBLANES), n16)
        rows4 = pl.ds(seq + (r % d4) * n4 + r // d4, n16, stride=step)
        rows1 = pl.ds(r, n16, stride=d16)
        la, lb, lc = ls_ref[rows1, :], ls_ref[rows4, :], ls_ref[rows16, :]
        top = jnp.maximum(jnp.maximum(la, lb), lc)
        wa, wb, wc = jnp.exp2(la - top), jnp.exp2(lb - top), jnp.exp2(lc - top)
        num = wa * os_ref[rows1, :] + wb * os_ref[rows4, :] + wc * os_ref[rows16, :]
        out_ref[rows1, :] = num * (1.0 / (wa + wb + wc))
        return carry

    lax.fori_loop(0, d16, merge, 0)


def _attn_b(qb_, kb, vb, batch, seq):
    (w1, d1), (w4, d4), (w16, d16) = DILATED_PATTERNS
    hw = w1 // (2 * d1)
    qb = ATT_QB
    assert d1 == 1 and d16 % d4 == 0 and w4 // (2 * d4) == hw and w16 // (2 * d16) == hw
    assert seq % 512 == 0 and seq // d16 >= qb + 2 * hw
    for (_, d) in DILATED_PATTERNS:
        nblk = seq // d // qb
        assert nblk * d * qb == seq and nblk & (nblk - 1) == 0
    spec = pl.BlockSpec((None, seq, LANES), lambda b, p: (p, b, 0))
    npat = len(DILATED_PATTERNS)
    return pl.pallas_call(
        functools.partial(_attn_b_kernel, seq=seq),
        grid=(batch, B_W // LANES),
        in_specs=[spec] * 3,
        out_specs=spec,
        out_shape=jax.ShapeDtypeStruct(qb_.shape, F32),
        scratch_shapes=[pltpu.VMEM((3, qb + 2 * hw, qb), F32), pltpu.VMEM((seq, LANES), F32),
                        pltpu.VMEM((npat * seq, LANES), F32), pltpu.VMEM((npat * seq, LANES), BF16),
                        pltpu.VMEM((npat * seq, LANES), BF16), pltpu.VMEM((npat * seq, LANES), F32),
                        pltpu.VMEM((npat * seq, LANES), F32)] + _band_scratch(qb, qb + 2 * hw, ATT_B_SLOTS),
        compiler_params=_cparams(("parallel", "parallel"), VMEM_LIMIT),
        name="attn_b",
    )(qb_, kb, vb)


def _mix_kernel(x_ref, oa_ref, ob_ref, ga_ref, gb_ref, wo_f32_ref, gf_ref, wr_ref,
                x1_ref, h2_ref, aff_ref, wo_ref):
    @pl.when(pl.program_id(0) == 0)
    def _cast_weights():
        wo_ref[...] = wo_f32_ref[...].astype(BF16)

    oa = jnp.concatenate([oa_ref[c] for c in range(oa_ref.shape[0])], axis=1)
    ob = jnp.concatenate([ob_ref[c] for c in range(ob_ref.shape[0])], axis=1)
    mixed = jnp.concatenate([_rms(oa, ga_ref[...]), _rms(ob, gb_ref[...])], axis=1).astype(BF16)
    for n in range(x_ref.shape[1] // MXU_COLS):
        cols = slice(n * MXU_COLS, (n + 1) * MXU_COLS)
        x1_ref[:, cols] = x_ref[:, cols] + jnp.dot(mixed, wo_ref[:, cols], preferred_element_type=F32)
    x1 = x1_ref[...]
    h2 = _rms(x1, gf_ref[...])
    tm = h2.shape[0]
    for c in range(h2.shape[1] // LANES):
        h2_ref[pl.ds(c, tm, stride=SUBLANES), :] = h2[:, c * LANES:(c + 1) * LANES]
    ne = aff_ref.shape[0]
    dn = (((1,), (1,)), ((), ()))
    h_hi = h2.astype(BF16)
    h_mid = (h2 - h_hi.astype(F32)).astype(BF16)
    r = lax.dot_general(wr_ref[...], h_hi, dn, preferred_element_type=F32)
    logits = r[:ne] + r[ne:] + lax.dot_general(wr_ref[:ne, :], h_mid, dn, preferred_element_type=F32)
    e = jnp.exp(logits - jnp.max(logits, axis=0, keepdims=True))
    aff_ref[...] = e / jnp.sum(e, axis=0, keepdims=True)


def _mix(x2, oa, ob, g_out_a, g_out_b, w_out, g_ffn, w_router, batch, seq):
    t, d = x2.shape
    tm = MIX_TM
    assert seq % tm == 0 and d % LANES == 0 and d // LANES == SUBLANES
    ns = seq // tm
    row = lambda i: (i, 0)
    const = lambda i: (0, 0)
    ne = w_router.shape[1]
    w_hi = w_router.T.astype(BF16)
    w_mid = (w_router.T - w_hi.astype(F32)).astype(BF16)
    wr = jnp.concatenate([w_hi, w_mid], axis=0)
    return pl.pallas_call(
        _mix_kernel,
        grid=(t // tm,),
        in_specs=[pl.BlockSpec((tm, d), row), pl.BlockSpec((A_Q // LANES, tm, LANES), lambda i: (0, i, 0)),
                  pl.BlockSpec((B_W // LANES, tm, LANES), lambda i: (0, i, 0)),
                  pl.BlockSpec((1, A_Q), const), pl.BlockSpec((1, B_W), const),
                  pl.BlockSpec(w_out.shape, const), pl.BlockSpec((1, d), const),
                  pl.BlockSpec((2 * ne, d), const)],
        out_specs=[pl.BlockSpec((tm, d), row), pl.BlockSpec((tm * SUBLANES, LANES), row),
                   pl.BlockSpec((None, ne, tm), lambda i: (i // ns, 0, i % ns))],
        out_shape=[jax.ShapeDtypeStruct((t, d), F32),
                   jax.ShapeDtypeStruct((t * SUBLANES, LANES), F32),
                   jax.ShapeDtypeStruct((batch, ne, seq), F32)],
        scratch_shapes=[pltpu.VMEM(w_out.shape, BF16)],
        compiler_params=_cparams(("arbitrary",), VMEM_LIMIT),
        name="mix",
    )(x2, oa, ob, g_out_a.reshape(1, -1), g_out_b.reshape(1, -1), w_out, g_ffn.reshape(1, d), wr)


def _lane_cumsum(mask_f, tri):
    rows, s = mask_f.shape
    carry = jnp.zeros((rows, 1), F32)
    out = []
    for c in range(s // LANES):
        inc = jnp.dot(mask_f[:, c * LANES:(c + 1) * LANES].astype(BF16), tri,
                      preferred_element_type=F32) + carry
        out.append(inc)
        carry = inc[:, LANES - 1:LANES]
    return jnp.concatenate(out, axis=1)


def _route_kernel(aff_ref, idx_e_ref, idx_l_ref, gate_ref, *, cap, seq):
    b = pl.program_id(0)
    aff = aff_ref[...]
    ne = aff.shape[0]

    def search(i, thr):
        lo = 28 - 2 * i
        for digit in (1, 2, 3):
            cand = thr | (jnp.int32(digit) << lo)
            cnt = jnp.sum((aff >= lax.bitcast_convert_type(cand, F32)).astype(F32), axis=1, keepdims=True)
            best = jnp.where(cnt >= cap, cand, thr if digit == 1 else best)
        return best

    thr = lax.bitcast_convert_type(lax.fori_loop(0, 15, search, jnp.zeros((ne, 1), jnp.int32)), F32)
    gt = aff > thr
    eq = aff == thr
    n_gt = jnp.sum(gt.astype(F32), axis=1, keepdims=True)
    ri = lax.broadcasted_iota(jnp.int32, (LANES, LANES), 0)
    ci = lax.broadcasted_iota(jnp.int32, (LANES, LANES), 1)
    tri = (ri <= ci).astype(BF16)
    eq_f = eq.astype(F32)
    eq_rank = _lane_cumsum(eq_f, tri) - eq_f
    sel = gt | (eq & (eq_rank < cap - n_gt))
    sel_f = sel.astype(F32)
    slot = _lane_cumsum(sel_f, tri) - sel_f
    key_t = jnp.where(sel, slot, -1.0).T
    tok = lax.broadcasted_iota(jnp.int32, (SUBLANES, seq), 1)
    piece = lax.broadcasted_iota(jnp.int32, (SUBLANES, seq), 0)
    tok_rows = jnp.where(piece == 0, tok // 64, jnp.where(piece == 1, tok % 64, 0)).astype(F32)
    p_iota = lax.broadcasted_iota(jnp.int32, (1, cap), 1).astype(F32)
    for e in range(ne):
        onehot = (key_t[:, e:e + 1] == p_iota).astype(BF16)
        g = aff[e:e + 1, :]
        g_hi = g.astype(BF16).astype(F32)
        g_mid = (g - g_hi).astype(BF16).astype(F32)
        g_lo = g - g_hi - g_mid
        lhs = jnp.where(piece == 2, g_hi, jnp.where(piece == 3, g_mid, jnp.where(piece == 4, g_lo, tok_rows)))
        res = jnp.dot(lhs.astype(BF16), onehot, preferred_element_type=F32)
        tile_row = (res[0:1] * 64.0 + res[1:2]).astype(jnp.int32) * SUBLANES
        idx_l_ref[e] = tile_row
        idx_e_ref[e] = tile_row + b * (seq * SUBLANES)
        gate_ref[e] = res[2:3] + res[3:4] + res[4:5]


def _route(aff_t, cap):
    batch, ne, seq = aff_t.shape
    assert seq % LANES == 0 and seq <= 64 * 256
    return pl.pallas_call(
        functools.partial(_route_kernel, cap=cap, seq=seq),
        grid=(batch,),
        in_specs=[pl.BlockSpec((None, ne, seq), lambda b: (b, 0, 0))],
        out_specs=[pl.BlockSpec((ne, 1, cap), lambda b: (0, 0, b)), pl.BlockSpec((ne, 1, cap), lambda b: (b, 0, 0)),
                   pl.BlockSpec((ne, 1, cap), lambda b: (b, 0, 0))],
        out_shape=[jax.ShapeDtypeStruct((ne, 1, batch * cap), jnp.int32),
                   jax.ShapeDtypeStruct((batch * ne, 1, cap), jnp.int32),
                   jax.ShapeDtypeStruct((batch * ne, 1, cap), F32)],
        compiler_params=_cparams(("parallel",), VMEM_LIMIT),
        name="route",
    )(aff_t)


def _experts_kernel(idx_ref, idx_next_ref, h_hbm, wg_ref, wu_ref, wd_ref, y_ref, rows_ref, xe_ref, sems,
                    *, ne, nf):
    e, f = pl.program_id(0), pl.program_id(1)
    ntok, d = xe_ref.shape
    slot = e % 2
    chunks = ntok // FFN_TM
    per_chunk = ntok // (nf * chunks)

    def row_copy(idx_smem, i, buf):
        src = pl.multiple_of(idx_smem[0, 0, i], SUBLANES)
        dst = pl.multiple_of(i * SUBLANES, SUBLANES)
        return pltpu.make_async_copy(h_hbm.at[pl.ds(src, SUBLANES), :],
                                     rows_ref.at[buf, pl.ds(dst, SUBLANES), :], sems.at[buf])

    def wait_rows(buf):
        pltpu.make_async_copy(h_hbm.at[pl.ds(0, ntok * SUBLANES), :], rows_ref.at[buf], sems.at[buf]).wait()

    @pl.when((e == 0) & (f == 0))
    def _first_gather():
        def issue(i, carry):
            row_copy(idx_ref, i, 0).start()
            return carry

        lax.fori_loop(0, ntok, issue, 0, unroll=8)

    @pl.when(f == 0)
    def _repack():
        wait_rows(slot)
        for m in range(chunks):
            xe_ref[m * FFN_TM:(m + 1) * FFN_TM, :] = jnp.concatenate(
                [rows_ref[slot, pl.ds(m * FFN_TM * SUBLANES + c, FFN_TM, stride=SUBLANES), :]
                 for c in range(d // LANES)], axis=1).astype(BF16)
        y_ref[...] = jnp.zeros_like(y_ref)

    wg = wg_ref[...].astype(BF16)
    wu = wu_ref[...].astype(BF16)
    wd = wd_ref[...].astype(BF16)
    for m in range(chunks):
        xe = xe_ref[m * FFN_TM:(m + 1) * FFN_TM, :]
        hmid = []
        for n in range(wg.shape[1] // MXU_COLS):
            cols = slice(n * MXU_COLS, (n + 1) * MXU_COLS)
            a = jnp.dot(xe, wg[:, cols], preferred_element_type=F32)
            u = jnp.dot(xe, wu[:, cols], preferred_element_type=F32)
            hmid.append((a * jax.nn.sigmoid(a) * u).astype(BF16))
        hmid = jnp.concatenate(hmid, axis=1)
        y_ref[m * FFN_TM:(m + 1) * FFN_TM, :] += jnp.dot(hmid, wd, preferred_element_type=F32)
        for i in range(per_chunk):
            row_copy(idx_next_ref, (f * chunks + m) * per_chunk + i, 1 - slot).start()

    @pl.when((e == ne - 1) & (f == nf - 1))
    def _drain():
        wait_rows(1 - slot)


def _experts(idx_e, h2_rows, w_gate, w_up, w_down):
    ne, d, ff = w_gate.shape
    ntok = idx_e.shape[-1]
    tf = FFN_TF
    nf = ff // tf
    assert ff % tf == 0 and ntok % FFN_TM == 0 and ntok % (nf * (ntok // FFN_TM)) == 0 and d // LANES == SUBLANES
    return pl.pallas_call(
        functools.partial(_experts_kernel, ne=ne, nf=nf),
        grid=(ne, nf),
        in_specs=[pl.BlockSpec((1, 1, ntok), lambda e, f: (e, 0, 0), memory_space=pltpu.SMEM),
                  pl.BlockSpec((1, 1, ntok), lambda e, f: (jnp.minimum(e + 1, ne - 1), 0, 0),
                               memory_space=pltpu.SMEM),
                  pl.BlockSpec(memory_space=pl.ANY),
                  pl.BlockSpec((None, d, tf), lambda e, f: (e, 0, f)),
                  pl.BlockSpec((None, d, tf), lambda e, f: (e, 0, f)),
                  pl.BlockSpec((None, tf, d), lambda e, f: (e, f, 0))],
        out_specs=pl.BlockSpec((None, ntok, d), lambda e, f: (e, 0, 0)),
        out_shape=jax.ShapeDtypeStruct((ne, ntok, d), F32),
        scratch_shapes=[pltpu.VMEM((2, ntok * SUBLANES, LANES), F32), pltpu.VMEM((ntok, d), BF16),
                        pltpu.SemaphoreType.DMA((2,))],
        compiler_params=_cparams(("arbitrary", "arbitrary"), VMEM_LIMIT),
        name="experts",
    )(idx_e, idx_e, h2_rows, w_gate, w_up, w_down)


def _combine_kernel(idx_ref, gate_ref, y_ref, x1_ref, g_ref, o_ref, acc_ref, ybuf_ref, *, batch):
    b, e = pl.program_id(0), pl.program_id(1)
    cap, d = y_ref.shape
    tm = x1_ref.shape[0]
    cur = b % 2

    @pl.when(b < batch)
    def _scatter():
        @pl.when(e == 0)
        def _init():
            acc_ref[cur] = jnp.zeros(acc_ref.shape[1:], F32)

        gate_col = jnp.broadcast_to(gate_ref[0], (LANES, cap)).T
        for c in range(d // LANES):
            ybuf_ref[pl.ds(c, cap, stride=SUBLANES), :] = y_ref[:, c * LANES:(c + 1) * LANES] * gate_col

        def group(k, carry):
            rows, sums = [], []
            for u in range(RMW_UNROLL):
                i = k * RMW_UNROLL + u
                t = pl.ds(pl.multiple_of(idx_ref[0, 0, i], SUBLANES), SUBLANES)
                yi = ybuf_ref[pl.ds(pl.multiple_of(i * SUBLANES, SUBLANES), SUBLANES), :]
                rows.append(t)
                sums.append(acc_ref[cur, t, :] + yi)
            for t, s in zip(rows, sums):
                acc_ref[cur, t, :] = s
            return carry

        lax.fori_loop(0, cap // RMW_UNROLL, group, 0)

    @pl.when(b > 0)
    def _emit():
        base = pl.multiple_of(e * tm * SUBLANES, SUBLANES)
        moe = jnp.concatenate([acc_ref[1 - cur, pl.ds(base + c, tm, stride=SUBLANES), :]
                               for c in range(d // LANES)], axis=1)
        o_ref[...] = _rms(x1_ref[...] + moe, g_ref[...])


def _combine(idx_l, gates, y, x1, g_final, batch, seq):
    ne = y.shape[0]
    t, d = x1.shape
    cap = idx_l.shape[-1]
    tm = seq // ne
    assert cap % RMW_UNROLL == 0 and d // LANES == SUBLANES and tm * ne == seq and tm % SUBLANES == 0
    scattered = lambda b: jnp.minimum(b, batch - 1)
    rows = lambda b, e: (jnp.maximum(b - 1, 0) * ne + jnp.where(b > 0, e, 0), 0)
    slots = lambda b, e: (scattered(b) * ne + e, 0, 0)
    return pl.pallas_call(
        functools.partial(_combine_kernel, batch=batch),
        grid=(batch + 1, ne),
        in_specs=[pl.BlockSpec((1, 1, cap), slots, memory_space=pltpu.SMEM), pl.BlockSpec((1, 1, cap), slots),
                  pl.BlockSpec((None, cap, d), lambda b, e: (e, scattered(b), 0)),
                  pl.BlockSpec((tm, d), rows), pl.BlockSpec((1, d), lambda b, e: (0, 0))],
        out_specs=pl.BlockSpec((tm, d), rows),
        out_shape=jax.ShapeDtypeStruct((t, d), F32),
        scratch_shapes=[pltpu.VMEM((2, seq * SUBLANES, LANES), F32), pltpu.VMEM((cap * SUBLANES, LANES), F32)],
        compiler_params=_cparams(("arbitrary", "arbitrary"), VMEM_LIMIT),
        name="combine",
    )(idx_l, gates, y, x1, g_final.reshape(1, d))


def kernel(x, g_mix, w_in, a_sink, g_out_a, g_out_b, w_out, g_ffn, w_router, w_gate, w_up, w_down, g_final):
    batch, seq, d = x.shape
    depth = w_in.shape[0]
    ne = w_router.shape[-1]
    cap = CAPACITY_FACTOR * seq // ne
    assert depth == 1, "the last kernel fuses the MoE residual with the output norm of a single layer"
    l = 0
    x2 = x.reshape(batch * seq, d)
    qa, ka, va, qb, kb, vb = _proj(x2, g_mix[l], w_in[l], seq)
    oa = _attn_a(qa, ka, va, a_sink[l], batch, seq)
    ob = _attn_b(qb, kb, vb, batch, seq)
    x1, h2_rows, aff_t = _mix(x2, oa, ob, g_out_a[l], g_out_b[l], w_out[l], g_ffn[l], w_router[l], batch, seq)
    idx_e, idx_l, gates = _route(aff_t, cap)
    y = _experts(idx_e, h2_rows, w_gate[l], w_up[l], w_down[l])
    out = _combine(idx_l, gates, y, x1, g_final, batch, seq)
    return out.reshape(batch, seq, d)
```

```python
import functools

import jax
import jax.numpy as jnp
from jax import lax
from jax.experimental import pallas as pl
from jax.experimental.pallas import tpu as pltpu

F32 = jnp.float32
BF16 = jnp.bfloat16

LANES = 128
MXU_COLS = 256
SUBLANES = 8
BF16_ROWS = 16
VMEM_LIMIT = 56 * 1024 * 1024

HEAD_DIM = 64
A_HEADS = 8
A_KV_HEADS = 2
A_GROUP = A_HEADS // A_KV_HEADS
A_HALF_WINDOW = 128
B_HEADS = 8
DILATED_PATTERNS = ((128, 1), (512, 4), (2048, 16))
A_Q = A_HEADS * HEAD_DIM
A_KV = A_KV_HEADS * HEAD_DIM
B_W = B_HEADS * HEAD_DIM
ROPE_THETA = 500000.0
ROT_DIM = HEAD_DIM // 4
CAPACITY_FACTOR = 2
NORM_EPS = 1e-6
NEG_INF = -1e30
LOG2_E = 1.4426950408889634

PROJ_TM = 1024
ATT_QB = 128
ATT_A_SLOTS = 2
ATT_B_SLOTS = 4
MIX_TM = 1024
FFN_TF = 512
FFN_TM = 1024
RMW_UNROLL = 8
COMBINE_EXPERTS = 2


def _cparams(sem, vmem=None):
    return pltpu.CompilerParams(dimension_semantics=sem, vmem_limit_bytes=vmem)


def _rope_tables(seq_len):
    half = ROT_DIM // 2
    inv_freq = ROPE_THETA ** (-jnp.arange(0, ROT_DIM, 2, dtype=F32) / ROT_DIM)
    ang = jnp.arange(seq_len, dtype=F32)[:, None] * inv_freq[None, :]
    cos = jnp.tile(jnp.cos(ang), (1, LANES // half))
    sin = jnp.tile(jnp.sin(ang), (1, LANES // half))
    j = jnp.arange(LANES) % HEAD_DIM
    cos_t = jnp.where(j < ROT_DIM, cos, 1.0)
    sin_a = jnp.where(j < half, -sin, 0.0)
    sin_b = jnp.where((j >= half) & (j < ROT_DIM), sin, 0.0)
    return cos_t, sin_a, sin_b


def _rms(x, g):
    return x * lax.rsqrt(jnp.mean(x * x, axis=-1, keepdims=True) + NORM_EPS) * g


def _proj_kernel(x_ref, g_ref, w_ref, cos_ref, sa_ref, sb_ref,
                 qa_ref, ka_ref, va_ref, qb_ref, kb_ref, vb_ref, wbf_ref):
    @pl.when(pl.program_id(0) == 0)
    def _cast_weights():
        wbf_ref[...] = w_ref[...].astype(BF16)

    h = _rms(x_ref[...], g_ref[...]).astype(BF16)
    cos, sa, sb = cos_ref[...], sa_ref[...], sb_ref[...]
    half = ROT_DIM // 2

    def rope(t):
        return t * cos + pltpu.roll(t, LANES - half, 1) * sa + pltpu.roll(t, half, 1) * sb

    def blocks(col0, width):
        for n in range(width // MXU_COLS):
            c0 = col0 + n * MXU_COLS
            yield 2 * n, jnp.dot(h, wbf_ref[:, c0:c0 + MXU_COLS], preferred_element_type=F32)

    scale = HEAD_DIM ** -0.5 * LOG2_E
    for c, p in blocks(0, A_Q):
        for i in range(2):
            qa_ref[c + i] = (rope(p[:, i * LANES:(i + 1) * LANES]) * scale).astype(qa_ref.dtype)
    (_, kv), = blocks(A_Q, 2 * A_KV)
    first = lax.broadcasted_iota(jnp.int32, (kv.shape[0], LANES), 1) < HEAD_DIM
    for ref, t in ((ka_ref, rope(kv[:, :A_KV])), (va_ref, kv[:, A_KV:])):
        swapped = pltpu.roll(t, HEAD_DIM, 1)
        ref[0] = jnp.where(first, t, swapped).astype(ref.dtype)
        ref[1] = jnp.where(first, swapped, t).astype(ref.dtype)
    for c, p in blocks(A_Q + 2 * A_KV, B_W):
        for i in range(2):
            qb_ref[c + i] = rope(p[:, i * LANES:(i + 1) * LANES]) * scale
    for c, p in blocks(A_Q + 2 * A_KV + B_W, B_W):
        for i in range(2):
            kb_ref[c + i] = rope(p[:, i * LANES:(i + 1) * LANES])
    for c, p in blocks(A_Q + 2 * A_KV + 2 * B_W, B_W):
        for i in range(2):
            vb_ref[c + i] = p[:, i * LANES:(i + 1) * LANES]


def _proj(x2, g_mix, w_in, seq):
    t, d = x2.shape
    tm = PROJ_TM
    assert t % tm == 0 and seq % tm == 0 and A_KV == LANES and A_KV_HEADS == 2
    nseq = seq // tm
    tables = _rope_tables(seq)
    slab = lambda i: (0, i, 0)
    const = lambda i: (0, 0)
    tab = lambda i: (i % nseq, 0)
    widths = (A_Q, A_KV_HEADS * LANES, A_KV_HEADS * LANES, B_W, B_W, B_W)
    return pl.pallas_call(
        _proj_kernel,
        grid=(t // tm,),
        in_specs=[pl.BlockSpec((tm, d), lambda i: (i, 0)), pl.BlockSpec((1, d), const),
                  pl.BlockSpec(w_in.shape, const)] + [pl.BlockSpec((tm, LANES), tab)] * 3,
        out_specs=[pl.BlockSpec((w // LANES, tm, LANES), slab) for w in widths],
        out_shape=[jax.ShapeDtypeStruct((w // LANES, t, LANES), dt)
                   for w, dt in zip(widths, (BF16, BF16, BF16, F32, F32, F32))],
        scratch_shapes=[pltpu.VMEM(w_in.shape, BF16)],
        compiler_params=_cparams(("arbitrary",), VMEM_LIMIT),
        name="proj",
    )(x2, g_mix.reshape(1, d), w_in, *tables)


def _band_bias(bias_ref, qb, kw, hw):
    kk = lax.broadcasted_iota(jnp.int32, (kw, qb), 0)
    qq = lax.broadcasted_iota(jnp.int32, (kw, qb), 1)
    for t, delta in enumerate((0, -hw, -2 * hw)):
        bias_ref[t] = jnp.where(jnp.abs(kk + delta - qq) <= hw, 0.0, NEG_INF)


def _band_scratch(qb, kw, slots):
    n = 2 * slots
    return ([pltpu.VMEM((kw, 2 * qb), F32)] * n + [pltpu.VMEM((kw, 2 * qb), BF16)] * n
            + [pltpu.VMEM((SUBLANES, 2 * qb), F32)] * n + [pltpu.VMEM((LANES, kw), BF16)] * n
            + [pltpu.VMEM((SUBLANES, 2 * qb), F32)] * n)


def _band_pipeline(q_ref, k_ref, v_ref, bias_ref, o_ref, lse_ref, bufs, *, nblocks, geometry, stage_vt,
                   sink_row=None):
    qb = ATT_QB
    n = len(bufs) // 5
    slots = n // 2
    s_bufs, p_bufs, ml_bufs, vt_bufs, mx_bufs = (bufs[i * n:(i + 1) * n] for i in range(5))
    kw = s_bufs[0].shape[0]
    for s_buf, p_buf, ml_buf, vt_buf, mx_buf in zip(s_bufs, p_bufs, ml_bufs, vt_bufs, mx_bufs):
        mx_buf[...] = jnp.zeros_like(mx_buf)
        s_buf[...] = jnp.zeros_like(s_buf)
        p_buf[...] = jnp.zeros_like(p_buf)
        ml_buf[...] = jnp.ones_like(ml_buf)
        vt_buf[...] = jnp.zeros_like(vt_buf)

    def rows(block):
        q0, k0, kind = geometry(jnp.clip(block, 0, nblocks - 1))
        return pl.ds(pl.multiple_of(q0, BF16_ROWS), qb), pl.ds(pl.multiple_of(k0, BF16_ROWS), kw), kind

    def scores(block, s_buf, mx_buf):
        q_rows, k_rows, kind = rows(block)
        q = q_ref[q_rows, :]
        lo = lax.broadcasted_iota(jnp.int32, q.shape, 1) < HEAD_DIM
        q2 = jnp.concatenate([jnp.where(lo, q, 0.0), jnp.where(lo, 0.0, q)], axis=0).astype(BF16)
        s = lax.dot_general(k_ref[k_rows, :].astype(BF16), q2, (((1,), (1,)), ((), ())),
                            preferred_element_type=F32)
        bias = bias_ref[kind]
        for hh in range(2):
            s_h = s[:, hh * qb:(hh + 1) * qb] + bias
            s_buf[:, hh * qb:(hh + 1) * qb] = s_h
            mx_buf[0:1, hh * qb:(hh + 1) * qb] = jnp.max(s_h, axis=0, keepdims=True)

    def softmax(block, s_buf, mx_buf, p_buf, ml_buf, vt_buf):
        if stage_vt:
            _, k_rows, _ = rows(block)
            vt_buf[...] = v_ref[k_rows, :].astype(F32).T.astype(BF16)
        m = mx_buf[0:1, :]
        if sink_row is not None:
            m = jnp.maximum(m, sink_row)
        p = jnp.exp2(s_buf[...] - m)
        l = jnp.sum(p, axis=0, keepdims=True)
        if sink_row is not None:
            l = l + jnp.exp2(sink_row - m)
        p_buf[...] = p.astype(BF16)
        ml_buf[0:1, :] = m
        ml_buf[1:2, :] = l

    def output(block, p_buf, ml_buf, vt_buf):
        q_rows, k_rows, _ = rows(block)
        if stage_vt:
            o_t = jnp.dot(vt_buf[...], p_buf[...], preferred_element_type=F32)
        else:
            o_t = lax.dot_general(v_ref[k_rows, :].astype(BF16), p_buf[...], (((0,), (0,)), ((), ())),
                                  preferred_element_type=F32)
        o_tc = jnp.concatenate([o_t[:HEAD_DIM, :qb], o_t[HEAD_DIM:, qb:]], axis=0)
        top = lax.broadcasted_iota(jnp.int32, (LANES, qb), 0) < HEAD_DIM
        m, l = ml_buf[0:1, :], ml_buf[1:2, :]
        inv = 1.0 / l
        o_ref[q_rows, :] = (o_tc * jnp.where(top, inv[:, :qb], inv[:, qb:])).T
        if lse_ref is not None:
            lse = m + jnp.log2(l)
            lse_ref[q_rows, :] = jnp.where(top, lse[:, :qb], lse[:, qb:]).T

    def trip_pair(jj, carry):
        for parity in range(2):
            for slot in range(slots):
                t = (2 * jj + parity) * slots + slot
                rd, wr = (1 - parity) * slots + slot, parity * slots + slot
                output(t - 2 * slots, p_bufs[rd], ml_bufs[rd], vt_bufs[rd])
                softmax(t - slots, s_bufs[rd], mx_bufs[rd], p_bufs[wr], ml_bufs[wr], vt_bufs[wr])
                scores(t, s_bufs[wr], mx_bufs[wr])
        return carry

    trips = nblocks // slots + 2
    assert nblocks % slots == 0 and trips % 2 == 0
    lax.fori_loop(0, trips // 2, trip_pair, 0)


def _segment_geometry(block, n, hw, base=0):
    qb = ATT_QB
    nblk = n // qb
    seg, jb = block // nblk, block % nblk
    qs = jb * qb
    ks = jnp.clip(qs - hw, 0, n - (qb + 2 * hw))
    kind = jnp.where(jb == 0, 0, jnp.where(jb == nblk - 1, 2, 1))
    return base + seg * n + qs, base + seg * n + ks, kind


def _attn_a_kernel(sink_ref, q_ref, k_ref, v_ref, o_ref, bias_ref, *bufs, seq):
    j = pl.program_id(1)
    qb, hw = ATT_QB, A_HALF_WINDOW
    _band_bias(bias_ref, qb, qb + 2 * hw, hw)
    col = lax.broadcasted_iota(jnp.int32, (1, 2 * qb), 1)
    sink_row = jnp.where(col < qb, sink_ref[2 * j], sink_ref[2 * j + 1]) * LOG2_E
    _band_pipeline(q_ref, k_ref, v_ref, bias_ref, o_ref, None, bufs, nblocks=seq // qb,
                   geometry=functools.partial(_segment_geometry, n=seq, hw=hw), stage_vt=False,
                   sink_row=sink_row)


def _attn_a(qa, ka, va, sink, batch, seq):
    qb, hw = ATT_QB, A_HALF_WINDOW
    assert seq % qb == 0 and seq >= qb + 2 * hw and A_KV == LANES
    qspec = pl.BlockSpec((None, seq, LANES), lambda b, j: (j, b, 0))
    per_kv = A_GROUP * HEAD_DIM // LANES
    kvspec = pl.BlockSpec((None, seq, LANES), lambda b, j: (j // per_kv, b, 0))
    return pl.pallas_call(
        functools.partial(_attn_a_kernel, seq=seq),
        grid=(batch, A_Q // LANES),
        in_specs=[pl.BlockSpec(memory_space=pltpu.SMEM), qspec, kvspec, kvspec],
        out_specs=qspec,
        out_shape=jax.ShapeDtypeStruct(qa.shape, F32),
        scratch_shapes=[pltpu.VMEM((3, qb + 2 * hw, qb), F32)] + _band_scratch(qb, qb + 2 * hw, ATT_A_SLOTS),
        compiler_params=_cparams(("parallel", "parallel"), VMEM_LIMIT),
        name="attn_a",
    )(sink, qa, ka, va)


def _attn_b_kernel(q_ref, k_ref, v_ref, out_ref, bias_ref, tmp_ref, qs_ref, ks_ref, vs_ref, os_ref, ls_ref,
                   *bufs, seq):
    (w1, d1), (w4, d4), (w16, d16) = DILATED_PATTERNS
    hw = w1 // (2 * d1)
    qb = ATT_QB
    _band_bias(bias_ref, qb, qb + 2 * hw, hw)
    n4, n16 = seq // d4, seq // d16
    step = d16 // d4
    chunk = 512
    for src, dst in ((q_ref, qs_ref), (k_ref, ks_ref), (v_ref, vs_ref)):
        for c in range(seq // chunk):
            dst[c * chunk:(c + 1) * chunk, :] = src[c * chunk:(c + 1) * chunk, :].astype(dst.dtype)
        for r in range(d4):
            tmp_ref[r * n4:(r + 1) * n4, :] = src[pl.ds(r, n4, stride=d4), :]
        for c in range(seq // chunk):
            dst[seq + c * chunk:seq + (c + 1) * chunk, :] = tmp_ref[c * chunk:(c + 1) * chunk, :].astype(dst.dtype)
        for r in range(d16):
            dst[2 * seq + r * n16:2 * seq + (r + 1) * n16, :] = tmp_ref[
                pl.ds((r % d4) * n4 + r // d4, n16, stride=step), :].astype(dst.dtype)

    bps = seq // qb
    lg = [(seq // d // qb).bit_length() - 1 for (_, d) in DILATED_PATTERNS]

    def geometry(block):
        pat, within = block // bps, block % bps
        sh = jnp.where(pat == 0, lg[0], jnp.where(pat == 1, lg[1], lg[2]))
        seg, jb = within >> sh, within & ((1 << sh) - 1)
        n = qb << sh
        qs = jb * qb
        ks = jnp.clip(qs - hw, 0, n - (qb + 2 * hw))
        kind = jnp.where(jb == 0, 0, jnp.where(jb == (1 << sh) - 1, 2, 1))
        base = pat * seq + seg * n
        return base + qs, base + ks, kind

    _band_pipeline(qs_ref, ks_ref, vs_ref, bias_ref, os_ref, ls_ref, bufs,
                   nblocks=len(DILATED_PATTERNS) * bps, geometry=geometry, stage_vt=True)

    def merge(r, carry):
        rows16 = pl.ds(pl.multiple_of(2 * seq + r * n16, SUBLANES), n16)
        rows4 = pl.ds(seq + (r % d4) * n4 + r // d4, n16, stride=step)
        rows1 = pl.ds(r, n16, stride=d16)
        la, lb, lc = ls_ref[rows1, :], ls_ref[rows4, :], ls_ref[rows16, :]
        top = jnp.maximum(jnp.maximum(la, lb), lc)
        wa, wb, wc = jnp.exp2(la - top), jnp.exp2(lb - top), jnp.exp2(lc - top)
        num = wa * os_ref[rows1, :] + wb * os_ref[rows4, :] + wc * os_ref[rows16, :]
        out_ref[rows1, :] = num * (1.0 / (wa + wb + wc))
        return carry

    lax.fori_loop(0, d16, merge, 0)


def _attn_b(qb_, kb, vb, batch, seq):
    (w1, d1), (w4, d4), (w16, d16) = DILATED_PATTERNS
    hw = w1 // (2 * d1)
    qb = ATT_QB
    assert d1 == 1 and d16 % d4 == 0 and w4 // (2 * d4) == hw and w16 // (2 * d16) == hw
    assert seq % 512 == 0 and seq // d16 >= qb + 2 * hw
    for (_, d) in DILATED_PATTERNS:
        nblk = seq // d // qb
        assert nblk * d * qb == seq and nblk & (nblk - 1) == 0
    spec = pl.BlockSpec((None, seq, LANES), lambda b, p: (p, b, 0))
    npat = len(DILATED_PATTERNS)
    return pl.pallas_call(
        functools.partial(_attn_b_kernel, seq=seq),
        grid=(batch, B_W // LANES),
        in_specs=[spec] * 3,
        out_specs=spec,
        out_shape=jax.ShapeDtypeStruct(qb_.shape, F32),
        scratch_shapes=[pltpu.VMEM((3, qb + 2 * hw, qb), F32), pltpu.VMEM((seq, LANES), F32),
                        pltpu.VMEM((npat * seq, LANES), F32), pltpu.VMEM((npat * seq, LANES), BF16),
                        pltpu.VMEM((npat * seq, LANES), BF16), pltpu.VMEM((npat * seq, LANES), F32),
                        pltpu.VMEM((npat * seq, LANES), F32)] + _band_scratch(qb, qb + 2 * hw, ATT_B_SLOTS),
        compiler_params=_cparams(("parallel", "parallel"), VMEM_LIMIT),
        name="attn_b",
    )(qb_, kb, vb)


def _mix_kernel(x_ref, oa_ref, ob_ref, ga_ref, gb_ref, wo_f32_ref, gf_ref, wr_ref,
                x1_ref, h2_ref, aff_ref, wo_ref):
    @pl.when(pl.program_id(0) == 0)
    def _cast_weights():
        wo_ref[...] = wo_f32_ref[...].astype(BF16)

    oa = jnp.concatenate([oa_ref[c] for c in range(oa_ref.shape[0])], axis=1)
    ob = jnp.concatenate([ob_ref[c] for c in range(ob_ref.shape[0])], axis=1)
    mixed = jnp.concatenate([_rms(oa, ga_ref[...]), _rms(ob, gb_ref[...])], axis=1).astype(BF16)
    for n in range(x_ref.shape[1] // MXU_COLS):
        cols = slice(n * MXU_COLS, (n + 1) * MXU_COLS)
        x1_ref[:, cols] = x_ref[:, cols] + jnp.dot(mixed, wo_ref[:, cols], preferred_element_type=F32)
    x1 = x1_ref[...]
    h2 = _rms(x1, gf_ref[...])
    tm = h2.shape[0]
    for c in range(h2.shape[1] // LANES):
        h2_ref[pl.ds(c, tm, stride=SUBLANES), :] = h2[:, c * LANES:(c + 1) * LANES]
    ne = aff_ref.shape[0]
    dn = (((1,), (1,)), ((), ()))
    h_hi = h2.astype(BF16)
    h_mid = (h2 - h_hi.astype(F32)).astype(BF16)
    r = lax.dot_general(wr_ref[...], h_hi, dn, preferred_element_type=F32)
    logits = r[:ne] + r[ne:] + lax.dot_general(wr_ref[:ne, :], h_mid, dn, preferred_element_type=F32)
    e = jnp.exp(logits - jnp.max(logits, axis=0, keepdims=True))
    aff_ref[...] = e / jnp.sum(e, axis=0, keepdims=True)


def _mix(x2, oa, ob, g_out_a, g_out_b, w_out, g_ffn, w_router, batch, seq):
    t, d = x2.shape
    tm = MIX_TM
    assert seq % tm == 0 and d % LANES == 0 and d // LANES == SUBLANES
    ns = seq // tm
    row = lambda i: (i, 0)
    const = lambda i: (0, 0)
    ne = w_router.shape[1]
    w_hi = w_router.T.astype(BF16)
    w_mid = (w_router.T - w_hi.astype(F32)).astype(BF16)
    wr = jnp.concatenate([w_hi, w_mid], axis=0)
    return pl.pallas_call(
        _mix_kernel,
        grid=(t // tm,),
        in_specs=[pl.BlockSpec((tm, d), row), pl.BlockSpec((A_Q // LANES, tm, LANES), lambda i: (0, i, 0)),
                  pl.BlockSpec((B_W // LANES, tm, LANES), lambda i: (0, i, 0)),
                  pl.BlockSpec((1, A_Q), const), pl.BlockSpec((1, B_W), const),
                  pl.BlockSpec(w_out.shape, const), pl.BlockSpec((1, d), const),
                  pl.BlockSpec((2 * ne, d), const)],
        out_specs=[pl.BlockSpec((tm, d), row), pl.BlockSpec((tm * SUBLANES, LANES), row),
                   pl.BlockSpec((None, ne, tm), lambda i: (i // ns, 0, i % ns))],
        out_shape=[jax.ShapeDtypeStruct((t, d), F32),
                   jax.ShapeDtypeStruct((t * SUBLANES, LANES), F32),
                   jax.ShapeDtypeStruct((batch, ne, seq), F32)],
        scratch_shapes=[pltpu.VMEM(w_out.shape, BF16)],
        compiler_params=_cparams(("arbitrary",), VMEM_LIMIT),
        name="mix",
    )(x2, oa, ob, g_out_a.reshape(1, -1), g_out_b.reshape(1, -1), w_out, g_ffn.reshape(1, d), wr)


def _lane_cumsum(mask_f, tri):
    rows, s = mask_f.shape
    carry = jnp.zeros((rows, 1), F32)
    out = []
    for c in range(s // LANES):
        inc = jnp.dot(mask_f[:, c * LANES:(c + 1) * LANES].astype(BF16), tri,
                      preferred_element_type=F32) + carry
        out.append(inc)
        carry = inc[:, LANES - 1:LANES]
    return jnp.concatenate(out, axis=1)


def _route_kernel(aff_ref, idx_e_ref, idx_l_ref, gate_ref, *, cap, seq):
    b = pl.program_id(0)
    aff = aff_ref[...]
    ne = aff.shape[0]

    def search(i, thr):
        lo = 28 - 2 * i
        for digit in (1, 2, 3):
            cand = thr | (jnp.int32(digit) << lo)
            cnt = jnp.sum((aff >= lax.bitcast_convert_type(cand, F32)).astype(F32), axis=1, keepdims=True)
            best = jnp.where(cnt >= cap, cand, thr if digit == 1 else best)
        return best

    thr = lax.bitcast_convert_type(lax.fori_loop(0, 15, search, jnp.zeros((ne, 1), jnp.int32)), F32)
    gt = aff > thr
    eq = aff == thr
    n_gt = jnp.sum(gt.astype(F32), axis=1, keepdims=True)
    ri = lax.broadcasted_iota(jnp.int32, (LANES, LANES), 0)
    ci = lax.broadcasted_iota(jnp.int32, (LANES, LANES), 1)
    tri = (ri <= ci).astype(BF16)
    eq_f = eq.astype(F32)
    eq_rank = _lane_cumsum(eq_f, tri) - eq_f
    sel = gt | (eq & (eq_rank < cap - n_gt))
    sel_f = sel.astype(F32)
    slot = _lane_cumsum(sel_f, tri) - sel_f
    key_t = jnp.where(sel, slot, -1.0).T
    tok = lax.broadcasted_iota(jnp.int32, (SUBLANES, seq), 1)
    piece = lax.broadcasted_iota(jnp.int32, (SUBLANES, seq), 0)
    tok_rows = jnp.where(piece == 0, tok // 64, jnp.where(piece == 1, tok % 64, 0)).astype(F32)
    p_iota = lax.broadcasted_iota(jnp.int32, (1, cap), 1).astype(F32)
    for e in range(ne):
        onehot = (key_t[:, e:e + 1] == p_iota).astype(BF16)
        g = aff[e:e + 1, :]
        g_hi = g.astype(BF16).astype(F32)
        g_mid = (g - g_hi).astype(BF16).astype(F32)
        g_lo = g - g_hi - g_mid
        lhs = jnp.where(piece == 2, g_hi, jnp.where(piece == 3, g_mid, jnp.where(piece == 4, g_lo, tok_rows)))
        res = jnp.dot(lhs.astype(BF16), onehot, preferred_element_type=F32)
        tile_row = (res[0:1] * 64.0 + res[1:2]).astype(jnp.int32) * SUBLANES
        idx_l_ref[e] = tile_row
        idx_e_ref[e] = tile_row + b * (seq * SUBLANES)
        gate_ref[e] = res[2:3] + res[3:4] + res[4:5]


def _route(aff_t, cap):
    batch, ne, seq = aff_t.shape
    assert seq % LANES == 0 and seq <= 64 * 256
    return pl.pallas_call(
        functools.partial(_route_kernel, cap=cap, seq=seq),
        grid=(batch,),
        in_specs=[pl.BlockSpec((None, ne, seq), lambda b: (b, 0, 0))],
        out_specs=[pl.BlockSpec((ne, 1, cap), lambda b: (0, 0, b)), pl.BlockSpec((ne, 1, cap), lambda b: (b, 0, 0)),
                   pl.BlockSpec((ne, 1, cap), lambda b: (b, 0, 0))],
        out_shape=[jax.ShapeDtypeStruct((ne, 1, batch * cap), jnp.int32),
                   jax.ShapeDtypeStruct((batch * ne, 1, cap), jnp.int32),
                   jax.ShapeDtypeStruct((batch * ne, 1, cap), F32)],
        compiler_params=_cparams(("parallel",), VMEM_LIMIT),
        name="route",
    )(aff_t)


def _experts_kernel(idx_ref, idx_next_ref, h_hbm, wg_ref, wu_ref, wd_ref, y_ref, rows_ref, xe_ref, sems,
                    *, ne, nf):
    e, f = pl.program_id(0), pl.program_id(1)
    ntok, d = xe_ref.shape
    slot = e % 2
    chunks = ntok // FFN_TM
    per_chunk = ntok // (nf * chunks)

    def row_copy(idx_smem, i, buf):
        src = pl.multiple_of(idx_smem[0, 0, i], SUBLANES)
        dst = pl.multiple_of(i * SUBLANES, SUBLANES)
        return pltpu.make_async_copy(h_hbm.at[pl.ds(src, SUBLANES), :],
                                     rows_ref.at[buf, pl.ds(dst, SUBLANES), :], sems.at[buf])

    def wait_rows(buf):
        pltpu.make_async_copy(h_hbm.at[pl.ds(0, ntok * SUBLANES), :], rows_ref.at[buf], sems.at[buf]).wait()

    @pl.when((e == 0) & (f == 0))
    def _first_gather():
        def issue(i, carry):
            row_copy(idx_ref, i, 0).start()
            return carry

        lax.fori_loop(0, ntok, issue, 0, unroll=8)

    @pl.when(f == 0)
    def _repack():
        wait_rows(slot)
        for m in range(chunks):
            xe_ref[m * FFN_TM:(m + 1) * FFN_TM, :] = jnp.concatenate(
                [rows_ref[slot, pl.ds(m * FFN_TM * SUBLANES + c, FFN_TM, stride=SUBLANES), :]
                 for c in range(d // LANES)], axis=1).astype(BF16)
        y_ref[...] = jnp.zeros_like(y_ref)

    wg = wg_ref[...].astype(BF16)
    wu = wu_ref[...].astype(BF16)
    wd = wd_ref[...].astype(BF16)
    for m in range(chunks):
        xe = xe_ref[m * FFN_TM:(m + 1) * FFN_TM, :]
        hmid = []
        for n in range(wg.shape[1] // MXU_COLS):
            cols = slice(n * MXU_COLS, (n + 1) * MXU_COLS)
            a = jnp.dot(xe, wg[:, cols], preferred_element_type=F32)
            u = jnp.dot(xe, wu[:, cols], preferred_element_type=F32)
            hmid.append((a * jax.nn.sigmoid(a) * u).astype(BF16))
        hmid = jnp.concatenate(hmid, axis=1)
        y_ref[m * FFN_TM:(m + 1) * FFN_TM, :] += jnp.dot(hmid, wd, preferred_element_type=F32)
        for i in range(per_chunk):
            row_copy(idx_next_ref, (f * chunks + m) * per_chunk + i, 1 - slot).start()

    @pl.when((e == ne - 1) & (f == nf - 1))
    def _drain():
        wait_rows(1 - slot)


def _experts(idx_e, h2_rows, w_gate, w_up, w_down):
    ne, d, ff = w_gate.shape
    ntok = idx_e.shape[-1]
    tf = FFN_TF
    nf = ff // tf
    assert ff % tf == 0 and ntok % FFN_TM == 0 and ntok % (nf * (ntok // FFN_TM)) == 0 and d // LANES == SUBLANES
    return pl.pallas_call(
        functools.partial(_experts_kernel, ne=ne, nf=nf),
        grid=(ne, nf),
        in_specs=[pl.BlockSpec((1, 1, ntok), lambda e, f: (e, 0, 0), memory_space=pltpu.SMEM),
                  pl.BlockSpec((1, 1, ntok), lambda e, f: (jnp.minimum(e + 1, ne - 1), 0, 0),
                               memory_space=pltpu.SMEM),
                  pl.BlockSpec(memory_space=pl.ANY),
                  pl.BlockSpec((None, d, tf), lambda e, f: (e, 0, f)),
                  pl.BlockSpec((None, d, tf), lambda e, f: (e, 0, f)),
                  pl.BlockSpec((None, tf, d), lambda e, f: (e, f, 0))],
        out_specs=pl.BlockSpec((None, ntok, d), lambda e, f: (e, 0, 0)),
        out_shape=jax.ShapeDtypeStruct((ne, ntok, d), F32),
        scratch_shapes=[pltpu.VMEM((2, ntok * SUBLANES, LANES), F32), pltpu.VMEM((ntok, d), BF16),
                        pltpu.SemaphoreType.DMA((2,))],
        compiler_params=_cparams(("arbitrary", "arbitrary"), VMEM_LIMIT),
        name="experts",
    )(idx_e, idx_e, h2_rows, w_gate, w_up, w_down)


def _combine_kernel(*refs, batch, per_step):
    idx_refs, gate_refs, y_refs = (refs[i * per_step:(i + 1) * per_step] for i in range(3))
    x1_ref, g_ref, o_ref, acc_ref, ybuf_ref = refs[3 * per_step:]
    b, j = pl.program_id(0), pl.program_id(1)
    cap, d = y_refs[0].shape
    tm = x1_ref.shape[0]
    cur = b % 2

    @pl.when(b < batch)
    def _scatter():
        @pl.when(j == 0)
        def _init():
            acc_ref[cur] = jnp.zeros(acc_ref.shape[1:], F32)

        for idx_ref, gate_ref, y_ref in zip(idx_refs, gate_refs, y_refs):
            gate_col = jnp.broadcast_to(gate_ref[0], (LANES, cap)).T
            for c in range(d // LANES):
                ybuf_ref[pl.ds(c, cap, stride=SUBLANES), :] = y_ref[:, c * LANES:(c + 1) * LANES] * gate_col

            def group(k, carry, idx_ref=idx_ref):
                rows, sums = [], []
                for u in range(RMW_UNROLL):
                    i = k * RMW_UNROLL + u
                    t = pl.ds(pl.multiple_of(idx_ref[0, 0, i], SUBLANES), SUBLANES)
                    yi = ybuf_ref[pl.ds(pl.multiple_of(i * SUBLANES, SUBLANES), SUBLANES), :]
                    rows.append(t)
                    sums.append(acc_ref[cur, t, :] + yi)
                for t, s in zip(rows, sums):
                    acc_ref[cur, t, :] = s
                return carry

            lax.fori_loop(0, cap // RMW_UNROLL, group, 0)

    @pl.when(b > 0)
    def _emit():
        base = pl.multiple_of(j * tm * SUBLANES, SUBLANES)
        moe = jnp.concatenate([acc_ref[1 - cur, pl.ds(base + c, tm, stride=SUBLANES), :]
                               for c in range(d // LANES)], axis=1)
        o_ref[...] = _rms(x1_ref[...] + moe, g_ref[...])


def _combine(idx_l, gates, y, x1, g_final, batch, seq):
    ne = y.shape[0]
    t, d = x1.shape
    cap = idx_l.shape[-1]
    per_step = COMBINE_EXPERTS
    steps = ne // per_step
    tm = seq // steps
    assert cap % RMW_UNROLL == 0 and d // LANES == SUBLANES and tm * steps == seq and ne % per_step == 0
    scattered = lambda b: jnp.minimum(b, batch - 1)
    rows = lambda b, j: (jnp.maximum(b - 1, 0) * steps + jnp.where(b > 0, j, 0), 0)
    slot_specs = [pl.BlockSpec((1, 1, cap), functools.partial(
        lambda b, j, x: (scattered(b) * ne + j * per_step + x, 0, 0), x=x), memory_space=ms)
        for ms in (pltpu.SMEM, None) for x in range(per_step)]
    y_specs = [pl.BlockSpec((None, cap, d), functools.partial(
        lambda b, j, x: (j * per_step + x, scattered(b), 0), x=x)) for x in range(per_step)]
    return pl.pallas_call(
        functools.partial(_combine_kernel, batch=batch, per_step=per_step),
        grid=(batch + 1, steps),
        in_specs=slot_specs + y_specs + [pl.BlockSpec((tm, d), rows), pl.BlockSpec((1, d), lambda b, j: (0, 0))],
        out_specs=pl.BlockSpec((tm, d), rows),
        out_shape=jax.ShapeDtypeStruct((t, d), F32),
        scratch_shapes=[pltpu.VMEM((2, seq * SUBLANES, LANES), F32), pltpu.VMEM((cap * SUBLANES, LANES), F32)],
        compiler_params=_cparams(("arbitrary", "arbitrary"), VMEM_LIMIT),
        name="combine",
    )(*([idx_l] * per_step + [gates] * per_step + [y] * per_step), x1, g_final.reshape(1, d))


def kernel(x, g_mix, w_in, a_sink, g_out_a, g_out_b, w_out, g_ffn, w_router, w_gate, w_up, w_down, g_final):
    batch, seq, d = x.shape
    depth = w_in.shape[0]
    ne = w_router.shape[-1]
    cap = CAPACITY_FACTOR * seq // ne
    assert depth == 1, "the last kernel fuses the MoE residual with the output norm of a single layer"
    l = 0
    x2 = x.reshape(batch * seq, d)
    qa, ka, va, qb, kb, vb = _proj(x2, g_mix[l], w_in[l], seq)
    oa = _attn_a(qa, ka, va, a_sink[l], batch, seq)
    ob = _attn_b(qb, kb, vb, batch, seq)
    x1, h2_rows, aff_t = _mix(x2, oa, ob, g_out_a[l], g_out_b[l], w_out[l], g_ffn[l], w_router[l], batch, seq)
    idx_e, idx_l, gates = _route(aff_t, cap)
    y = _experts(idx_e, h2_rows, w_gate[l], w_up[l], w_down[l])
    out = _combine(idx_l, gates, y, x1, g_final, batch, seq)
    return out.reshape(batch, seq, d)
```

```python
import functools

import jax
import jax.numpy as jnp
from jax import lax
from jax.experimental import pallas as pl
from jax.experimental.pallas import tpu as pltpu

F32 = jnp.float32
BF16 = jnp.bfloat16

LANES = 128
MXU_COLS = 256
SUBLANES = 8
BF16_ROWS = 16
VMEM_LIMIT = 56 * 1024 * 1024

HEAD_DIM = 64
A_HEADS = 8
A_KV_HEADS = 2
A_GROUP = A_HEADS // A_KV_HEADS
A_HALF_WINDOW = 128
B_HEADS = 8
DILATED_PATTERNS = ((128, 1), (512, 4), (2048, 16))
A_Q = A_HEADS * HEAD_DIM
A_KV = A_KV_HEADS * HEAD_DIM
B_W = B_HEADS * HEAD_DIM
ROPE_THETA = 500000.0
ROT_DIM = HEAD_DIM // 4
CAPACITY_FACTOR = 2
NORM_EPS = 1e-6
NEG_INF = -1e30
LOG2_E = 1.4426950408889634

PROJ_TM = 1024
PROJ_ROWS = 512
ATT_QB = 128
ATT_A_SLOTS = 2
ATT_B_SLOTS = 4
MIX_TM = 1024
FFN_TF = 512
FFN_TM = 1024
RMW_UNROLL = 8
COMBINE_EXPERTS = 2


def _cparams(sem, vmem=None):
    return pltpu.CompilerParams(dimension_semantics=sem, vmem_limit_bytes=vmem)


def _rope_tables(seq_len):
    half = ROT_DIM // 2
    inv_freq = ROPE_THETA ** (-jnp.arange(0, ROT_DIM, 2, dtype=F32) / ROT_DIM)
    ang = jnp.arange(seq_len, dtype=F32)[:, None] * inv_freq[None, :]
    cos = jnp.tile(jnp.cos(ang), (1, LANES // half))
    sin = jnp.tile(jnp.sin(ang), (1, LANES // half))
    j = jnp.arange(LANES) % HEAD_DIM
    cos_t = jnp.where(j < ROT_DIM, cos, 1.0)
    sin_a = jnp.where(j < half, -sin, 0.0)
    sin_b = jnp.where((j >= half) & (j < ROT_DIM), sin, 0.0)
    return cos_t, sin_a, sin_b


def _rms(x, g):
    return x * lax.rsqrt(jnp.mean(x * x, axis=-1, keepdims=True) + NORM_EPS) * g


def _proj_kernel(x_ref, g_ref, w_ref, cos_ref, sa_ref, sb_ref,
                 qa_ref, ka_ref, va_ref, qb_ref, kb_ref, vb_ref, wbf_ref):
    @pl.when(pl.program_id(0) == 0)
    def _cast_weights():
        wbf_ref[...] = w_ref[...].astype(BF16)

    half = ROT_DIM // 2
    scale = HEAD_DIM ** -0.5 * LOG2_E
    for r0 in range(0, x_ref.shape[0], PROJ_ROWS):
        rows = slice(r0, r0 + PROJ_ROWS)
        h = _rms(x_ref[rows, :], g_ref[...]).astype(BF16)
        cos, sa, sb = cos_ref[rows, :], sa_ref[rows, :], sb_ref[rows, :]

        def rope(t):
            return t * cos + pltpu.roll(t, LANES - half, 1) * sa + pltpu.roll(t, half, 1) * sb

        def blocks(col0, width):
            for n in range(width // MXU_COLS):
                c0 = col0 + n * MXU_COLS
                yield 2 * n, jnp.dot(h, wbf_ref[:, c0:c0 + MXU_COLS], preferred_element_type=F32)

        for c, p in blocks(0, A_Q):
            for i in range(2):
                qa_ref[c + i, rows, :] = (rope(p[:, i * LANES:(i + 1) * LANES]) * scale).astype(qa_ref.dtype)
        (_, kv), = blocks(A_Q, 2 * A_KV)
        first = lax.broadcasted_iota(jnp.int32, (PROJ_ROWS, LANES), 1) < HEAD_DIM
        for ref, t in ((ka_ref, rope(kv[:, :A_KV])), (va_ref, kv[:, A_KV:])):
            swapped = pltpu.roll(t, HEAD_DIM, 1)
            ref[0, rows, :] = jnp.where(first, t, swapped).astype(ref.dtype)
            ref[1, rows, :] = jnp.where(first, swapped, t).astype(ref.dtype)
        for c, p in blocks(A_Q + 2 * A_KV, B_W):
            for i in range(2):
                qb_ref[c + i, rows, :] = rope(p[:, i * LANES:(i + 1) * LANES]) * scale
        for c, p in blocks(A_Q + 2 * A_KV + B_W, B_W):
            for i in range(2):
                kb_ref[c + i, rows, :] = rope(p[:, i * LANES:(i + 1) * LANES])
        for c, p in blocks(A_Q + 2 * A_KV + 2 * B_W, B_W):
            for i in range(2):
                vb_ref[c + i, rows, :] = p[:, i * LANES:(i + 1) * LANES]


def _proj(x2, g_mix, w_in, seq):
    t, d = x2.shape
    tm = PROJ_TM
    assert t % tm == 0 and seq % tm == 0 and A_KV == LANES and A_KV_HEADS == 2
    nseq = seq // tm
    tables = _rope_tables(seq)
    slab = lambda i: (0, i, 0)
    const = lambda i: (0, 0)
    tab = lambda i: (i % nseq, 0)
    widths = (A_Q, A_KV_HEADS * LANES, A_KV_HEADS * LANES, B_W, B_W, B_W)
    return pl.pallas_call(
        _proj_kernel,
        grid=(t // tm,),
        in_specs=[pl.BlockSpec((tm, d), lambda i: (i, 0)), pl.BlockSpec((1, d), const),
                  pl.BlockSpec(w_in.shape, const)] + [pl.BlockSpec((tm, LANES), tab)] * 3,
        out_specs=[pl.BlockSpec((w // LANES, tm, LANES), slab) for w in widths],
        out_shape=[jax.ShapeDtypeStruct((w // LANES, t, LANES), dt)
                   for w, dt in zip(widths, (BF16, BF16, BF16, F32, F32, F32))],
        scratch_shapes=[pltpu.VMEM(w_in.shape, BF16)],
        compiler_params=_cparams(("arbitrary",), VMEM_LIMIT),
        name="proj",
    )(x2, g_mix.reshape(1, d), w_in, *tables)


def _band_bias(bias_ref, qb, kw, hw):
    kk = lax.broadcasted_iota(jnp.int32, (kw, qb), 0)
    qq = lax.broadcasted_iota(jnp.int32, (kw, qb), 1)
    for t, delta in enumerate((0, -hw, -2 * hw)):
        bias_ref[t] = jnp.where(jnp.abs(kk + delta - qq) <= hw, 0.0, NEG_INF)


def _band_scratch(qb, kw, slots):
    n = 2 * slots
    return ([pltpu.VMEM((kw, 2 * qb), F32)] * n + [pltpu.VMEM((kw, 2 * qb), BF16)] * n
            + [pltpu.VMEM((SUBLANES, 2 * qb), F32)] * n + [pltpu.VMEM((LANES, kw), BF16)] * n
            + [pltpu.VMEM((SUBLANES, 2 * qb), F32)] * n)


def _band_pipeline(q_ref, k_ref, v_ref, bias_ref, o_ref, lse_ref, bufs, *, nblocks, geometry, stage_vt,
                   sink_row=None):
    qb = ATT_QB
    n = len(bufs) // 5
    slots = n // 2
    s_bufs, p_bufs, ml_bufs, vt_bufs, mx_bufs = (bufs[i * n:(i + 1) * n] for i in range(5))
    kw = s_bufs[0].shape[0]
    for s_buf, p_buf, ml_buf, vt_buf, mx_buf in zip(s_bufs, p_bufs, ml_bufs, vt_bufs, mx_bufs):
        mx_buf[...] = jnp.zeros_like(mx_buf)
        s_buf[...] = jnp.zeros_like(s_buf)
        p_buf[...] = jnp.zeros_like(p_buf)
        ml_buf[...] = jnp.ones_like(ml_buf)
        vt_buf[...] = jnp.zeros_like(vt_buf)

    def rows(block):
        q0, k0, kind = geometry(jnp.clip(block, 0, nblocks - 1))
        return pl.ds(pl.multiple_of(q0, BF16_ROWS), qb), pl.ds(pl.multiple_of(k0, BF16_ROWS), kw), kind

    def scores(block, s_buf, mx_buf):
        q_rows, k_rows, kind = rows(block)
        q = q_ref[q_rows, :]
        lo = lax.broadcasted_iota(jnp.int32, q.shape, 1) < HEAD_DIM
        q2 = jnp.concatenate([jnp.where(lo, q, 0.0), jnp.where(lo, 0.0, q)], axis=0).astype(BF16)
        s = lax.dot_general(k_ref[k_rows, :].astype(BF16), q2, (((1,), (1,)), ((), ())),
                            preferred_element_type=F32)
        bias = bias_ref[kind]
        for hh in range(2):
            s_h = s[:, hh * qb:(hh + 1) * qb] + bias
            s_buf[:, hh * qb:(hh + 1) * qb] = s_h
            mx_buf[0:1, hh * qb:(hh + 1) * qb] = jnp.max(s_h, axis=0, keepdims=True)

    def softmax(block, s_buf, mx_buf, p_buf, ml_buf, vt_buf):
        if stage_vt:
            _, k_rows, _ = rows(block)
            vt_buf[...] = v_ref[k_rows, :].astype(F32).T.astype(BF16)
        m = mx_buf[0:1, :]
        if sink_row is not None:
            m = jnp.maximum(m, sink_row)
        p = jnp.exp2(s_buf[...] - m)
        l = jnp.sum(p, axis=0, keepdims=True)
        if sink_row is not None:
            l = l + jnp.exp2(sink_row - m)
        p_buf[...] = p.astype(BF16)
        ml_buf[0:1, :] = m
        ml_buf[1:2, :] = l

    def output(block, p_buf, ml_buf, vt_buf):
        q_rows, k_rows, _ = rows(block)
        if stage_vt:
            o_t = jnp.dot(vt_buf[...], p_buf[...], preferred_element_type=F32)
        else:
            o_t = lax.dot_general(v_ref[k_rows, :].astype(BF16), p_buf[...], (((0,), (0,)), ((), ())),
                                  preferred_element_type=F32)
        o_tc = jnp.concatenate([o_t[:HEAD_DIM, :qb], o_t[HEAD_DIM:, qb:]], axis=0)
        top = lax.broadcasted_iota(jnp.int32, (LANES, qb), 0) < HEAD_DIM
        m, l = ml_buf[0:1, :], ml_buf[1:2, :]
        inv = 1.0 / l
        o_ref[q_rows, :] = (o_tc * jnp.where(top, inv[:, :qb], inv[:, qb:])).T
        if lse_ref is not None:
            lse = m + jnp.log2(l)
            lse_ref[q_rows, :] = jnp.where(top, lse[:, :qb], lse[:, qb:]).T

    def trip_pair(jj, carry):
        for parity in range(2):
            for slot in range(slots):
                t = (2 * jj + parity) * slots + slot
                rd, wr = (1 - parity) * slots + slot, parity * slots + slot
                output(t - 2 * slots, p_bufs[rd], ml_bufs[rd], vt_bufs[rd])
                softmax(t - slots, s_bufs[rd], mx_bufs[rd], p_bufs[wr], ml_bufs[wr], vt_bufs[wr])
                scores(t, s_bufs[wr], mx_bufs[wr])
        return carry

    trips = nblocks // slots + 2
    assert nblocks % slots == 0 and trips % 2 == 0
    lax.fori_loop(0, trips // 2, trip_pair, 0)


def _segment_geometry(block, n, hw, base=0):
    qb = ATT_QB
    nblk = n // qb
    seg, jb = block // nblk, block % nblk
    qs = jb * qb
    ks = jnp.clip(qs - hw, 0, n - (qb + 2 * hw))
    kind = jnp.where(jb == 0, 0, jnp.where(jb == nblk - 1, 2, 1))
    return base + seg * n + qs, base + seg * n + ks, kind


def _attn_a_kernel(sink_ref, q_ref, k_ref, v_ref, o_ref, bias_ref, *bufs, seq):
    j = pl.program_id(1)
    qb, hw = ATT_QB, A_HALF_WINDOW
    _band_bias(bias_ref, qb, qb + 2 * hw, hw)
    col = lax.broadcasted_iota(jnp.int32, (1, 2 * qb), 1)
    sink_row = jnp.where(col < qb, sink_ref[2 * j], sink_ref[2 * j + 1]) * LOG2_E
    _band_pipeline(q_ref, k_ref, v_ref, bias_ref, o_ref, None, bufs, nblocks=seq // qb,
                   geometry=functools.partial(_segment_geometry, n=seq, hw=hw), stage_vt=False,
                   sink_row=sink_row)


def _attn_a(qa, ka, va, sink, batch, seq):
    qb, hw = ATT_QB, A_HALF_WINDOW
    assert seq % qb == 0 and seq >= qb + 2 * hw and A_KV == LANES
    qspec = pl.BlockSpec((None, seq, LANES), lambda b, j: (j, b, 0))
    per_kv = A_GROUP * HEAD_DIM // LANES
    kvspec = pl.BlockSpec((None, seq, LANES), lambda b, j: (j // per_kv, b, 0))
    return pl.pallas_call(
        functools.partial(_attn_a_kernel, seq=seq),
        grid=(batch, A_Q // LANES),
        in_specs=[pl.BlockSpec(memory_space=pltpu.SMEM), qspec, kvspec, kvspec],
        out_specs=qspec,
        out_shape=jax.ShapeDtypeStruct(qa.shape, F32),
        scratch_shapes=[pltpu.VMEM((3, qb + 2 * hw, qb), F32)] + _band_scratch(qb, qb + 2 * hw, ATT_A_SLOTS),
        compiler_params=_cparams(("parallel", "parallel"), VMEM_LIMIT),
        name="attn_a",
    )(sink, qa, ka, va)


def _attn_b_kernel(q_ref, k_ref, v_ref, out_ref, bias_ref, tmp_ref, qs_ref, ks_ref, vs_ref, os_ref, ls_ref,
                   *bufs, seq):
    (w1, d1), (w4, d4), (w16, d16) = DILATED_PATTERNS
    hw = w1 // (2 * d1)
    qb = ATT_QB
    _band_bias(bias_ref, qb, qb + 2 * hw, hw)
    n4, n16 = seq // d4, seq // d16
    step = d16 // d4
    chunk = 512
    for src, dst in ((q_ref, qs_ref), (k_ref, ks_ref), (v_ref, vs_ref)):
        for c in range(seq // chunk):
            dst[c * chunk:(c + 1) * chunk, :] = src[c * chunk:(c + 1) * chunk, :].astype(dst.dtype)
        for r in range(d4):
            tmp_ref[r * n4:(r + 1) * n4, :] = src[pl.ds(r, n4, stride=d4), :]
        for c in range(seq // chunk):
            dst[seq + c * chunk:seq + (c + 1) * chunk, :] = tmp_ref[c * chunk:(c + 1) * chunk, :].astype(dst.dtype)
        for r in range(d16):
            dst[2 * seq + r * n16:2 * seq + (r + 1) * n16, :] = tmp_ref[
                pl.ds((r % d4) * n4 + r // d4, n16, stride=step), :].astype(dst.dtype)

    bps = seq // qb
    lg = [(seq // d // qb).bit_length() - 1 for (_, d) in DILATED_PATTERNS]

    def geometry(block):
        pat, within = block // bps, block % bps
        sh = jnp.where(pat == 0, lg[0], jnp.where(pat == 1, lg[1], lg[2]))
        seg, jb = within >> sh, within & ((1 << sh) - 1)
        n = qb << sh
        qs = jb * qb
        ks = jnp.clip(qs - hw, 0, n - (qb + 2 * hw))
        kind = jnp.where(jb == 0, 0, jnp.where(jb == (1 << sh) - 1, 2, 1))
        base = pat * seq + seg * n
        return base + qs, base + ks, kind

    _band_pipeline(qs_ref, ks_ref, vs_ref, bias_ref, os_ref, ls_ref, bufs,
                   nblocks=len(DILATED_PATTERNS) * bps, geometry=geometry, stage_vt=True)

    def merge(r, carry):
        rows16 = pl.ds(pl.multiple_of(2 * seq + r * n16, SUBLANES), n16)
        rows4 = pl.ds(seq + (r % d4) * n4 + r // d4, n16, stride=step)
        rows1 = pl.ds(r, n16, stride=d16)
        la, lb, lc = ls_ref[rows1, :], ls_ref[rows4, :], ls_ref[rows16, :]
        top = jnp.maximum(jnp.maximum(la, lb), lc)
        wa, wb, wc = jnp.exp2(la - top), jnp.exp2(lb - top), jnp.exp2(lc - top)
        num = wa * os_ref[rows1, :] + wb * os_ref[rows4, :] + wc * os_ref[rows16, :]
        out_ref[rows1, :] = num * (1.0 / (wa + wb + wc))
        return carry

    lax.fori_loop(0, d16, merge, 0)


def _attn_b(qb_, kb, vb, batch, seq):
    (w1, d1), (w4, d4), (w16, d16) = DILATED_PATTERNS
    hw = w1 // (2 * d1)
    qb = ATT_QB
    assert d1 == 1 and d16 % d4 == 0 and w4 // (2 * d4) == hw and w16 // (2 * d16) == hw
    assert seq % 512 == 0 and seq // d16 >= qb + 2 * hw
    for (_, d) in DILATED_PATTERNS:
        nblk = seq // d // qb
        assert nblk * d * qb == seq and nblk & (nblk - 1) == 0
    spec = pl.BlockSpec((None, seq, LANES), lambda b, p: (p, b, 0))
    npat = len(DILATED_PATTERNS)
    return pl.pallas_call(
        functools.partial(_attn_b_kernel, seq=seq),
        grid=(batch, B_W // LANES),
        in_specs=[spec] * 3,
        out_specs=spec,
        out_shape=jax.ShapeDtypeStruct(qb_.shape, F32),
        scratch_shapes=[pltpu.VMEM((3, qb + 2 * hw, qb), F32), pltpu.VMEM((seq, LANES), F32),
                        pltpu.VMEM((npat * seq, LANES), F32), pltpu.VMEM((npat * seq, LANES), BF16),
                        pltpu.VMEM((npat * seq, LANES), BF16), pltpu.VMEM((npat * seq, LANES), F32),
                        pltpu.VMEM((npat * seq, LANES), F32)] + _band_scratch(qb, qb + 2 * hw, ATT_B_SLOTS),
        compiler_params=_cparams(("parallel", "parallel"), VMEM_LIMIT),
        name="attn_b",
    )(qb_, kb, vb)


def _mix_kernel(x_ref, oa_ref, ob_ref, ga_ref, gb_ref, wo_f32_ref, gf_ref, wr_ref,
                x1_ref, h2_ref, aff_ref, wo_ref):
    @pl.when(pl.program_id(0) == 0)
    def _cast_weights():
        wo_ref[...] = wo_f32_ref[...].astype(BF16)

    oa = jnp.concatenate([oa_ref[c] for c in range(oa_ref.shape[0])], axis=1)
    ob = jnp.concatenate([ob_ref[c] for c in range(ob_ref.shape[0])], axis=1)
    mixed = jnp.concatenate([_rms(oa, ga_ref[...]), _rms(ob, gb_ref[...])], axis=1).astype(BF16)
    for n in range(x_ref.shape[1] // MXU_COLS):
        cols = slice(n * MXU_COLS, (n + 1) * MXU_COLS)
        x1_ref[:, cols] = x_ref[:, cols] + jnp.dot(mixed, wo_ref[:, cols], preferred_element_type=F32)
    x1 = x1_ref[...]
    h2 = _rms(x1, gf_ref[...])
    tm = h2.shape[0]
    for c in range(h2.shape[1] // LANES):
        h2_ref[pl.ds(c, tm, stride=SUBLANES), :] = h2[:, c * LANES:(c + 1) * LANES]
    ne = aff_ref.shape[0]
    dn = (((1,), (1,)), ((), ()))
    h_hi = h2.astype(BF16)
    h_mid = (h2 - h_hi.astype(F32)).astype(BF16)
    r = lax.dot_general(wr_ref[...], h_hi, dn, preferred_element_type=F32)
    logits = r[:ne] + r[ne:] + lax.dot_general(wr_ref[:ne, :], h_mid, dn, preferred_element_type=F32)
    e = jnp.exp(logits - jnp.max(logits, axis=0, keepdims=True))
    aff_ref[...] = e / jnp.sum(e, axis=0, keepdims=True)


def _mix(x2, oa, ob, g_out_a, g_out_b, w_out, g_ffn, w_router, batch, seq):
    t, d = x2.shape
    tm = MIX_TM
    assert seq % tm == 0 and d % LANES == 0 and d // LANES == SUBLANES
    ns = seq // tm
    row = lambda i: (i, 0)
    const = lambda i: (0, 0)
    ne = w_router.shape[1]
    w_hi = w_router.T.astype(BF16)
    w_mid = (w_router.T - w_hi.astype(F32)).astype(BF16)
    wr = jnp.concatenate([w_hi, w_mid], axis=0)
    return pl.pallas_call(
        _mix_kernel,
        grid=(t // tm,),
        in_specs=[pl.BlockSpec((tm, d), row), pl.BlockSpec((A_Q // LANES, tm, LANES), lambda i: (0, i, 0)),
                  pl.BlockSpec((B_W // LANES, tm, LANES), lambda i: (0, i, 0)),
                  pl.BlockSpec((1, A_Q), const), pl.BlockSpec((1, B_W), const),
                  pl.BlockSpec(w_out.shape, const), pl.BlockSpec((1, d), const),
                  pl.BlockSpec((2 * ne, d), const)],
        out_specs=[pl.BlockSpec((tm, d), row), pl.BlockSpec((tm * SUBLANES, LANES), row),
                   pl.BlockSpec((None, ne, tm), lambda i: (i // ns, 0, i % ns))],
        out_shape=[jax.ShapeDtypeStruct((t, d), F32),
                   jax.ShapeDtypeStruct((t * SUBLANES, LANES), F32),
                   jax.ShapeDtypeStruct((batch, ne, seq), F32)],
        scratch_shapes=[pltpu.VMEM(w_out.shape, BF16)],
        compiler_params=_cparams(("arbitrary",), VMEM_LIMIT),
        name="mix",
    )(x2, oa, ob, g_out_a.reshape(1, -1), g_out_b.reshape(1, -1), w_out, g_ffn.reshape(1, d), wr)


def _lane_cumsum(mask_f, tri):
    rows, s = mask_f.shape
    carry = jnp.zeros((rows, 1), F32)
    out = []
    for c in range(s // LANES):
        inc = jnp.dot(mask_f[:, c * LANES:(c + 1) * LANES].astype(BF16), tri,
                      preferred_element_type=F32) + carry
        out.append(inc)
        carry = inc[:, LANES - 1:LANES]
    return jnp.concatenate(out, axis=1)


def _route_kernel(aff_ref, idx_e_ref, idx_l_ref, gate_ref, *, cap, seq):
    b = pl.program_id(0)
    aff = aff_ref[...]
    ne = aff.shape[0]

    def search(i, thr):
        lo = 28 - 2 * i
        for digit in (1, 2, 3):
            cand = thr | (jnp.int32(digit) << lo)
            cnt = jnp.sum((aff >= lax.bitcast_convert_type(cand, F32)).astype(F32), axis=1, keepdims=True)
            best = jnp.where(cnt >= cap, cand, thr if digit == 1 else best)
        return best

    thr = lax.bitcast_convert_type(lax.fori_loop(0, 15, search, jnp.zeros((ne, 1), jnp.int32)), F32)
    gt = aff > thr
    eq = aff == thr
    n_gt = jnp.sum(gt.astype(F32), axis=1, keepdims=True)
    ri = lax.broadcasted_iota(jnp.int32, (LANES, LANES), 0)
    ci = lax.broadcasted_iota(jnp.int32, (LANES, LANES), 1)
    tri = (ri <= ci).astype(BF16)
    eq_f = eq.astype(F32)
    eq_rank = _lane_cumsum(eq_f, tri) - eq_f
    sel = gt | (eq & (eq_rank < cap - n_gt))
    sel_f = sel.astype(F32)
    slot = _lane_cumsum(sel_f, tri) - sel_f
    key_t = jnp.where(sel, slot, -1.0).T
    tok = lax.broadcasted_iota(jnp.int32, (SUBLANES, seq), 1)
    piece = lax.broadcasted_iota(jnp.int32, (SUBLANES, seq), 0)
    tok_rows = jnp.where(piece == 0, tok // 64, jnp.where(piece == 1, tok % 64, 0)).astype(F32)
    p_iota = lax.broadcasted_iota(jnp.int32, (1, cap), 1).astype(F32)
    for e in range(ne):
        onehot = (key_t[:, e:e + 1] == p_iota).astype(BF16)
        g = aff[e:e + 1, :]
        g_hi = g.astype(BF16).astype(F32)
        g_mid = (g - g_hi).astype(BF16).astype(F32)
        g_lo = g - g_hi - g_mid
        lhs = jnp.where(piece == 2, g_hi, jnp.where(piece == 3, g_mid, jnp.where(piece == 4, g_lo, tok_rows)))
        res = jnp.dot(lhs.astype(BF16), onehot, preferred_element_type=F32)
        tile_row = (res[0:1] * 64.0 + res[1:2]).astype(jnp.int32) * SUBLANES
        idx_l_ref[e] = tile_row
        idx_e_ref[e] = tile_row + b * (seq * SUBLANES)
        gate_ref[e] = res[2:3] + res[3:4] + res[4:5]


def _route(aff_t, cap):
    batch, ne, seq = aff_t.shape
    assert seq % LANES == 0 and seq <= 64 * 256
    return pl.pallas_call(
        functools.partial(_route_kernel, cap=cap, seq=seq),
        grid=(batch,),
        in_specs=[pl.BlockSpec((None, ne, seq), lambda b: (b, 0, 0))],
        out_specs=[pl.BlockSpec((ne, 1, cap), lambda b: (0, 0, b)), pl.BlockSpec((ne, 1, cap), lambda b: (b, 0, 0)),
                   pl.BlockSpec((ne, 1, cap), lambda b: (b, 0, 0))],
        out_shape=[jax.ShapeDtypeStruct((ne, 1, batch * cap), jnp.int32),
                   jax.ShapeDtypeStruct((batch * ne, 1, cap), jnp.int32),
                   jax.ShapeDtypeStruct((batch * ne, 1, cap), F32)],
        compiler_params=_cparams(("parallel",), VMEM_LIMIT),
        name="route",
    )(aff_t)


def _experts_kernel(idx_ref, idx_next_ref, h_hbm, wg_ref, wu_ref, wd_ref, y_ref, rows_ref, xe_ref, sems,
                    *, ne, nf):
    e, f = pl.program_id(0), pl.program_id(1)
    ntok, d = xe_ref.shape
    slot = e % 2
    chunks = ntok // FFN_TM
    per_chunk = ntok // (nf * chunks)

    def row_copy(idx_smem, i, buf):
        src = pl.multiple_of(idx_smem[0, 0, i], SUBLANES)
        dst = pl.multiple_of(i * SUBLANES, SUBLANES)
        return pltpu.make_async_copy(h_hbm.at[pl.ds(src, SUBLANES), :],
                                     rows_ref.at[buf, pl.ds(dst, SUBLANES), :], sems.at[buf])

    def wait_rows(buf):
        pltpu.make_async_copy(h_hbm.at[pl.ds(0, ntok * SUBLANES), :], rows_ref.at[buf], sems.at[buf]).wait()

    @pl.when((e == 0) & (f == 0))
    def _first_gather():
        def issue(i, carry):
            row_copy(idx_ref, i, 0).start()
            return carry

        lax.fori_loop(0, ntok, issue, 0, unroll=8)

    @pl.when(f == 0)
    def _repack():
        wait_rows(slot)
        for m in range(chunks):
            xe_ref[m * FFN_TM:(m + 1) * FFN_TM, :] = jnp.concatenate(
                [rows_ref[slot, pl.ds(m * FFN_TM * SUBLANES + c, FFN_TM, stride=SUBLANES), :]
                 for c in range(d // LANES)], axis=1).astype(BF16)
        y_ref[...] = jnp.zeros_like(y_ref)

    wg = wg_ref[...].astype(BF16)
    wu = wu_ref[...].astype(BF16)
    wd = wd_ref[...].astype(BF16)
    for m in range(chunks):
        xe = xe_ref[m * FFN_TM:(m + 1) * FFN_TM, :]
        hmid = []
        for n in range(wg.shape[1] // MXU_COLS):
            cols = slice(n * MXU_COLS, (n + 1) * MXU_COLS)
            a = jnp.dot(xe, wg[:, cols], preferred_element_type=F32)
            u = jnp.dot(xe, wu[:, cols], preferred_element_type=F32)
            hmid.append((a * jax.nn.sigmoid(a) * u).astype(BF16))
        hmid = jnp.concatenate(hmid, axis=1)
        y_ref[m * FFN_TM:(m + 1) * FFN_TM, :] += jnp.dot(hmid, wd, preferred_element_type=F32)
        for i in range(per_chunk):
            row_copy(idx_next_ref, (f * chunks + m) * per_chunk + i, 1 - slot).start()

    @pl.when((e == ne - 1) & (f == nf - 1))
    def _drain():
        wait_rows(1 - slot)


def _experts(idx_e, h2_rows, w_gate, w_up, w_down):
    ne, d, ff = w_gate.shape
    ntok = idx_e.shape[-1]
    tf = FFN_TF
    nf = ff // tf
    assert ff % tf == 0 and ntok % FFN_TM == 0 and ntok % (nf * (ntok // FFN_TM)) == 0 and d // LANES == SUBLANES
    return pl.pallas_call(
        functools.partial(_experts_kernel, ne=ne, nf=nf),
        grid=(ne, nf),
        in_specs=[pl.BlockSpec((1, 1, ntok), lambda e, f: (e, 0, 0), memory_space=pltpu.SMEM),
                  pl.BlockSpec((1, 1, ntok), lambda e, f: (jnp.minimum(e + 1, ne - 1), 0, 0),
                               memory_space=pltpu.SMEM),
                  pl.BlockSpec(memory_space=pl.ANY),
                  pl.BlockSpec((None, d, tf), lambda e, f: (e, 0, f)),
                  pl.BlockSpec((None, d, tf), lambda e, f: (e, 0, f)),
                  pl.BlockSpec((None, tf, d), lambda e, f: (e, f, 0))],
        out_specs=pl.BlockSpec((None, ntok, d), lambda e, f: (e, 0, 0)),
        out_shape=jax.ShapeDtypeStruct((ne, ntok, d), F32),
        scratch_shapes=[pltpu.VMEM((2, ntok * SUBLANES, LANES), F32), pltpu.VMEM((ntok, d), BF16),
                        pltpu.SemaphoreType.DMA((2,))],
        compiler_params=_cparams(("arbitrary", "arbitrary"), VMEM_LIMIT),
        name="experts",
    )(idx_e, idx_e, h2_rows, w_gate, w_up, w_down)


def _combine_kernel(*refs, batch, per_step):
    idx_refs, gate_refs, y_refs = (refs[i * per_step:(i + 1) * per_step] for i in range(3))
    x1_ref, g_ref, o_ref, acc_ref, ybuf_ref = refs[3 * per_step:]
    b, j = pl.program_id(0), pl.program_id(1)
    cap, d = y_refs[0].shape
    tm = x1_ref.shape[0]
    cur = b % 2

    @pl.when(b < batch)
    def _scatter():
        @pl.when(j == 0)
        def _init():
            acc_ref[cur] = jnp.zeros(acc_ref.shape[1:], F32)

        for idx_ref, gate_ref, y_ref in zip(idx_refs, gate_refs, y_refs):
            gate_col = jnp.broadcast_to(gate_ref[0], (LANES, cap)).T
            for c in range(d // LANES):
                ybuf_ref[pl.ds(c, cap, stride=SUBLANES), :] = y_ref[:, c * LANES:(c + 1) * LANES] * gate_col

            def group(k, carry, idx_ref=idx_ref):
                rows, sums = [], []
                for u in range(RMW_UNROLL):
                    i = k * RMW_UNROLL + u
                    t = pl.ds(pl.multiple_of(idx_ref[0, 0, i], SUBLANES), SUBLANES)
                    yi = ybuf_ref[pl.ds(pl.multiple_of(i * SUBLANES, SUBLANES), SUBLANES), :]
                    rows.append(t)
                    sums.append(acc_ref[cur, t, :] + yi)
                for t, s in zip(rows, sums):
                    acc_ref[cur, t, :] = s
                return carry

            lax.fori_loop(0, cap // RMW_UNROLL, group, 0)

    @pl.when(b > 0)
    def _emit():
        base = pl.multiple_of(j * tm * SUBLANES, SUBLANES)
        moe = jnp.concatenate([acc_ref[1 - cur, pl.ds(base + c, tm, stride=SUBLANES), :]
                               for c in range(d // LANES)], axis=1)
        o_ref[...] = _rms(x1_ref[...] + moe, g_ref[...])


def _combine(idx_l, gates, y, x1, g_final, batch, seq):
    ne = y.shape[0]
    t, d = x1.shape
    cap = idx_l.shape[-1]
    per_step = COMBINE_EXPERTS
    steps = ne // per_step
    tm = seq // steps
    assert cap % RMW_UNROLL == 0 and d // LANES == SUBLANES and tm * steps == seq and ne % per_step == 0
    scattered = lambda b: jnp.minimum(b, batch - 1)
    rows = lambda b, j: (jnp.maximum(b - 1, 0) * steps + jnp.where(b > 0, j, 0), 0)
    slot_specs = [pl.BlockSpec((1, 1, cap), functools.partial(
        lambda b, j, x: (scattered(b) * ne + j * per_step + x, 0, 0), x=x), memory_space=ms)
        for ms in (pltpu.SMEM, None) for x in range(per_step)]
    y_specs = [pl.BlockSpec((None, cap, d), functools.partial(
        lambda b, j, x: (j * per_step + x, scattered(b), 0), x=x)) for x in range(per_step)]
    return pl.pallas_call(
        functools.partial(_combine_kernel, batch=batch, per_step=per_step),
        grid=(batch + 1, steps),
        in_specs=slot_specs + y_specs + [pl.BlockSpec((tm, d), rows), pl.BlockSpec((1, d), lambda b, j: (0, 0))],
        out_specs=pl.BlockSpec((tm, d), rows),
        out_shape=jax.ShapeDtypeStruct((t, d), F32),
        scratch_shapes=[pltpu.VMEM((2, seq * SUBLANES, LANES), F32), pltpu.VMEM((cap * SUBLANES, LANES), F32)],
        compiler_params=_cparams(("arbitrary", "arbitrary"), VMEM_LIMIT),
        name="combine",
    )(*([idx_l] * per_step + [gates] * per_step + [y] * per_step), x1, g_final.reshape(1, d))


def kernel(x, g_mix, w_in, a_sink, g_out_a, g_out_b, w_out, g_ffn, w_router, w_gate, w_up, w_down, g_final):
    batch, seq, d = x.shape
    depth = w_in.shape[0]
    ne = w_router.shape[-1]
    cap = CAPACITY_FACTOR * seq // ne
    assert depth == 1, "the last kernel fuses the MoE residual with the output norm of a single layer"
    l = 0
    x2 = x.reshape(batch * seq, d)
    qa, ka, va, qb, kb, vb = _proj(x2, g_mix[l], w_in[l], seq)
    oa = _attn_a(qa, ka, va, a_sink[l], batch, seq)
    ob = _attn_b(qb, kb, vb, batch, seq)
    x1, h2_rows, aff_t = _mix(x2, oa, ob, g_out_a[l], g_out_b[l], w_out[l], g_ffn[l], w_router[l], batch, seq)
    idx_e, idx_l, gates = _route(aff_t, cap)
    y = _experts(idx_e, h2_rows, w_gate[l], w_up[l], w_down[l])
    out = _combine(idx_l, gates, y, x1, g_final, batch, seq)
    return out.reshape(batch, seq, d)
```

```python
import functools

import jax
import jax.numpy as jnp
from jax import lax
from jax.experimental import pallas as pl
from jax.experimental.pallas import tpu as pltpu

F32 = jnp.float32
BF16 = jnp.bfloat16

LANES = 128
MXU_COLS = 256
SUBLANES = 8
BF16_ROWS = 16
VMEM_LIMIT = 56 * 1024 * 1024

HEAD_DIM = 64
A_HEADS = 8
A_KV_HEADS = 2
A_GROUP = A_HEADS // A_KV_HEADS
A_HALF_WINDOW = 128
B_HEADS = 8
DILATED_PATTERNS = ((128, 1), (512, 4), (2048, 16))
A_Q = A_HEADS * HEAD_DIM
A_KV = A_KV_HEADS * HEAD_DIM
B_W = B_HEADS * HEAD_DIM
ROPE_THETA = 500000.0
ROT_DIM = HEAD_DIM // 4
CAPACITY_FACTOR = 2
NORM_EPS = 1e-6
NEG_INF = -1e30
LOG2_E = 1.4426950408889634

PROJ_TM = 1024
ATT_QB = 128
ATT_A_SLOTS = 2
ATT_B_SLOTS = 4
MIX_TM = 1024
FFN_TF = 512
FFN_TM = 1024
RMW_UNROLL = 16
RMW_GROUPS = 4
COMBINE_EXPERTS = 2


def _cparams(sem, vmem=None):
    return pltpu.CompilerParams(dimension_semantics=sem, vmem_limit_bytes=vmem)


def _rope_tables(seq_len):
    half = ROT_DIM // 2
    inv_freq = ROPE_THETA ** (-jnp.arange(0, ROT_DIM, 2, dtype=F32) / ROT_DIM)
    ang = jnp.arange(seq_len, dtype=F32)[:, None] * inv_freq[None, :]
    cos = jnp.tile(jnp.cos(ang), (1, LANES // half))
    sin = jnp.tile(jnp.sin(ang), (1, LANES // half))
    j = jnp.arange(LANES) % HEAD_DIM
    cos_t = jnp.where(j < ROT_DIM, cos, 1.0)
    sin_a = jnp.where(j < half, -sin, 0.0)
    sin_b = jnp.where((j >= half) & (j < ROT_DIM), sin, 0.0)
    return cos_t, sin_a, sin_b


def _rms(x, g):
    return x * lax.rsqrt(jnp.mean(x * x, axis=-1, keepdims=True) + NORM_EPS) * g


def _proj_kernel(x_ref, g_ref, w_ref, cos_ref, sa_ref, sb_ref,
                 qa_ref, ka_ref, va_ref, qb_ref, kb_ref, vb_ref, wbf_ref):
    @pl.when(pl.program_id(0) == 0)
    def _cast_weights():
        wbf_ref[...] = w_ref[...].astype(BF16)

    h = _rms(x_ref[...], g_ref[...]).astype(BF16)
    cos, sa, sb = cos_ref[...], sa_ref[...], sb_ref[...]
    half = ROT_DIM // 2

    def rope(t):
        return t * cos + pltpu.roll(t, LANES - half, 1) * sa + pltpu.roll(t, half, 1) * sb

    def blocks(col0, width):
        for n in range(width // MXU_COLS):
            c0 = col0 + n * MXU_COLS
            yield 2 * n, jnp.dot(h, wbf_ref[:, c0:c0 + MXU_COLS], preferred_element_type=F32)

    scale = HEAD_DIM ** -0.5 * LOG2_E
    for c, p in blocks(0, A_Q):
        for i in range(2):
            qa_ref[c + i] = (rope(p[:, i * LANES:(i + 1) * LANES]) * scale).astype(qa_ref.dtype)
    (_, kv), = blocks(A_Q, 2 * A_KV)
    first = lax.broadcasted_iota(jnp.int32, (kv.shape[0], LANES), 1) < HEAD_DIM
    for ref, t in ((ka_ref, rope(kv[:, :A_KV])), (va_ref, kv[:, A_KV:])):
        swapped = pltpu.roll(t, HEAD_DIM, 1)
        ref[0] = jnp.where(first, t, swapped).astype(ref.dtype)
        ref[1] = jnp.where(first, swapped, t).astype(ref.dtype)
    for c, p in blocks(A_Q + 2 * A_KV, B_W):
        for i in range(2):
            qb_ref[c + i] = rope(p[:, i * LANES:(i + 1) * LANES]) * scale
    for c, p in blocks(A_Q + 2 * A_KV + B_W, B_W):
        for i in range(2):
            kb_ref[c + i] = rope(p[:, i * LANES:(i + 1) * LANES])
    for c, p in blocks(A_Q + 2 * A_KV + 2 * B_W, B_W):
        for i in range(2):
            vb_ref[c + i] = p[:, i * LANES:(i + 1) * LANES]


def _proj(x2, g_mix, w_in, seq):
    t, d = x2.shape
    tm = PROJ_TM
    assert t % tm == 0 and seq % tm == 0 and A_KV == LANES and A_KV_HEADS == 2
    nseq = seq // tm
    tables = _rope_tables(seq)
    slab = lambda i: (0, i, 0)
    const = lambda i: (0, 0)
    tab = lambda i: (i % nseq, 0)
    widths = (A_Q, A_KV_HEADS * LANES, A_KV_HEADS * LANES, B_W, B_W, B_W)
    return pl.pallas_call(
        _proj_kernel,
        grid=(t // tm,),
        in_specs=[pl.BlockSpec((tm, d), lambda i: (i, 0)), pl.BlockSpec((1, d), const),
                  pl.BlockSpec(w_in.shape, const)] + [pl.BlockSpec((tm, LANES), tab)] * 3,
        out_specs=[pl.BlockSpec((w // LANES, tm, LANES), slab) for w in widths],
        out_shape=[jax.ShapeDtypeStruct((w // LANES, t, LANES), dt)
                   for w, dt in zip(widths, (BF16, BF16, BF16, F32, F32, F32))],
        scratch_shapes=[pltpu.VMEM(w_in.shape, BF16)],
        compiler_params=_cparams(("arbitrary",), VMEM_LIMIT),
        name="proj",
    )(x2, g_mix.reshape(1, d), w_in, *tables)


def _band_bias(bias_ref, qb, kw, hw):
    kk = lax.broadcasted_iota(jnp.int32, (kw, qb), 0)
    qq = lax.broadcasted_iota(jnp.int32, (kw, qb), 1)
    for t, delta in enumerate((0, -hw, -2 * hw)):
        bias_ref[t] = jnp.where(jnp.abs(kk + delta - qq) <= hw, 0.0, NEG_INF)


def _band_scratch(qb, kw, slots):
    n = 2 * slots
    return ([pltpu.VMEM((kw, 2 * qb), F32)] * n + [pltpu.VMEM((kw, 2 * qb), BF16)] * n
            + [pltpu.VMEM((SUBLANES, 2 * qb), F32)] * n + [pltpu.VMEM((LANES, kw), BF16)] * n
            + [pltpu.VMEM((SUBLANES, 2 * qb), F32)] * n)


def _band_pipeline(q_ref, k_ref, v_ref, bias_ref, o_ref, lse_ref, bufs, *, nblocks, geometry, stage_vt,
                   sink_row=None):
    qb = ATT_QB
    n = len(bufs) // 5
    slots = n // 2
    s_bufs, p_bufs, ml_bufs, vt_bufs, mx_bufs = (bufs[i * n:(i + 1) * n] for i in range(5))
    kw = s_bufs[0].shape[0]
    for s_buf, p_buf, ml_buf, vt_buf, mx_buf in zip(s_bufs, p_bufs, ml_bufs, vt_bufs, mx_bufs):
        mx_buf[...] = jnp.zeros_like(mx_buf)
        s_buf[...] = jnp.zeros_like(s_buf)
        p_buf[...] = jnp.zeros_like(p_buf)
        ml_buf[...] = jnp.ones_like(ml_buf)
        vt_buf[...] = jnp.zeros_like(vt_buf)

    def rows(block):
        q0, k0, kind = geometry(jnp.clip(block, 0, nblocks - 1))
        return pl.ds(pl.multiple_of(q0, BF16_ROWS), qb), pl.ds(pl.multiple_of(k0, BF16_ROWS), kw), kind

    def scores(block, s_buf, mx_buf):
        q_rows, k_rows, kind = rows(block)
        q = q_ref[q_rows, :]
        lo = lax.broadcasted_iota(jnp.int32, q.shape, 1) < HEAD_DIM
        q2 = jnp.concatenate([jnp.where(lo, q, 0.0), jnp.where(lo, 0.0, q)], axis=0).astype(BF16)
        s = lax.dot_general(k_ref[k_rows, :].astype(BF16), q2, (((1,), (1,)), ((), ())),
                            preferred_element_type=F32)
        bias = bias_ref[kind]
        for hh in range(2):
            s_h = s[:, hh * qb:(hh + 1) * qb] + bias
            s_buf[:, hh * qb:(hh + 1) * qb] = s_h
            mx_buf[0:1, hh * qb:(hh + 1) * qb] = jnp.max(s_h, axis=0, keepdims=True)

    def softmax(block, s_buf, mx_buf, p_buf, ml_buf, vt_buf):
        if stage_vt:
            _, k_rows, _ = rows(block)
            vt_buf[...] = v_ref[k_rows, :].astype(F32).T.astype(BF16)
        m = mx_buf[0:1, :]
        if sink_row is not None:
            m = jnp.maximum(m, sink_row)
        p = jnp.exp2(s_buf[...] - m)
        l = jnp.sum(p, axis=0, keepdims=True)
        if sink_row is not None:
            l = l + jnp.exp2(sink_row - m)
        p_buf[...] = p.astype(BF16)
        ml_buf[0:1, :] = m
        ml_buf[1:2, :] = l

    def output(block, p_buf, ml_buf, vt_buf):
        q_rows, k_rows, _ = rows(block)
        if stage_vt:
            o_t = jnp.dot(vt_buf[...], p_buf[...], preferred_element_type=F32)
        else:
            o_t = lax.dot_general(v_ref[k_rows, :].astype(BF16), p_buf[...], (((0,), (0,)), ((), ())),
                                  preferred_element_type=F32)
        o_tc = jnp.concatenate([o_t[:HEAD_DIM, :qb], o_t[HEAD_DIM:, qb:]], axis=0)
        top = lax.broadcasted_iota(jnp.int32, (LANES, qb), 0) < HEAD_DIM
        m, l = ml_buf[0:1, :], ml_buf[1:2, :]
        inv = 1.0 / l
        o_ref[q_rows, :] = (o_tc * jnp.where(top, inv[:, :qb], inv[:, qb:])).T
        if lse_ref is not None:
            lse = m + jnp.log2(l)
            lse_ref[q_rows, :] = jnp.where(top, lse[:, :qb], lse[:, qb:]).T

    def trip_pair(jj, carry):
        for parity in range(2):
            for slot in range(slots):
                t = (2 * jj + parity) * slots + slot
                rd, wr = (1 - parity) * slots + slot, parity * slots + slot
                output(t - 2 * slots, p_bufs[rd], ml_bufs[rd], vt_bufs[rd])
                softmax(t - slots, s_bufs[rd], mx_bufs[rd], p_bufs[wr], ml_bufs[wr], vt_bufs[wr])
                scores(t, s_bufs[wr], mx_bufs[wr])
        return carry

    trips = nblocks // slots + 2
    assert nblocks % slots == 0 and trips % 2 == 0
    lax.fori_loop(0, trips // 2, trip_pair, 0)


def _segment_geometry(block, n, hw, base=0):
    qb = ATT_QB
    nblk = n // qb
    seg, jb = block // nblk, block % nblk
    qs = jb * qb
    ks = jnp.clip(qs - hw, 0, n - (qb + 2 * hw))
    kind = jnp.where(jb == 0, 0, jnp.where(jb == nblk - 1, 2, 1))
    return base + seg * n + qs, base + seg * n + ks, kind


def _attn_a_kernel(sink_ref, q_ref, k_ref, v_ref, o_ref, bias_ref, *bufs, seq):
    j = pl.program_id(1)
    qb, hw = ATT_QB, A_HALF_WINDOW
    _band_bias(bias_ref, qb, qb + 2 * hw, hw)
    col = lax.broadcasted_iota(jnp.int32, (1, 2 * qb), 1)
    sink_row = jnp.where(col < qb, sink_ref[2 * j], sink_ref[2 * j + 1]) * LOG2_E
    _band_pipeline(q_ref, k_ref, v_ref, bias_ref, o_ref, None, bufs, nblocks=seq // qb,
                   geometry=functools.partial(_segment_geometry, n=seq, hw=hw), stage_vt=False,
                   sink_row=sink_row)


def _attn_a(qa, ka, va, sink, batch, seq):
    qb, hw = ATT_QB, A_HALF_WINDOW
    assert seq % qb == 0 and seq >= qb + 2 * hw and A_KV == LANES
    qspec = pl.BlockSpec((None, seq, LANES), lambda b, j: (j, b, 0))
    per_kv = A_GROUP * HEAD_DIM // LANES
    kvspec = pl.BlockSpec((None, seq, LANES), lambda b, j: (j // per_kv, b, 0))
    return pl.pallas_call(
        functools.partial(_attn_a_kernel, seq=seq),
        grid=(batch, A_Q // LANES),
        in_specs=[pl.BlockSpec(memory_space=pltpu.SMEM), qspec, kvspec, kvspec],
        out_specs=qspec,
        out_shape=jax.ShapeDtypeStruct(qa.shape, F32),
        scratch_shapes=[pltpu.VMEM((3, qb + 2 * hw, qb), F32)] + _band_scratch(qb, qb + 2 * hw, ATT_A_SLOTS),
        compiler_params=_cparams(("parallel", "parallel"), VMEM_LIMIT),
        name="attn_a",
    )(sink, qa, ka, va)


def _attn_b_kernel(q_ref, k_ref, v_ref, out_ref, bias_ref, tmp_ref, qs_ref, ks_ref, vs_ref, os_ref, ls_ref,
                   *bufs, seq):
    (w1, d1), (w4, d4), (w16, d16) = DILATED_PATTERNS
    hw = w1 // (2 * d1)
    qb = ATT_QB
    _band_bias(bias_ref, qb, qb + 2 * hw, hw)
    n4, n16 = seq // d4, seq // d16
    step = d16 // d4
    chunk = 512
    for src, dst in ((q_ref, qs_ref), (k_ref, ks_ref), (v_ref, vs_ref)):
        for c in range(seq // chunk):
            dst[c * chunk:(c + 1) * chunk, :] = src[c * chunk:(c + 1) * chunk, :].astype(dst.dtype)
        for r in range(d4):
            tmp_ref[r * n4:(r + 1) * n4, :] = src[pl.ds(r, n4, stride=d4), :]
        for c in range(seq // chunk):
            dst[seq + c * chunk:seq + (c + 1) * chunk, :] = tmp_ref[c * chunk:(c + 1) * chunk, :].astype(dst.dtype)
        for r in range(d16):
            dst[2 * seq + r * n16:2 * seq + (r + 1) * n16, :] = tmp_ref[
                pl.ds((r % d4) * n4 + r // d4, n16, stride=step), :].astype(dst.dtype)

    bps = seq // qb
    lg = [(seq // d // qb).bit_length() - 1 for (_, d) in DILATED_PATTERNS]

    def geometry(block):
        pat, within = block // bps, block % bps
        sh = jnp.where(pat == 0, lg[0], jnp.where(pat == 1, lg[1], lg[2]))
        seg, jb = within >> sh, within & ((1 << sh) - 1)
        n = qb << sh
        qs = jb * qb
        ks = jnp.clip(qs - hw, 0, n - (qb + 2 * hw))
        kind = jnp.where(jb == 0, 0, jnp.where(jb == (1 << sh) - 1, 2, 1))
        base = pat * seq + seg * n
        return base + qs, base + ks, kind

    _band_pipeline(qs_ref, ks_ref, vs_ref, bias_ref, os_ref, ls_ref, bufs,
                   nblocks=len(DILATED_PATTERNS) * bps, geometry=geometry, stage_vt=True)

    def merge(r, carry):
        rows16 = pl.ds(pl.multiple_of(2 * seq + r * n16, SUBLANES), n16)
        rows4 = pl.ds(seq + (r % d4) * n4 + r // d4, n16, stride=step)
        rows1 = pl.ds(r, n16, stride=d16)
        la, lb, lc = ls_ref[rows1, :], ls_ref[rows4, :], ls_ref[rows16, :]
        top = jnp.maximum(jnp.maximum(la, lb), lc)
        wa, wb, wc = jnp.exp2(la - top), jnp.exp2(lb - top), jnp.exp2(lc - top)
        num = wa * os_ref[rows1, :] + wb * os_ref[rows4, :] + wc * os_ref[rows16, :]
        out_ref[rows1, :] = num * (1.0 / (wa + wb + wc))
        return carry

    lax.fori_loop(0, d16, merge, 0)


def _attn_b(qb_, kb, vb, batch, seq):
    (w1, d1), (w4, d4), (w16, d16) = DILATED_PATTERNS
    hw = w1 // (2 * d1)
    qb = ATT_QB
    assert d1 == 1 and d16 % d4 == 0 and w4 // (2 * d4) == hw and w16 // (2 * d16) == hw
    assert seq % 512 == 0 and seq // d16 >= qb + 2 * hw
    for (_, d) in DILATED_PATTERNS:
        nblk = seq // d // qb
        assert nblk * d * qb == seq and nblk & (nblk - 1) == 0
    spec = pl.BlockSpec((None, seq, LANES), lambda b, p: (p, b, 0))
    npat = len(DILATED_PATTERNS)
    return pl.pallas_call(
        functools.partial(_attn_b_kernel, seq=seq),
        grid=(batch, B_W // LANES),
        in_specs=[spec] * 3,
        out_specs=spec,
        out_shape=jax.ShapeDtypeStruct(qb_.shape, F32),
        scratch_shapes=[pltpu.VMEM((3, qb + 2 * hw, qb), F32), pltpu.VMEM((seq, LANES), F32),
                        pltpu.VMEM((npat * seq, LANES), F32), pltpu.VMEM((npat * seq, LANES), BF16),
                        pltpu.VMEM((npat * seq, LANES), BF16), pltpu.VMEM((npat * seq, LANES), F32),
                        pltpu.VMEM((npat * seq, LANES), F32)] + _band_scratch(qb, qb + 2 * hw, ATT_B_SLOTS),
        compiler_params=_cparams(("parallel", "parallel"), VMEM_LIMIT),
        name="attn_b",
    )(qb_, kb, vb)


def _mix_kernel(x_ref, oa_ref, ob_ref, ga_ref, gb_ref, wo_f32_ref, gf_ref, wr_ref,
                x1_ref, h2_ref, aff_ref, wo_ref):
    @pl.when(pl.program_id(0) == 0)
    def _cast_weights():
        wo_ref[...] = wo_f32_ref[...].astype(BF16)

    oa = jnp.concatenate([oa_ref[c] for c in range(oa_ref.shape[0])], axis=1)
    ob = jnp.concatenate([ob_ref[c] for c in range(ob_ref.shape[0])], axis=1)
    mixed = jnp.concatenate([_rms(oa, ga_ref[...]), _rms(ob, gb_ref[...])], axis=1).astype(BF16)
    for n in range(x_ref.shape[1] // MXU_COLS):
        cols = slice(n * MXU_COLS, (n + 1) * MXU_COLS)
        x1_ref[:, cols] = x_ref[:, cols] + jnp.dot(mixed, wo_ref[:, cols], preferred_element_type=F32)
    x1 = x1_ref[...]
    h2 = _rms(x1, gf_ref[...])
    tm = h2.shape[0]
    for c in range(h2.shape[1] // LANES):
        h2_ref[pl.ds(c, tm, stride=SUBLANES), :] = h2[:, c * LANES:(c + 1) * LANES]
    ne = aff_ref.shape[0]
    dn = (((1,), (1,)), ((), ()))
    h_hi = h2.astype(BF16)
    h_mid = (h2 - h_hi.astype(F32)).astype(BF16)
    r = lax.dot_general(wr_ref[...], h_hi, dn, preferred_element_type=F32)
    logits = r[:ne] + r[ne:] + lax.dot_general(wr_ref[:ne, :], h_mid, dn, preferred_element_type=F32)
    e = jnp.exp(logits - jnp.max(logits, axis=0, keepdims=True))
    aff_ref[...] = e / jnp.sum(e, axis=0, keepdims=True)


def _mix(x2, oa, ob, g_out_a, g_out_b, w_out, g_ffn, w_router, batch, seq):
    t, d = x2.shape
    tm = MIX_TM
    assert seq % tm == 0 and d % LANES == 0 and d // LANES == SUBLANES
    ns = seq // tm
    row = lambda i: (i, 0)
    const = lambda i: (0, 0)
    ne = w_router.shape[1]
    w_hi = w_router.T.astype(BF16)
    w_mid = (w_router.T - w_hi.astype(F32)).astype(BF16)
    wr = jnp.concatenate([w_hi, w_mid], axis=0)
    return pl.pallas_call(
        _mix_kernel,
        grid=(t // tm,),
        in_specs=[pl.BlockSpec((tm, d), row), pl.BlockSpec((A_Q // LANES, tm, LANES), lambda i: (0, i, 0)),
                  pl.BlockSpec((B_W // LANES, tm, LANES), lambda i: (0, i, 0)),
                  pl.BlockSpec((1, A_Q), const), pl.BlockSpec((1, B_W), const),
                  pl.BlockSpec(w_out.shape, const), pl.BlockSpec((1, d), const),
                  pl.BlockSpec((2 * ne, d), const)],
        out_specs=[pl.BlockSpec((tm, d), row), pl.BlockSpec((tm * SUBLANES, LANES), row),
                   pl.BlockSpec((None, ne, tm), lambda i: (i // ns, 0, i % ns))],
        out_shape=[jax.ShapeDtypeStruct((t, d), F32),
                   jax.ShapeDtypeStruct((t * SUBLANES, LANES), F32),
                   jax.ShapeDtypeStruct((batch, ne, seq), F32)],
        scratch_shapes=[pltpu.VMEM(w_out.shape, BF16)],
        compiler_params=_cparams(("arbitrary",), VMEM_LIMIT),
        name="mix",
    )(x2, oa, ob, g_out_a.reshape(1, -1), g_out_b.reshape(1, -1), w_out, g_ffn.reshape(1, d), wr)


def _lane_cumsum(mask_f, tri):
    rows, s = mask_f.shape
    carry = jnp.zeros((rows, 1), F32)
    out = []
    for c in range(s // LANES):
        inc = jnp.dot(mask_f[:, c * LANES:(c + 1) * LANES].astype(BF16), tri,
                      preferred_element_type=F32) + carry
        out.append(inc)
        carry = inc[:, LANES - 1:LANES]
    return jnp.concatenate(out, axis=1)


def _route_kernel(aff_ref, idx_e_ref, idx_l_ref, gate_ref, *, cap, seq):
    b = pl.program_id(0)
    aff = aff_ref[...]
    ne = aff.shape[0]

    def search(i, thr):
        lo = 28 - 2 * i
        for digit in (1, 2, 3):
            cand = thr | (jnp.int32(digit) << lo)
            cnt = jnp.sum((aff >= lax.bitcast_convert_type(cand, F32)).astype(F32), axis=1, keepdims=True)
            best = jnp.where(cnt >= cap, cand, thr if digit == 1 else best)
        return best

    thr = lax.bitcast_convert_type(lax.fori_loop(0, 15, search, jnp.zeros((ne, 1), jnp.int32)), F32)
    gt = aff > thr
    eq = aff == thr
    n_gt = jnp.sum(gt.astype(F32), axis=1, keepdims=True)
    ri = lax.broadcasted_iota(jnp.int32, (LANES, LANES), 0)
    ci = lax.broadcasted_iota(jnp.int32, (LANES, LANES), 1)
    tri = (ri <= ci).astype(BF16)
    eq_f = eq.astype(F32)
    eq_rank = _lane_cumsum(eq_f, tri) - eq_f
    sel = gt | (eq & (eq_rank < cap - n_gt))
    sel_f = sel.astype(F32)
    slot = _lane_cumsum(sel_f, tri) - sel_f
    key_t = jnp.where(sel, slot, -1.0).T
    tok = lax.broadcasted_iota(jnp.int32, (SUBLANES, seq), 1)
    piece = lax.broadcasted_iota(jnp.int32, (SUBLANES, seq), 0)
    tok_rows = jnp.where(piece == 0, tok // 64, jnp.where(piece == 1, tok % 64, 0)).astype(F32)
    p_iota = lax.broadcasted_iota(jnp.int32, (1, cap), 1).astype(F32)
    for e in range(ne):
        onehot = (key_t[:, e:e + 1] == p_iota).astype(BF16)
        g = aff[e:e + 1, :]
        g_hi = g.astype(BF16).astype(F32)
        g_mid = (g - g_hi).astype(BF16).astype(F32)
        g_lo = g - g_hi - g_mid
        lhs = jnp.where(piece == 2, g_hi, jnp.where(piece == 3, g_mid, jnp.where(piece == 4, g_lo, tok_rows)))
        res = jnp.dot(lhs.astype(BF16), onehot, preferred_element_type=F32)
        tile_row = (res[0:1] * 64.0 + res[1:2]).astype(jnp.int32) * SUBLANES
        idx_l_ref[e] = tile_row + (b % 2) * (seq * SUBLANES)
        idx_e_ref[e] = tile_row + b * (seq * SUBLANES)
        gate_ref[e] = res[2:3] + res[3:4] + res[4:5]


def _route(aff_t, cap):
    batch, ne, seq = aff_t.shape
    assert seq % LANES == 0 and seq <= 64 * 256
    return pl.pallas_call(
        functools.partial(_route_kernel, cap=cap, seq=seq),
        grid=(batch,),
        in_specs=[pl.BlockSpec((None, ne, seq), lambda b: (b, 0, 0))],
        out_specs=[pl.BlockSpec((ne, 1, cap), lambda b: (0, 0, b)), pl.BlockSpec((ne, 1, cap), lambda b: (b, 0, 0)),
                   pl.BlockSpec((ne, 1, cap), lambda b: (b, 0, 0))],
        out_shape=[jax.ShapeDtypeStruct((ne, 1, batch * cap), jnp.int32),
                   jax.ShapeDtypeStruct((batch * ne, 1, cap), jnp.int32),
                   jax.ShapeDtypeStruct((batch * ne, 1, cap), F32)],
        compiler_params=_cparams(("parallel",), VMEM_LIMIT),
        name="route",
    )(aff_t)


def _experts_kernel(idx_ref, idx_next_ref, h_hbm, wg_ref, wu_ref, wd_ref, y_ref, rows_ref, xe_ref, sems,
                    *, ne, nf):
    e, f = pl.program_id(0), pl.program_id(1)
    ntok, d = xe_ref.shape
    slot = e % 2
    chunks = ntok // FFN_TM
    per_chunk = ntok // (nf * chunks)

    def row_copy(idx_smem, i, buf):
        src = pl.multiple_of(idx_smem[0, 0, i], SUBLANES)
        dst = pl.multiple_of(i * SUBLANES, SUBLANES)
        return pltpu.make_async_copy(h_hbm.at[pl.ds(src, SUBLANES), :],
                                     rows_ref.at[buf, pl.ds(dst, SUBLANES), :], sems.at[buf])

    def wait_rows(buf):
        pltpu.make_async_copy(h_hbm.at[pl.ds(0, ntok * SUBLANES), :], rows_ref.at[buf], sems.at[buf]).wait()

    @pl.when((e == 0) & (f == 0))
    def _first_gather():
        def issue(i, carry):
            row_copy(idx_ref, i, 0).start()
            return carry

        lax.fori_loop(0, ntok, issue, 0, unroll=8)

    @pl.when(f == 0)
    def _repack():
        wait_rows(slot)
        for m in range(chunks):
            xe_ref[m * FFN_TM:(m + 1) * FFN_TM, :] = jnp.concatenate(
                [rows_ref[slot, pl.ds(m * FFN_TM * SUBLANES + c, FFN_TM, stride=SUBLANES), :]
                 for c in range(d // LANES)], axis=1).astype(BF16)
        y_ref[...] = jnp.zeros_like(y_ref)

    wg = wg_ref[...].astype(BF16)
    wu = wu_ref[...].astype(BF16)
    wd = wd_ref[...].astype(BF16)
    for m in range(chunks):
        xe = xe_ref[m * FFN_TM:(m + 1) * FFN_TM, :]
        hmid = []
        for n in range(wg.shape[1] // MXU_COLS):
            cols = slice(n * MXU_COLS, (n + 1) * MXU_COLS)
            a = jnp.dot(xe, wg[:, cols], preferred_element_type=F32)
            u = jnp.dot(xe, wu[:, cols], preferred_element_type=F32)
            hmid.append((a * jax.nn.sigmoid(a) * u).astype(BF16))
        hmid = jnp.concatenate(hmid, axis=1)
        y_ref[m * FFN_TM:(m + 1) * FFN_TM, :] += jnp.dot(hmid, wd, preferred_element_type=F32)
        for i in range(per_chunk):
            row_copy(idx_next_ref, (f * chunks + m) * per_chunk + i, 1 - slot).start()

    @pl.when((e == ne - 1) & (f == nf - 1))
    def _drain():
        wait_rows(1 - slot)


def _experts(idx_e, h2_rows, w_gate, w_up, w_down):
    ne, d, ff = w_gate.shape
    ntok = idx_e.shape[-1]
    tf = FFN_TF
    nf = ff // tf
    assert ff % tf == 0 and ntok % FFN_TM == 0 and ntok % (nf * (ntok // FFN_TM)) == 0 and d // LANES == SUBLANES
    return pl.pallas_call(
        functools.partial(_experts_kernel, ne=ne, nf=nf),
        grid=(ne, nf),
        in_specs=[pl.BlockSpec((1, 1, ntok), lambda e, f: (e, 0, 0), memory_space=pltpu.SMEM),
                  pl.BlockSpec((1, 1, ntok), lambda e, f: (jnp.minimum(e + 1, ne - 1), 0, 0),
                               memory_space=pltpu.SMEM),
                  pl.BlockSpec(memory_space=pl.ANY),
                  pl.BlockSpec((None, d, tf), lambda e, f: (e, 0, f)),
                  pl.BlockSpec((None, d, tf), lambda e, f: (e, 0, f)),
                  pl.BlockSpec((None, tf, d), lambda e, f: (e, f, 0))],
        out_specs=pl.BlockSpec((None, ntok, d), lambda e, f: (e, 0, 0)),
        out_shape=jax.ShapeDtypeStruct((ne, ntok, d), F32),
        scratch_shapes=[pltpu.VMEM((2, ntok * SUBLANES, LANES), F32), pltpu.VMEM((ntok, d), BF16),
                        pltpu.SemaphoreType.DMA((2,))],
        compiler_params=_cparams(("arbitrary", "arbitrary"), VMEM_LIMIT),
        name="experts",
    )(idx_e, idx_e, h2_rows, w_gate, w_up, w_down)


def _combine_kernel(*refs, batch, per_step):
    idx_refs, gate_refs, y_refs = (refs[i * per_step:(i + 1) * per_step] for i in range(3))
    x1_ref, g_ref, o_ref, acc_ref, ybuf_ref = refs[3 * per_step:]
    b, j = pl.program_id(0), pl.program_id(1)
    cap, d = y_refs[0].shape
    tm = x1_ref.shape[0]
    half_rows = acc_ref.shape[0] // 2
    cur = pl.multiple_of((b % 2) * half_rows, SUBLANES)
    prev = pl.multiple_of(((b + 1) % 2) * half_rows, SUBLANES)

    @pl.when(b < batch)
    def _scatter():
        @pl.when(j == 0)
        def _init():
            acc_ref[pl.ds(cur, half_rows), :] = jnp.zeros((half_rows, LANES), F32)

        for idx_ref, gate_ref, y_ref in zip(idx_refs, gate_refs, y_refs):
            gate_col = jnp.broadcast_to(gate_ref[0], (LANES, cap)).T
            for c in range(d // LANES):
                ybuf_ref[pl.ds(c, cap, stride=SUBLANES), :] = y_ref[:, c * LANES:(c + 1) * LANES] * gate_col

            def group(k, carry, idx_ref=idx_ref):
                rows, sums = [], []
                for u in range(RMW_UNROLL):
                    i = k * RMW_UNROLL + u
                    t = pl.ds(pl.multiple_of(idx_ref[0, 0, i], SUBLANES), SUBLANES)
                    yi = ybuf_ref[pl.ds(pl.multiple_of(i * SUBLANES, SUBLANES), SUBLANES), :]
                    rows.append(t)
                    sums.append(acc_ref[t, :] + yi)
                for t, s in zip(rows, sums):
                    acc_ref[t, :] = s
                return carry

            lax.fori_loop(0, cap // RMW_UNROLL, group, 0, unroll=RMW_GROUPS)

    @pl.when(b > 0)
    def _emit():
        base = prev + pl.multiple_of(j * tm * SUBLANES, SUBLANES)
        moe = jnp.concatenate([acc_ref[pl.ds(base + c, tm, stride=SUBLANES), :]
                               for c in range(d // LANES)], axis=1)
        o_ref[...] = _rms(x1_ref[...] + moe, g_ref[...])


def _combine(idx_l, gates, y, x1, g_final, batch, seq):
    ne = y.shape[0]
    t, d = x1.shape
    cap = idx_l.shape[-1]
    per_step = COMBINE_EXPERTS
    steps = ne // per_step
    tm = seq // steps
    assert cap % RMW_UNROLL == 0 and d // LANES == SUBLANES and tm * steps == seq and ne % per_step == 0
    scattered = lambda b: jnp.minimum(b, batch - 1)
    rows = lambda b, j: (jnp.maximum(b - 1, 0) * steps + jnp.where(b > 0, j, 0), 0)
    slot_specs = [pl.BlockSpec((1, 1, cap), functools.partial(
        lambda b, j, x: (scattered(b) * ne + j * per_step + x, 0, 0), x=x), memory_space=ms)
        for ms in (pltpu.SMEM, None) for x in range(per_step)]
    y_specs = [pl.BlockSpec((None, cap, d), functools.partial(
        lambda b, j, x: (j * per_step + x, scattered(b), 0), x=x)) for x in range(per_step)]
    return pl.pallas_call(
        functools.partial(_combine_kernel, batch=batch, per_step=per_step),
        grid=(batch + 1, steps),
        in_specs=slot_specs + y_specs + [pl.BlockSpec((tm, d), rows), pl.BlockSpec((1, d), lambda b, j: (0, 0))],
        out_specs=pl.BlockSpec((tm, d), rows),
        out_shape=jax.ShapeDtypeStruct((t, d), F32),
        scratch_shapes=[pltpu.VMEM((2 * seq * SUBLANES, LANES), F32), pltpu.VMEM((cap * SUBLANES, LANES), F32)],
        compiler_params=_cparams(("arbitrary", "arbitrary"), VMEM_LIMIT),
        name="combine",
    )(*([idx_l] * per_step + [gates] * per_step + [y] * per_step), x1, g_final.reshape(1, d))


def kernel(x, g_mix, w_in, a_sink, g_out_a, g_out_b, w_out, g_ffn, w_router, w_gate, w_up, w_down, g_final):
    batch, seq, d = x.shape
    depth = w_in.shape[0]
    ne = w_router.shape[-1]
    cap = CAPACITY_FACTOR * seq // ne
    assert depth == 1, "the last kernel fuses the MoE residual with the output norm of a single layer"
    l = 0
    x2 = x.reshape(batch * seq, d)
    qa, ka, va, qb, kb, vb = _proj(x2, g_mix[l], w_in[l], seq)
    oa = _attn_a(qa, ka, va, a_sink[l], batch, seq)
    ob = _attn_b(qb, kb, vb, batch, seq)
    x1, h2_rows, aff_t = _mix(x2, oa, ob, g_out_a[l], g_out_b[l], w_out[l], g_ffn[l], w_router[l], batch, seq)
    idx_e, idx_l, gates = _route(aff_t, cap)
    y = _experts(idx_e, h2_rows, w_gate[l], w_up[l], w_down[l])
    out = _combine(idx_l, gates, y, x1, g_final, batch, seq)
    return out.reshape(batch, seq, d)
```

```python
import functools

import jax
import jax.numpy as jnp
from jax import lax
from jax.experimental import pallas as pl
from jax.experimental.pallas import tpu as pltpu

F32 = jnp.float32
BF16 = jnp.bfloat16

LANES = 128
MXU_COLS = 256
SUBLANES = 8
BF16_ROWS = 16
VMEM_LIMIT = 56 * 1024 * 1024

HEAD_DIM = 64
A_HEADS = 8
A_KV_HEADS = 2
A_GROUP = A_HEADS // A_KV_HEADS
A_HALF_WINDOW = 128
B_HEADS = 8
DILATED_PATTERNS = ((128, 1), (512, 4), (2048, 16))
A_Q = A_HEADS * HEAD_DIM
A_KV = A_KV_HEADS * HEAD_DIM
B_W = B_HEADS * HEAD_DIM
ROPE_THETA = 500000.0
ROT_DIM = HEAD_DIM // 4
CAPACITY_FACTOR = 2
NORM_EPS = 1e-6
NEG_INF = -1e30
LOG2_E = 1.4426950408889634

PROJ_TM = 1024
ATT_QB = 128
ATT_A_SLOTS = 2
ATT_B_SLOTS = 4
MIX_TM = 1024
FFN_TF = 512
FFN_TM = 1024
RMW_UNROLL = 16
RMW_GROUPS = 4
COMBINE_EXPERTS = 2


def _cparams(sem, vmem=None):
    return pltpu.CompilerParams(dimension_semantics=sem, vmem_limit_bytes=vmem)


def _rope_tables(seq_len):
    half = ROT_DIM // 2
    inv_freq = ROPE_THETA ** (-jnp.arange(0, ROT_DIM, 2, dtype=F32) / ROT_DIM)
    ang = jnp.arange(seq_len, dtype=F32)[:, None] * inv_freq[None, :]
    cos = jnp.tile(jnp.cos(ang), (1, LANES // half))
    sin = jnp.tile(jnp.sin(ang), (1, LANES // half))
    j = jnp.arange(LANES) % HEAD_DIM
    cos_t = jnp.where(j < ROT_DIM, cos, 1.0)
    sin_a = jnp.where(j < half, -sin, 0.0)
    sin_b = jnp.where((j >= half) & (j < ROT_DIM), sin, 0.0)
    return cos_t, sin_a, sin_b


def _rms(x, g):
    return x * lax.rsqrt(jnp.mean(x * x, axis=-1, keepdims=True) + NORM_EPS) * g


def _proj_kernel(x_ref, g_ref, w_ref, cos_ref, sa_ref, sb_ref,
                 qa_ref, ka_ref, va_ref, qb_ref, kb_ref, vb_ref, wbf_ref):
    @pl.when(pl.program_id(0) == 0)
    def _cast_weights():
        wbf_ref[...] = w_ref[...].astype(BF16)

    h = _rms(x_ref[...], g_ref[...]).astype(BF16)
    cos, sa, sb = cos_ref[...], sa_ref[...], sb_ref[...]
    half = ROT_DIM // 2

    def rope(t):
        return t * cos + pltpu.roll(t, LANES - half, 1) * sa + pltpu.roll(t, half, 1) * sb

    def blocks(col0, width):
        for n in range(width // MXU_COLS):
            c0 = col0 + n * MXU_COLS
            yield 2 * n, jnp.dot(h, wbf_ref[:, c0:c0 + MXU_COLS], preferred_element_type=F32)

    scale = HEAD_DIM ** -0.5 * LOG2_E
    for c, p in blocks(0, A_Q):
        for i in range(2):
            qa_ref[c + i] = (rope(p[:, i * LANES:(i + 1) * LANES]) * scale).astype(qa_ref.dtype)
    (_, kv), = blocks(A_Q, 2 * A_KV)
    first = lax.broadcasted_iota(jnp.int32, (kv.shape[0], LANES), 1) < HEAD_DIM
    for ref, t in ((ka_ref, rope(kv[:, :A_KV])), (va_ref, kv[:, A_KV:])):
        swapped = pltpu.roll(t, HEAD_DIM, 1)
        ref[0] = jnp.where(first, t, swapped).astype(ref.dtype)
        ref[1] = jnp.where(first, swapped, t).astype(ref.dtype)
    for c, p in blocks(A_Q + 2 * A_KV, B_W):
        for i in range(2):
            qb_ref[c + i] = rope(p[:, i * LANES:(i + 1) * LANES]) * scale
    for c, p in blocks(A_Q + 2 * A_KV + B_W, B_W):
        for i in range(2):
            kb_ref[c + i] = rope(p[:, i * LANES:(i + 1) * LANES])
    for c, p in blocks(A_Q + 2 * A_KV + 2 * B_W, B_W):
        for i in range(2):
            vb_ref[c + i] = p[:, i * LANES:(i + 1) * LANES]


def _proj(x2, g_mix, w_in, seq):
    t, d = x2.shape
    tm = PROJ_TM
    assert t % tm == 0 and seq % tm == 0 and A_KV == LANES and A_KV_HEADS == 2
    nseq = seq // tm
    tables = _rope_tables(seq)
    slab = lambda i: (0, i, 0)
    const = lambda i: (0, 0)
    tab = lambda i: (i % nseq, 0)
    widths = (A_Q, A_KV_HEADS * LANES, A_KV_HEADS * LANES, B_W, B_W, B_W)
    return pl.pallas_call(
        _proj_kernel,
        grid=(t // tm,),
        in_specs=[pl.BlockSpec((tm, d), lambda i: (i, 0)), pl.BlockSpec((1, d), const),
                  pl.BlockSpec(w_in.shape, const)] + [pl.BlockSpec((tm, LANES), tab)] * 3,
        out_specs=[pl.BlockSpec((w // LANES, tm, LANES), slab) for w in widths],
        out_shape=[jax.ShapeDtypeStruct((w // LANES, t, LANES), dt)
                   for w, dt in zip(widths, (BF16, BF16, BF16, F32, F32, F32))],
        scratch_shapes=[pltpu.VMEM(w_in.shape, BF16)],
        compiler_params=_cparams(("arbitrary",), VMEM_LIMIT),
        name="proj",
    )(x2, g_mix.reshape(1, d), w_in, *tables)


def _band_bias(bias_ref, qb, kw, hw):
    kk = lax.broadcasted_iota(jnp.int32, (kw, qb), 0)
    qq = lax.broadcasted_iota(jnp.int32, (kw, qb), 1)
    for t, delta in enumerate((0, -hw, -2 * hw)):
        bias_ref[t] = jnp.where(jnp.abs(kk + delta - qq) <= hw, 0.0, NEG_INF)


def _band_scratch(qb, kw, slots):
    n = 2 * slots
    return ([pltpu.VMEM((kw, 2 * qb), F32)] * n + [pltpu.VMEM((kw, 2 * qb), BF16)] * n
            + [pltpu.VMEM((SUBLANES, 2 * qb), F32)] * n + [pltpu.VMEM((LANES, kw), BF16)] * n
            + [pltpu.VMEM((SUBLANES, 2 * qb), F32)] * n)


def _band_pipeline(q_ref, k_ref, v_ref, bias_ref, o_ref, lse_ref, bufs, *, nblocks, geometry, stage_vt,
                   sink_row=None):
    qb = ATT_QB
    n = len(bufs) // 5
    slots = n // 2
    s_bufs, p_bufs, ml_bufs, vt_bufs, mx_bufs = (bufs[i * n:(i + 1) * n] for i in range(5))
    kw = s_bufs[0].shape[0]
    for s_buf, p_buf, ml_buf, vt_buf, mx_buf in zip(s_bufs, p_bufs, ml_bufs, vt_bufs, mx_bufs):
        mx_buf[...] = jnp.zeros_like(mx_buf)
        s_buf[...] = jnp.zeros_like(s_buf)
        p_buf[...] = jnp.zeros_like(p_buf)
        ml_buf[...] = jnp.ones_like(ml_buf)
        vt_buf[...] = jnp.zeros_like(vt_buf)

    def rows(block):
        q0, k0, kind = geometry(jnp.clip(block, 0, nblocks - 1))
        return pl.ds(pl.multiple_of(q0, BF16_ROWS), qb), pl.ds(pl.multiple_of(k0, BF16_ROWS), kw), kind

    def scores(block, s_buf, mx_buf):
        q_rows, k_rows, kind = rows(block)
        q = q_ref[q_rows, :]
        lo = lax.broadcasted_iota(jnp.int32, q.shape, 1) < HEAD_DIM
        q2 = jnp.concatenate([jnp.where(lo, q, 0.0), jnp.where(lo, 0.0, q)], axis=0).astype(BF16)
        s = lax.dot_general(k_ref[k_rows, :].astype(BF16), q2, (((1,), (1,)), ((), ())),
                            preferred_element_type=F32)
        bias = bias_ref[kind]
        for hh in range(2):
            s_h = s[:, hh * qb:(hh + 1) * qb] + bias
            s_buf[:, hh * qb:(hh + 1) * qb] = s_h
            mx_buf[0:1, hh * qb:(hh + 1) * qb] = jnp.max(s_h, axis=0, keepdims=True)

    def softmax(block, s_buf, mx_buf, p_buf, ml_buf, vt_buf):
        if stage_vt:
            _, k_rows, _ = rows(block)
            vt_buf[...] = v_ref[k_rows, :].astype(F32).T.astype(BF16)
        m = mx_buf[0:1, :]
        if sink_row is not None:
            m = jnp.maximum(m, sink_row)
        p = jnp.exp2(s_buf[...] - m)
        l = jnp.sum(p, axis=0, keepdims=True)
        if sink_row is not None:
            l = l + jnp.exp2(sink_row - m)
        p_buf[...] = p.astype(BF16)
        ml_buf[0:1, :] = m
        ml_buf[1:2, :] = l

    def output(block, p_buf, ml_buf, vt_buf):
        q_rows, k_rows, _ = rows(block)
        if stage_vt:
            o_t = jnp.dot(vt_buf[...], p_buf[...], preferred_element_type=F32)
        else:
            o_t = lax.dot_general(v_ref[k_rows, :].astype(BF16), p_buf[...], (((0,), (0,)), ((), ())),
                                  preferred_element_type=F32)
        o_tc = jnp.concatenate([o_t[:HEAD_DIM, :qb], o_t[HEAD_DIM:, qb:]], axis=0)
        top = lax.broadcasted_iota(jnp.int32, (LANES, qb), 0) < HEAD_DIM
        m, l = ml_buf[0:1, :], ml_buf[1:2, :]
        inv = 1.0 / l
        o_ref[q_rows, :] = (o_tc * jnp.where(top, inv[:, :qb], inv[:, qb:])).T
        if lse_ref is not None:
            lse = m + jnp.log2(l)
            lse_ref[q_rows, :] = jnp.where(top, lse[:, :qb], lse[:, qb:]).T

    def trip_pair(jj, carry):
        for parity in range(2):
            for slot in range(slots):
                t = (2 * jj + parity) * slots + slot
                rd, wr = (1 - parity) * slots + slot, parity * slots + slot
                output(t - 2 * slots, p_bufs[rd], ml_bufs[rd], vt_bufs[rd])
                softmax(t - slots, s_bufs[rd], mx_bufs[rd], p_bufs[wr], ml_bufs[wr], vt_bufs[wr])
                scores(t, s_bufs[wr], mx_bufs[wr])
        return carry

    trips = nblocks // slots + 2
    assert nblocks % slots == 0 and trips % 2 == 0
    lax.fori_loop(0, trips // 2, trip_pair, 0, unroll=True)


def _segment_geometry(block, n, hw, base=0):
    qb = ATT_QB
    nblk = n // qb
    seg, jb = block // nblk, block % nblk
    qs = jb * qb
    ks = jnp.clip(qs - hw, 0, n - (qb + 2 * hw))
    kind = jnp.where(jb == 0, 0, jnp.where(jb == nblk - 1, 2, 1))
    return base + seg * n + qs, base + seg * n + ks, kind


def _attn_a_kernel(sink_ref, q_ref, k_ref, v_ref, o_ref, bias_ref, *bufs, seq):
    j = pl.program_id(1)
    qb, hw = ATT_QB, A_HALF_WINDOW
    _band_bias(bias_ref, qb, qb + 2 * hw, hw)
    col = lax.broadcasted_iota(jnp.int32, (1, 2 * qb), 1)
    sink_row = jnp.where(col < qb, sink_ref[2 * j], sink_ref[2 * j + 1]) * LOG2_E
    _band_pipeline(q_ref, k_ref, v_ref, bias_ref, o_ref, None, bufs, nblocks=seq // qb,
                   geometry=functools.partial(_segment_geometry, n=seq, hw=hw), stage_vt=False,
                   sink_row=sink_row)


def _attn_a(qa, ka, va, sink, batch, seq):
    qb, hw = ATT_QB, A_HALF_WINDOW
    assert seq % qb == 0 and seq >= qb + 2 * hw and A_KV == LANES
    qspec = pl.BlockSpec((None, seq, LANES), lambda b, j: (j, b, 0))
    per_kv = A_GROUP * HEAD_DIM // LANES
    kvspec = pl.BlockSpec((None, seq, LANES), lambda b, j: (j // per_kv, b, 0))
    return pl.pallas_call(
        functools.partial(_attn_a_kernel, seq=seq),
        grid=(batch, A_Q // LANES),
        in_specs=[pl.BlockSpec(memory_space=pltpu.SMEM), qspec, kvspec, kvspec],
        out_specs=qspec,
        out_shape=jax.ShapeDtypeStruct(qa.shape, F32),
        scratch_shapes=[pltpu.VMEM((3, qb + 2 * hw, qb), F32)] + _band_scratch(qb, qb + 2 * hw, ATT_A_SLOTS),
        compiler_params=_cparams(("parallel", "parallel"), VMEM_LIMIT),
        name="attn_a",
    )(sink, qa, ka, va)


def _attn_b_kernel(q_ref, k_ref, v_ref, out_ref, bias_ref, tmp_ref, qs_ref, ks_ref, vs_ref, os_ref, ls_ref,
                   *bufs, seq):
    (w1, d1), (w4, d4), (w16, d16) = DILATED_PATTERNS
    hw = w1 // (2 * d1)
    qb = ATT_QB
    _band_bias(bias_ref, qb, qb + 2 * hw, hw)
    n4, n16 = seq // d4, seq // d16
    step = d16 // d4
    chunk = 512
    for src, dst in ((q_ref, qs_ref), (k_ref, ks_ref), (v_ref, vs_ref)):
        for c in range(seq // chunk):
            dst[c * chunk:(c + 1) * chunk, :] = src[c * chunk:(c + 1) * chunk, :].astype(dst.dtype)
        for r in range(d4):
            tmp_ref[r * n4:(r + 1) * n4, :] = src[pl.ds(r, n4, stride=d4), :]
        for c in range(seq // chunk):
            dst[seq + c * chunk:seq + (c + 1) * chunk, :] = tmp_ref[c * chunk:(c + 1) * chunk, :].astype(dst.dtype)
        for r in range(d16):
            dst[2 * seq + r * n16:2 * seq + (r + 1) * n16, :] = tmp_ref[
                pl.ds((r % d4) * n4 + r // d4, n16, stride=step), :].astype(dst.dtype)

    bps = seq // qb
    lg = [(seq // d // qb).bit_length() - 1 for (_, d) in DILATED_PATTERNS]

    def geometry(block):
        pat, within = block // bps, block % bps
        sh = jnp.where(pat == 0, lg[0], jnp.where(pat == 1, lg[1], lg[2]))
        seg, jb = within >> sh, within & ((1 << sh) - 1)
        n = qb << sh
        qs = jb * qb
        ks = jnp.clip(qs - hw, 0, n - (qb + 2 * hw))
        kind = jnp.where(jb == 0, 0, jnp.where(jb == (1 << sh) - 1, 2, 1))
        base = pat * seq + seg * n
        return base + qs, base + ks, kind

    _band_pipeline(qs_ref, ks_ref, vs_ref, bias_ref, os_ref, ls_ref, bufs,
                   nblocks=len(DILATED_PATTERNS) * bps, geometry=geometry, stage_vt=True)

    def merge(r, carry):
        rows16 = pl.ds(pl.multiple_of(2 * seq + r * n16, SUBLANES), n16)
        rows4 = pl.ds(seq + (r % d4) * n4 + r // d4, n16, stride=step)
        rows1 = pl.ds(r, n16, stride=d16)
        la, lb, lc = ls_ref[rows1, :], ls_ref[rows4, :], ls_ref[rows16, :]
        top = jnp.maximum(jnp.maximum(la, lb), lc)
        wa, wb, wc = jnp.exp2(la - top), jnp.exp2(lb - top), jnp.exp2(lc - top)
        num = wa * os_ref[rows1, :] + wb * os_ref[rows4, :] + wc * os_ref[rows16, :]
        out_ref[rows1, :] = num * (1.0 / (wa + wb + wc))
        return carry

    lax.fori_loop(0, d16, merge, 0, unroll=4)


def _attn_b(qb_, kb, vb, batch, seq):
    (w1, d1), (w4, d4), (w16, d16) = DILATED_PATTERNS
    hw = w1 // (2 * d1)
    qb = ATT_QB
    assert d1 == 1 and d16 % d4 == 0 and w4 // (2 * d4) == hw and w16 // (2 * d16) == hw
    assert seq % 512 == 0 and seq // d16 >= qb + 2 * hw
    for (_, d) in DILATED_PATTERNS:
        nblk = seq // d // qb
        assert nblk * d * qb == seq and nblk & (nblk - 1) == 0
    spec = pl.BlockSpec((None, seq, LANES), lambda b, p: (p, b, 0))
    npat = len(DILATED_PATTERNS)
    return pl.pallas_call(
        functools.partial(_attn_b_kernel, seq=seq),
        grid=(batch, B_W // LANES),
        in_specs=[spec] * 3,
        out_specs=spec,
        out_shape=jax.ShapeDtypeStruct(qb_.shape, F32),
        scratch_shapes=[pltpu.VMEM((3, qb + 2 * hw, qb), F32), pltpu.VMEM((seq, LANES), F32),
                        pltpu.VMEM((npat * seq, LANES), F32), pltpu.VMEM((npat * seq, LANES), BF16),
                        pltpu.VMEM((npat * seq, LANES), BF16), pltpu.VMEM((npat * seq, LANES), F32),
                        pltpu.VMEM((npat * seq, LANES), F32)] + _band_scratch(qb, qb + 2 * hw, ATT_B_SLOTS),
        compiler_params=_cparams(("parallel", "parallel"), VMEM_LIMIT),
        name="attn_b",
    )(qb_, kb, vb)


def _mix_kernel(x_ref, oa_ref, ob_ref, ga_ref, gb_ref, wo_f32_ref, gf_ref, wr_ref,
                x1_ref, h2_ref, aff_ref, wo_ref):
    @pl.when(pl.program_id(0) == 0)
    def _cast_weights():
        wo_ref[...] = wo_f32_ref[...].astype(BF16)

    oa = jnp.concatenate([oa_ref[c] for c in range(oa_ref.shape[0])], axis=1)
    ob = jnp.concatenate([ob_ref[c] for c in range(ob_ref.shape[0])], axis=1)
    mixed = jnp.concatenate([_rms(oa, ga_ref[...]), _rms(ob, gb_ref[...])], axis=1).astype(BF16)
    for n in range(x_ref.shape[1] // MXU_COLS):
        cols = slice(n * MXU_COLS, (n + 1) * MXU_COLS)
        x1_ref[:, cols] = x_ref[:, cols] + jnp.dot(mixed, wo_ref[:, cols], preferred_element_type=F32)
    x1 = x1_ref[...]
    h2 = _rms(x1, gf_ref[...])
    tm = h2.shape[0]
    for c in range(h2.shape[1] // LANES):
        h2_ref[pl.ds(c, tm, stride=SUBLANES), :] = h2[:, c * LANES:(c + 1) * LANES]
    ne = aff_ref.shape[0]
    dn = (((1,), (1,)), ((), ()))
    h_hi = h2.astype(BF16)
    h_mid = (h2 - h_hi.astype(F32)).astype(BF16)
    r = lax.dot_general(wr_ref[...], h_hi, dn, preferred_element_type=F32)
    logits = r[:ne] + r[ne:] + lax.dot_general(wr_ref[:ne, :], h_mid, dn, preferred_element_type=F32)
    e = jnp.exp(logits - jnp.max(logits, axis=0, keepdims=True))
    aff_ref[...] = e / jnp.sum(e, axis=0, keepdims=True)


def _mix(x2, oa, ob, g_out_a, g_out_b, w_out, g_ffn, w_router, batch, seq):
    t, d = x2.shape
    tm = MIX_TM
    assert seq % tm == 0 and d % LANES == 0 and d // LANES == SUBLANES
    ns = seq // tm
    row = lambda i: (i, 0)
    const = lambda i: (0, 0)
    ne = w_router.shape[1]
    w_hi = w_router.T.astype(BF16)
    w_mid = (w_router.T - w_hi.astype(F32)).astype(BF16)
    wr = jnp.concatenate([w_hi, w_mid], axis=0)
    return pl.pallas_call(
        _mix_kernel,
        grid=(t // tm,),
        in_specs=[pl.BlockSpec((tm, d), row), pl.BlockSpec((A_Q // LANES, tm, LANES), lambda i: (0, i, 0)),
                  pl.BlockSpec((B_W // LANES, tm, LANES), lambda i: (0, i, 0)),
                  pl.BlockSpec((1, A_Q), const), pl.BlockSpec((1, B_W), const),
                  pl.BlockSpec(w_out.shape, const), pl.BlockSpec((1, d), const),
                  pl.BlockSpec((2 * ne, d), const)],
        out_specs=[pl.BlockSpec((tm, d), row), pl.BlockSpec((tm * SUBLANES, LANES), row),
                   pl.BlockSpec((None, ne, tm), lambda i: (i // ns, 0, i % ns))],
        out_shape=[jax.ShapeDtypeStruct((t, d), F32),
                   jax.ShapeDtypeStruct((t * SUBLANES, LANES), F32),
                   jax.ShapeDtypeStruct((batch, ne, seq), F32)],
        scratch_shapes=[pltpu.VMEM(w_out.shape, BF16)],
        compiler_params=_cparams(("arbitrary",), VMEM_LIMIT),
        name="mix",
    )(x2, oa, ob, g_out_a.reshape(1, -1), g_out_b.reshape(1, -1), w_out, g_ffn.reshape(1, d), wr)


def _lane_cumsum(mask_f, tri):
    rows, s = mask_f.shape
    carry = jnp.zeros((rows, 1), F32)
    out = []
    for c in range(s // LANES):
        inc = jnp.dot(mask_f[:, c * LANES:(c + 1) * LANES].astype(BF16), tri,
                      preferred_element_type=F32) + carry
        out.append(inc)
        carry = inc[:, LANES - 1:LANES]
    return jnp.concatenate(out, axis=1)


def _route_kernel(aff_ref, idx_e_ref, idx_l_ref, gate_ref, *, cap, seq):
    b = pl.program_id(0)
    aff = aff_ref[...]
    ne = aff.shape[0]

    def search(i, thr):
        lo = 28 - 2 * i
        for digit in (1, 2, 3):
            cand = thr | (jnp.int32(digit) << lo)
            cnt = jnp.sum((aff >= lax.bitcast_convert_type(cand, F32)).astype(F32), axis=1, keepdims=True)
            best = jnp.where(cnt >= cap, cand, thr if digit == 1 else best)
        return best

    thr = lax.bitcast_convert_type(lax.fori_loop(0, 15, search, jnp.zeros((ne, 1), jnp.int32)), F32)
    gt = aff > thr
    eq = aff == thr
    n_gt = jnp.sum(gt.astype(F32), axis=1, keepdims=True)
    ri = lax.broadcasted_iota(jnp.int32, (LANES, LANES), 0)
    ci = lax.broadcasted_iota(jnp.int32, (LANES, LANES), 1)
    tri = (ri <= ci).astype(BF16)
    eq_f = eq.astype(F32)
    eq_rank = _lane_cumsum(eq_f, tri) - eq_f
    sel = gt | (eq & (eq_rank < cap - n_gt))
    sel_f = sel.astype(F32)
    slot = _lane_cumsum(sel_f, tri) - sel_f
    key_t = jnp.where(sel, slot, -1.0).T
    tok = lax.broadcasted_iota(jnp.int32, (SUBLANES, seq), 1)
    piece = lax.broadcasted_iota(jnp.int32, (SUBLANES, seq), 0)
    tok_rows = jnp.where(piece == 0, tok // 64, jnp.where(piece == 1, tok % 64, 0)).astype(F32)
    p_iota = lax.broadcasted_iota(jnp.int32, (1, cap), 1).astype(F32)
    for e in range(ne):
        onehot = (key_t[:, e:e + 1] == p_iota).astype(BF16)
        g = aff[e:e + 1, :]
        g_hi = g.astype(BF16).astype(F32)
        g_mid = (g - g_hi).astype(BF16).astype(F32)
        g_lo = g - g_hi - g_mid
        lhs = jnp.where(piece == 2, g_hi, jnp.where(piece == 3, g_mid, jnp.where(piece == 4, g_lo, tok_rows)))
        res = jnp.dot(lhs.astype(BF16), onehot, preferred_element_type=F32)
        tile_row = (res[0:1] * 64.0 + res[1:2]).astype(jnp.int32) * SUBLANES
        idx_l_ref[e] = tile_row + (b % 2) * (seq * SUBLANES)
        idx_e_ref[e] = tile_row + b * (seq * SUBLANES)
        gate_ref[e] = res[2:3] + res[3:4] + res[4:5]


def _route(aff_t, cap):
    batch, ne, seq = aff_t.shape
    assert seq % LANES == 0 and seq <= 64 * 256
    return pl.pallas_call(
        functools.partial(_route_kernel, cap=cap, seq=seq),
        grid=(batch,),
        in_specs=[pl.BlockSpec((None, ne, seq), lambda b: (b, 0, 0))],
        out_specs=[pl.BlockSpec((ne, 1, cap), lambda b: (0, 0, b)), pl.BlockSpec((ne, 1, cap), lambda b: (b, 0, 0)),
                   pl.BlockSpec((ne, 1, cap), lambda b: (b, 0, 0))],
        out_shape=[jax.ShapeDtypeStruct((ne, 1, batch * cap), jnp.int32),
                   jax.ShapeDtypeStruct((batch * ne, 1, cap), jnp.int32),
                   jax.ShapeDtypeStruct((batch * ne, 1, cap), F32)],
        compiler_params=_cparams(("parallel",), VMEM_LIMIT),
        name="route",
    )(aff_t)


def _experts_kernel(idx_ref, idx_next_ref, h_hbm, wg_ref, wu_ref, wd_ref, y_ref, rows_ref, xe_ref, sems,
                    *, ne, nf):
    e, f = pl.program_id(0), pl.program_id(1)
    ntok, d = xe_ref.shape
    slot = e % 2
    chunks = ntok // FFN_TM
    per_chunk = ntok // (nf * chunks)

    def row_copy(idx_smem, i, buf):
        src = pl.multiple_of(idx_smem[0, 0, i], SUBLANES)
        dst = pl.multiple_of(i * SUBLANES, SUBLANES)
        return pltpu.make_async_copy(h_hbm.at[pl.ds(src, SUBLANES), :],
                                     rows_ref.at[buf, pl.ds(dst, SUBLANES), :], sems.at[buf])

    def wait_rows(buf):
        pltpu.make_async_copy(h_hbm.at[pl.ds(0, ntok * SUBLANES), :], rows_ref.at[buf], sems.at[buf]).wait()

    @pl.when((e == 0) & (f == 0))
    def _first_gather():
        def issue(i, carry):
            row_copy(idx_ref, i, 0).start()
            return carry

        lax.fori_loop(0, ntok, issue, 0, unroll=8)

    @pl.when(f == 0)
    def _repack():
        wait_rows(slot)
        for m in range(chunks):
            xe_ref[m * FFN_TM:(m + 1) * FFN_TM, :] = jnp.concatenate(
                [rows_ref[slot, pl.ds(m * FFN_TM * SUBLANES + c, FFN_TM, stride=SUBLANES), :]
                 for c in range(d // LANES)], axis=1).astype(BF16)
        y_ref[...] = jnp.zeros_like(y_ref)

    wg = wg_ref[...].astype(BF16)
    wu = wu_ref[...].astype(BF16)
    wd = wd_ref[...].astype(BF16)
    for m in range(chunks):
        xe = xe_ref[m * FFN_TM:(m + 1) * FFN_TM, :]
        hmid = []
        for n in range(wg.shape[1] // MXU_COLS):
            cols = slice(n * MXU_COLS, (n + 1) * MXU_COLS)
            a = jnp.dot(xe, wg[:, cols], preferred_element_type=F32)
            u = jnp.dot(xe, wu[:, cols], preferred_element_type=F32)
            hmid.append((a * jax.nn.sigmoid(a) * u).astype(BF16))
        hmid = jnp.concatenate(hmid, axis=1)
        y_ref[m * FFN_TM:(m + 1) * FFN_TM, :] += jnp.dot(hmid, wd, preferred_element_type=F32)
        for i in range(per_chunk):
            row_copy(idx_next_ref, (f * chunks + m) * per_chunk + i, 1 - slot).start()

    @pl.when((e == ne - 1) & (f == nf - 1))
    def _drain():
        wait_rows(1 - slot)


def _experts(idx_e, h2_rows, w_gate, w_up, w_down):
    ne, d, ff = w_gate.shape
    ntok = idx_e.shape[-1]
    tf = FFN_TF
    nf = ff // tf
    assert ff % tf == 0 and ntok % FFN_TM == 0 and ntok % (nf * (ntok // FFN_TM)) == 0 and d // LANES == SUBLANES
    return pl.pallas_call(
        functools.partial(_experts_kernel, ne=ne, nf=nf),
        grid=(ne, nf),
        in_specs=[pl.BlockSpec((1, 1, ntok), lambda e, f: (e, 0, 0), memory_space=pltpu.SMEM),
                  pl.BlockSpec((1, 1, ntok), lambda e, f: (jnp.minimum(e + 1, ne - 1), 0, 0),
                               memory_space=pltpu.SMEM),
                  pl.BlockSpec(memory_space=pl.ANY),
                  pl.BlockSpec((None, d, tf), lambda e, f: (e, 0, f)),
                  pl.BlockSpec((None, d, tf), lambda e, f: (e, 0, f)),
                  pl.BlockSpec((None, tf, d), lambda e, f: (e, f, 0))],
        out_specs=pl.BlockSpec((None, ntok, d), lambda e, f: (e, 0, 0)),
        out_shape=jax.ShapeDtypeStruct((ne, ntok, d), F32),
        scratch_shapes=[pltpu.VMEM((2, ntok * SUBLANES, LANES), F32), pltpu.VMEM((ntok, d), BF16),
                        pltpu.SemaphoreType.DMA((2,))],
        compiler_params=_cparams(("arbitrary", "arbitrary"), VMEM_LIMIT),
        name="experts",
    )(idx_e, idx_e, h2_rows, w_gate, w_up, w_down)


def _combine_kernel(*refs, batch, per_step):
    idx_refs, gate_refs, y_refs = (refs[i * per_step:(i + 1) * per_step] for i in range(3))
    x1_ref, g_ref, o_ref, acc_ref, ybuf_ref = refs[3 * per_step:]
    b, j = pl.program_id(0), pl.program_id(1)
    cap, d = y_refs[0].shape
    tm = x1_ref.shape[0]
    half_rows = acc_ref.shape[0] // 2
    cur = pl.multiple_of((b % 2) * half_rows, SUBLANES)
    prev = pl.multiple_of(((b + 1) % 2) * half_rows, SUBLANES)

    @pl.when(b < batch)
    def _scatter():
        @pl.when(j == 0)
        def _init():
            acc_ref[pl.ds(cur, half_rows), :] = jnp.zeros((half_rows, LANES), F32)

        for idx_ref, gate_ref, y_ref in zip(idx_refs, gate_refs, y_refs):
            gate_col = jnp.broadcast_to(gate_ref[0], (LANES, cap)).T
            for c in range(d // LANES):
                ybuf_ref[pl.ds(c, cap, stride=SUBLANES), :] = y_ref[:, c * LANES:(c + 1) * LANES] * gate_col

            def group(k, carry, idx_ref=idx_ref):
                rows, sums = [], []
                for u in range(RMW_UNROLL):
                    i = k * RMW_UNROLL + u
                    t = pl.ds(pl.multiple_of(idx_ref[0, 0, i], SUBLANES), SUBLANES)
                    yi = ybuf_ref[pl.ds(pl.multiple_of(i * SUBLANES, SUBLANES), SUBLANES), :]
                    rows.append(t)
                    sums.append(acc_ref[t, :] + yi)
                for t, s in zip(rows, sums):
                    acc_ref[t, :] = s
                return carry

            lax.fori_loop(0, cap // RMW_UNROLL, group, 0, unroll=RMW_GROUPS)

    @pl.when(b > 0)
    def _emit():
        base = prev + pl.multiple_of(j * tm * SUBLANES, SUBLANES)
        moe = jnp.concatenate([acc_ref[pl.ds(base + c, tm, stride=SUBLANES), :]
                               for c in range(d // LANES)], axis=1)
        o_ref[...] = _rms(x1_ref[...] + moe, g_ref[...])


def _combine(idx_l, gates, y, x1, g_final, batch, seq):
    ne = y.shape[0]
    t, d = x1.shape
    cap = idx_l.shape[-1]
    per_step = COMBINE_EXPERTS
    steps = ne // per_step
    tm = seq // steps
    assert cap % RMW_UNROLL == 0 and d // LANES == SUBLANES and tm * steps == seq and ne % per_step == 0
    scattered = lambda b: jnp.minimum(b, batch - 1)
    rows = lambda b, j: (jnp.maximum(b - 1, 0) * steps + jnp.where(b > 0, j, 0), 0)
    slot_specs = [pl.BlockSpec((1, 1, cap), functools.partial(
        lambda b, j, x: (scattered(b) * ne + j * per_step + x, 0, 0), x=x), memory_space=ms)
        for ms in (pltpu.SMEM, None) for x in range(per_step)]
    y_specs = [pl.BlockSpec((None, cap, d), functools.partial(
        lambda b, j, x: (j * per_step + x, scattered(b), 0), x=x)) for x in range(per_step)]
    return pl.pallas_call(
        functools.partial(_combine_kernel, batch=batch, per_step=per_step),
        grid=(batch + 1, steps),
        in_specs=slot_specs + y_specs + [pl.BlockSpec((tm, d), rows), pl.BlockSpec((1, d), lambda b, j: (0, 0))],
        out_specs=pl.BlockSpec((tm, d), rows),
        out_shape=jax.ShapeDtypeStruct((t, d), F32),
        scratch_shapes=[pltpu.VMEM((2 * seq * SUBLANES, LANES), F32), pltpu.VMEM((cap * SUBLANES, LANES), F32)],
        compiler_params=_cparams(("arbitrary", "arbitrary"), VMEM_LIMIT),
        name="combine",
    )(*([idx_l] * per_step + [gates] * per_step + [y] * per_step), x1, g_final.reshape(1, d))


def kernel(x, g_mix, w_in, a_sink, g_out_a, g_out_b, w_out, g_ffn, w_router, w_gate, w_up, w_down, g_final):
    batch, seq, d = x.shape
    depth = w_in.shape[0]
    ne = w_router.shape[-1]
    cap = CAPACITY_FACTOR * seq // ne
    assert depth == 1, "the last kernel fuses the MoE residual with the output norm of a single layer"
    l = 0
    x2 = x.reshape(batch * seq, d)
    qa, ka, va, qb, kb, vb = _proj(x2, g_mix[l], w_in[l], seq)
    oa = _attn_a(qa, ka, va, a_sink[l], batch, seq)
    ob = _attn_b(qb, kb, vb, batch, seq)
    x1, h2_rows, aff_t = _mix(x2, oa, ob, g_out_a[l], g_out_b[l], w_out[l], g_ffn[l], w_router[l], batch, seq)
    idx_e, idx_l, gates = _route(aff_t, cap)
    y = _experts(idx_e, h2_rows, w_gate[l], w_up[l], w_down[l])
    out = _combine(idx_l, gates, y, x1, g_final, batch, seq)
    return out.reshape(batch, seq, d)
```

```python
import functools

import jax
import jax.numpy as jnp
from jax import lax
from jax.experimental import pallas as pl
from jax.experimental.pallas import tpu as pltpu

F32 = jnp.float32
BF16 = jnp.bfloat16

LANES = 128
MXU_COLS = 256
SUBLANES = 8
BF16_ROWS = 16
VMEM_LIMIT = 56 * 1024 * 1024

HEAD_DIM = 64
A_HEADS = 8
A_KV_HEADS = 2
A_GROUP = A_HEADS // A_KV_HEADS
A_HALF_WINDOW = 128
B_HEADS = 8
DILATED_PATTERNS = ((128, 1), (512, 4), (2048, 16))
A_Q = A_HEADS * HEAD_DIM
A_KV = A_KV_HEADS * HEAD_DIM
B_W = B_HEADS * HEAD_DIM
ROPE_THETA = 500000.0
ROT_DIM = HEAD_DIM // 4
CAPACITY_FACTOR = 2
NORM_EPS = 1e-6
NEG_INF = -1e30
LOG2_E = 1.4426950408889634

PROJ_TM = 1024
ATT_QB = 128
ATT_A_SLOTS = 2
ATT_B_SLOTS = 4
MIX_TM = 1024
FFN_TF = 512
FFN_TM = 1024
RMW_UNROLL = 16
COMBINE_EXPERTS = 2


def _cparams(sem, vmem=None):
    return pltpu.CompilerParams(dimension_semantics=sem, vmem_limit_bytes=vmem)


def _rope_tables(seq_len):
    half = ROT_DIM // 2
    inv_freq = ROPE_THETA ** (-jnp.arange(0, ROT_DIM, 2, dtype=F32) / ROT_DIM)
    ang = jnp.arange(seq_len, dtype=F32)[:, None] * inv_freq[None, :]
    cos = jnp.tile(jnp.cos(ang), (1, LANES // half))
    sin = jnp.tile(jnp.sin(ang), (1, LANES // half))
    j = jnp.arange(LANES) % HEAD_DIM
    cos_t = jnp.where(j < ROT_DIM, cos, 1.0)
    sin_a = jnp.where(j < half, -sin, 0.0)
    sin_b = jnp.where((j >= half) & (j < ROT_DIM), sin, 0.0)
    return cos_t, sin_a, sin_b


def _rms(x, g):
    return x * lax.rsqrt(jnp.mean(x * x, axis=-1, keepdims=True) + NORM_EPS) * g


def _proj_kernel(x_ref, g_ref, w_ref, cos_ref, sa_ref, sb_ref,
                 qa_ref, ka_ref, va_ref, qb_ref, kb_ref, vb_ref, wbf_ref):
    @pl.when(pl.program_id(0) == 0)
    def _cast_weights():
        wbf_ref[...] = w_ref[...].astype(BF16)

    h = _rms(x_ref[...], g_ref[...]).astype(BF16)
    cos, sa, sb = cos_ref[...], sa_ref[...], sb_ref[...]
    half = ROT_DIM // 2

    def rope(t):
        return t * cos + pltpu.roll(t, LANES - half, 1) * sa + pltpu.roll(t, half, 1) * sb

    def blocks(col0, width):
        for n in range(width // MXU_COLS):
            c0 = col0 + n * MXU_COLS
            yield 2 * n, jnp.dot(h, wbf_ref[:, c0:c0 + MXU_COLS], preferred_element_type=F32)

    scale = HEAD_DIM ** -0.5 * LOG2_E
    for c, p in blocks(0, A_Q):
        for i in range(2):
            qa_ref[c + i] = (rope(p[:, i * LANES:(i + 1) * LANES]) * scale).astype(qa_ref.dtype)
    (_, kv), = blocks(A_Q, 2 * A_KV)
    first = lax.broadcasted_iota(jnp.int32, (kv.shape[0], LANES), 1) < HEAD_DIM
    for ref, t in ((ka_ref, rope(kv[:, :A_KV])), (va_ref, kv[:, A_KV:])):
        swapped = pltpu.roll(t, HEAD_DIM, 1)
        ref[0] = jnp.where(first, t, swapped).astype(ref.dtype)
        ref[1] = jnp.where(first, swapped, t).astype(ref.dtype)
    for c, p in blocks(A_Q + 2 * A_KV, B_W):
        for i in range(2):
            qb_ref[c + i] = rope(p[:, i * LANES:(i + 1) * LANES]) * scale
    for c, p in blocks(A_Q + 2 * A_KV + B_W, B_W):
        for i in range(2):
            kb_ref[c + i] = rope(p[:, i * LANES:(i + 1) * LANES])
    for c, p in blocks(A_Q + 2 * A_KV + 2 * B_W, B_W):
        for i in range(2):
            vb_ref[c + i] = p[:, i * LANES:(i + 1) * LANES]


def _proj(x2, g_mix, w_in, seq):
    t, d = x2.shape
    tm = PROJ_TM
    assert t % tm == 0 and seq % tm == 0 and A_KV == LANES and A_KV_HEADS == 2
    nseq = seq // tm
    tables = _rope_tables(seq)
    slab = lambda i: (0, i, 0)
    const = lambda i: (0, 0)
    tab = lambda i: (i % nseq, 0)
    widths = (A_Q, A_KV_HEADS * LANES, A_KV_HEADS * LANES, B_W, B_W, B_W)
    return pl.pallas_call(
        _proj_kernel,
        grid=(t // tm,),
        in_specs=[pl.BlockSpec((tm, d), lambda i: (i, 0)), pl.BlockSpec((1, d), const),
                  pl.BlockSpec(w_in.shape, const)] + [pl.BlockSpec((tm, LANES), tab)] * 3,
        out_specs=[pl.BlockSpec((w // LANES, tm, LANES), slab) for w in widths],
        out_shape=[jax.ShapeDtypeStruct((w // LANES, t, LANES), dt)
                   for w, dt in zip(widths, (BF16, BF16, BF16, F32, F32, F32))],
        scratch_shapes=[pltpu.VMEM(w_in.shape, BF16)],
        compiler_params=_cparams(("arbitrary",), VMEM_LIMIT),
        name="proj",
    )(x2, g_mix.reshape(1, d), w_in, *tables)


def _band_bias(bias_ref, qb, kw, hw):
    kk = lax.broadcasted_iota(jnp.int32, (kw, qb), 0)
    qq = lax.broadcasted_iota(jnp.int32, (kw, qb), 1)
    for t, delta in enumerate((0, -hw, -2 * hw)):
        bias_ref[t] = jnp.where(jnp.abs(kk + delta - qq) <= hw, 0.0, NEG_INF)


def _band_scratch(qb, kw, slots):
    n = 2 * slots
    return ([pltpu.VMEM((kw, 2 * qb), F32)] * n + [pltpu.VMEM((kw, 2 * qb), BF16)] * n
            + [pltpu.VMEM((SUBLANES, 2 * qb), F32)] * n + [pltpu.VMEM((LANES, kw), BF16)] * n
            + [pltpu.VMEM((SUBLANES, 2 * qb), F32)] * n)


def _band_pipeline(q_ref, k_ref, v_ref, bias_ref, o_ref, lse_ref, bufs, *, nblocks, geometry, stage_vt,
                   sink_row=None):
    qb = ATT_QB
    n = len(bufs) // 5
    slots = n // 2
    s_bufs, p_bufs, ml_bufs, vt_bufs, mx_bufs = (bufs[i * n:(i + 1) * n] for i in range(5))
    kw = s_bufs[0].shape[0]

    def rows(block):
        q0, k0, kind = geometry(block)
        return pl.ds(q0, qb), pl.ds(k0, kw), kind

    def scores(block, s_buf, mx_buf):
        q_rows, k_rows, kind = rows(block)
        q = q_ref[q_rows, :]
        lo = lax.broadcasted_iota(jnp.int32, q.shape, 1) < HEAD_DIM
        q2 = jnp.concatenate([jnp.where(lo, q, 0.0), jnp.where(lo, 0.0, q)], axis=0).astype(BF16)
        s = lax.dot_general(k_ref[k_rows, :].astype(BF16), q2, (((1,), (1,)), ((), ())),
                            preferred_element_type=F32)
        bias = bias_ref[kind]
        for hh in range(2):
            s_h = s[:, hh * qb:(hh + 1) * qb] + bias
            s_buf[:, hh * qb:(hh + 1) * qb] = s_h
            mx_buf[0:1, hh * qb:(hh + 1) * qb] = jnp.max(s_h, axis=0, keepdims=True)

    def softmax(block, s_buf, mx_buf, p_buf, ml_buf, vt_buf):
        if stage_vt:
            _, k_rows, _ = rows(block)
            vt_buf[...] = v_ref[k_rows, :].astype(F32).T.astype(BF16)
        m = mx_buf[0:1, :]
        if sink_row is not None:
            m = jnp.maximum(m, sink_row)
        p = jnp.exp2(s_buf[...] - m)
        l = jnp.sum(p, axis=0, keepdims=True)
        if sink_row is not None:
            l = l + jnp.exp2(sink_row - m)
        p_buf[...] = p.astype(BF16)
        ml_buf[0:1, :] = m
        ml_buf[1:2, :] = l

    def output(block, p_buf, ml_buf, vt_buf):
        q_rows, k_rows, _ = rows(block)
        if stage_vt:
            o_t = jnp.dot(vt_buf[...], p_buf[...], preferred_element_type=F32)
        else:
            o_t = lax.dot_general(v_ref[k_rows, :].astype(BF16), p_buf[...], (((0,), (0,)), ((), ())),
                                  preferred_element_type=F32)
        o_tc = jnp.concatenate([o_t[:HEAD_DIM, :qb], o_t[HEAD_DIM:, qb:]], axis=0)
        top = lax.broadcasted_iota(jnp.int32, (LANES, qb), 0) < HEAD_DIM
        m, l = ml_buf[0:1, :], ml_buf[1:2, :]
        inv = 1.0 / l
        o_ref[q_rows, :] = (o_tc * jnp.where(top, inv[:, :qb], inv[:, qb:])).T
        if lse_ref is not None:
            lse = m + jnp.log2(l)
            lse_ref[q_rows, :] = jnp.where(top, lse[:, :qb], lse[:, qb:]).T

    assert nblocks % slots == 0
    for trip in range(nblocks // slots + 2):
        for slot in range(slots):
            t = trip * slots + slot
            rd, wr = (1 - trip % 2) * slots + slot, (trip % 2) * slots + slot
            if 0 <= t - 2 * slots:
                output(t - 2 * slots, p_bufs[rd], ml_bufs[rd], vt_bufs[rd])
            if 0 <= t - slots < nblocks:
                softmax(t - slots, s_bufs[rd], mx_bufs[rd], p_bufs[wr], ml_bufs[wr], vt_bufs[wr])
            if t < nblocks:
                scores(t, s_bufs[wr], mx_bufs[wr])


def _segment_geometry(block, n, hw, base=0):
    qb = ATT_QB
    nblk = n // qb
    seg, jb = divmod(block, nblk)
    qs = jb * qb
    ks = min(max(qs - hw, 0), n - (qb + 2 * hw))
    kind = 0 if jb == 0 else 2 if jb == nblk - 1 else 1
    return base + seg * n + qs, base + seg * n + ks, kind


def _attn_a_kernel(sink_ref, q_ref, k_ref, v_ref, o_ref, bias_ref, *bufs, seq):
    j = pl.program_id(1)
    qb, hw = ATT_QB, A_HALF_WINDOW
    _band_bias(bias_ref, qb, qb + 2 * hw, hw)
    col = lax.broadcasted_iota(jnp.int32, (1, 2 * qb), 1)
    sink_row = jnp.where(col < qb, sink_ref[2 * j], sink_ref[2 * j + 1]) * LOG2_E
    _band_pipeline(q_ref, k_ref, v_ref, bias_ref, o_ref, None, bufs, nblocks=seq // qb,
                   geometry=functools.partial(_segment_geometry, n=seq, hw=hw), stage_vt=False,
                   sink_row=sink_row)


def _attn_a(qa, ka, va, sink, batch, seq):
    qb, hw = ATT_QB, A_HALF_WINDOW
    assert seq % qb == 0 and seq >= qb + 2 * hw and A_KV == LANES
    qspec = pl.BlockSpec((None, seq, LANES), lambda b, j: (j, b, 0))
    per_kv = A_GROUP * HEAD_DIM // LANES
    kvspec = pl.BlockSpec((None, seq, LANES), lambda b, j: (j // per_kv, b, 0))
    return pl.pallas_call(
        functools.partial(_attn_a_kernel, seq=seq),
        grid=(batch, A_Q // LANES),
        in_specs=[pl.BlockSpec(memory_space=pltpu.SMEM), qspec, kvspec, kvspec],
        out_specs=qspec,
        out_shape=jax.ShapeDtypeStruct(qa.shape, F32),
        scratch_shapes=[pltpu.VMEM((3, qb + 2 * hw, qb), F32)] + _band_scratch(qb, qb + 2 * hw, ATT_A_SLOTS),
        compiler_params=_cparams(("parallel", "parallel"), VMEM_LIMIT),
        name="attn_a",
    )(sink, qa, ka, va)


def _attn_b_kernel(q_ref, k_ref, v_ref, out_ref, bias_ref, tmp_ref, qs_ref, ks_ref, vs_ref, os_ref, ls_ref,
                   *bufs, seq):
    (w1, d1), (w4, d4), (w16, d16) = DILATED_PATTERNS
    hw = w1 // (2 * d1)
    qb = ATT_QB
    _band_bias(bias_ref, qb, qb + 2 * hw, hw)
    n4, n16 = seq // d4, seq // d16
    step = d16 // d4
    chunk = 512
    for src, dst in ((q_ref, qs_ref), (k_ref, ks_ref), (v_ref, vs_ref)):
        for c in range(seq // chunk):
            dst[c * chunk:(c + 1) * chunk, :] = src[c * chunk:(c + 1) * chunk, :].astype(dst.dtype)
        for r in range(d4):
            tmp_ref[r * n4:(r + 1) * n4, :] = src[pl.ds(r, n4, stride=d4), :]
        for c in range(seq // chunk):
            dst[seq + c * chunk:seq + (c + 1) * chunk, :] = tmp_ref[c * chunk:(c + 1) * chunk, :].astype(dst.dtype)
        for r in range(d16):
            dst[2 * seq + r * n16:2 * seq + (r + 1) * n16, :] = tmp_ref[
                pl.ds((r % d4) * n4 + r // d4, n16, stride=step), :].astype(dst.dtype)

    bps = seq // qb

    def geometry(block):
        pat, within = divmod(block, bps)
        return _segment_geometry(within, seq // DILATED_PATTERNS[pat][1], hw, base=pat * seq)

    _band_pipeline(qs_ref, ks_ref, vs_ref, bias_ref, os_ref, ls_ref, bufs,
                   nblocks=len(DILATED_PATTERNS) * bps, geometry=geometry, stage_vt=True)

    def merge(r, carry):
        rows16 = pl.ds(pl.multiple_of(2 * seq + r * n16, SUBLANES), n16)
        rows4 = pl.ds(seq + (r % d4) * n4 + r // d4, n16, stride=step)
        rows1 = pl.ds(r, n16, stride=d16)
        la, lb, lc = ls_ref[rows1, :], ls_ref[rows4, :], ls_ref[rows16, :]
        top = jnp.maximum(jnp.maximum(la, lb), lc)
        wa, wb, wc = jnp.exp2(la - top), jnp.exp2(lb - top), jnp.exp2(lc - top)
        num = wa * os_ref[rows1, :] + wb * os_ref[rows4, :] + wc * os_ref[rows16, :]
        out_ref[rows1, :] = num * (1.0 / (wa + wb + wc))
        return carry

    lax.fori_loop(0, d16, merge, 0, unroll=4)


def _attn_b(qb_, kb, vb, batch, seq):
    (w1, d1), (w4, d4), (w16, d16) = DILATED_PATTERNS
    hw = w1 // (2 * d1)
    qb = ATT_QB
    assert d1 == 1 and d16 % d4 == 0 and w4 // (2 * d4) == hw and w16 // (2 * d16) == hw
    assert seq % 512 == 0 and seq // d16 >= qb + 2 * hw
    for (_, d) in DILATED_PATTERNS:
        nblk = seq // d // qb
        assert nblk * d * qb == seq and nblk & (nblk - 1) == 0
    spec = pl.BlockSpec((None, seq, LANES), lambda b, p: (p, b, 0))
    npat = len(DILATED_PATTERNS)
    return pl.pallas_call(
        functools.partial(_attn_b_kernel, seq=seq),
        grid=(batch, B_W // LANES),
        in_specs=[spec] * 3,
        out_specs=spec,
        out_shape=jax.ShapeDtypeStruct(qb_.shape, F32),
        scratch_shapes=[pltpu.VMEM((3, qb + 2 * hw, qb), F32), pltpu.VMEM((seq, LANES), F32),
                        pltpu.VMEM((npat * seq, LANES), F32), pltpu.VMEM((npat * seq, LANES), BF16),
                        pltpu.VMEM((npat * seq, LANES), BF16), pltpu.VMEM((npat * seq, LANES), F32),
                        pltpu.VMEM((npat * seq, LANES), F32)] + _band_scratch(qb, qb + 2 * hw, ATT_B_SLOTS),
        compiler_params=_cparams(("parallel", "parallel"), VMEM_LIMIT),
        name="attn_b",
    )(qb_, kb, vb)


def _mix_kernel(x_ref, oa_ref, ob_ref, ga_ref, gb_ref, wo_f32_ref, gf_ref, wr_ref,
                x1_ref, h2_ref, aff_ref, wo_ref):
    @pl.when(pl.program_id(0) == 0)
    def _cast_weights():
        wo_ref[...] = wo_f32_ref[...].astype(BF16)

    oa = jnp.concatenate([oa_ref[c] for c in range(oa_ref.shape[0])], axis=1)
    ob = jnp.concatenate([ob_ref[c] for c in range(ob_ref.shape[0])], axis=1)
    mixed = jnp.concatenate([_rms(oa, ga_ref[...]), _rms(ob, gb_ref[...])], axis=1).astype(BF16)
    for n in range(x_ref.shape[1] // MXU_COLS):
        cols = slice(n * MXU_COLS, (n + 1) * MXU_COLS)
        x1_ref[:, cols] = x_ref[:, cols] + jnp.dot(mixed, wo_ref[:, cols], preferred_element_type=F32)
    x1 = x1_ref[...]
    h2 = _rms(x1, gf_ref[...])
    tm = h2.shape[0]
    for c in range(h2.shape[1] // LANES):
        h2_ref[pl.ds(c, tm, stride=SUBLANES), :] = h2[:, c * LANES:(c + 1) * LANES]
    ne = aff_ref.shape[0]
    dn = (((1,), (1,)), ((), ()))
    h_hi = h2.astype(BF16)
    h_mid = (h2 - h_hi.astype(F32)).astype(BF16)
    r = lax.dot_general(wr_ref[...], h_hi, dn, preferred_element_type=F32)
    logits = r[:ne] + r[ne:] + lax.dot_general(wr_ref[:ne, :], h_mid, dn, preferred_element_type=F32)
    e = jnp.exp(logits - jnp.max(logits, axis=0, keepdims=True))
    aff_ref[...] = e / jnp.sum(e, axis=0, keepdims=True)


def _mix(x2, oa, ob, g_out_a, g_out_b, w_out, g_ffn, w_router, batch, seq):
    t, d = x2.shape
    tm = MIX_TM
    assert seq % tm == 0 and d % LANES == 0 and d // LANES == SUBLANES
    ns = seq // tm
    row = lambda i: (i, 0)
    const = lambda i: (0, 0)
    ne = w_router.shape[1]
    w_hi = w_router.T.astype(BF16)
    w_mid = (w_router.T - w_hi.astype(F32)).astype(BF16)
    wr = jnp.concatenate([w_hi, w_mid], axis=0)
    return pl.pallas_call(
        _mix_kernel,
        grid=(t // tm,),
        in_specs=[pl.BlockSpec((tm, d), row), pl.BlockSpec((A_Q // LANES, tm, LANES), lambda i: (0, i, 0)),
                  pl.BlockSpec((B_W // LANES, tm, LANES), lambda i: (0, i, 0)),
                  pl.BlockSpec((1, A_Q), const), pl.BlockSpec((1, B_W), const),
                  pl.BlockSpec(w_out.shape, const), pl.BlockSpec((1, d), const),
                  pl.BlockSpec((2 * ne, d), const)],
        out_specs=[pl.BlockSpec((tm, d), row), pl.BlockSpec((tm * SUBLANES, LANES), row),
                   pl.BlockSpec((None, ne, tm), lambda i: (i // ns, 0, i % ns))],
        out_shape=[jax.ShapeDtypeStruct((t, d), F32),
                   jax.ShapeDtypeStruct((t * SUBLANES, LANES), F32),
                   jax.ShapeDtypeStruct((batch, ne, seq), F32)],
        scratch_shapes=[pltpu.VMEM(w_out.shape, BF16)],
        compiler_params=_cparams(("arbitrary",), VMEM_LIMIT),
        name="mix",
    )(x2, oa, ob, g_out_a.reshape(1, -1), g_out_b.reshape(1, -1), w_out, g_ffn.reshape(1, d), wr)


def _lane_cumsum(mask_f, tri):
    rows, s = mask_f.shape
    carry = jnp.zeros((rows, 1), F32)
    out = []
    for c in range(s // LANES):
        inc = jnp.dot(mask_f[:, c * LANES:(c + 1) * LANES].astype(BF16), tri,
                      preferred_element_type=F32) + carry
        out.append(inc)
        carry = inc[:, LANES - 1:LANES]
    return jnp.concatenate(out, axis=1)


def _route_kernel(aff_ref, idx_e_ref, idx_l_ref, gate_ref, *, cap, seq):
    b = pl.program_id(0)
    aff = aff_ref[...]
    ne = aff.shape[0]

    def search(i, thr):
        lo = 28 - 2 * i
        for digit in (1, 2, 3):
            cand = thr | (jnp.int32(digit) << lo)
            cnt = jnp.sum((aff >= lax.bitcast_convert_type(cand, F32)).astype(F32), axis=1, keepdims=True)
            best = jnp.where(cnt >= cap, cand, thr if digit == 1 else best)
        return best

    thr = lax.bitcast_convert_type(lax.fori_loop(0, 15, search, jnp.zeros((ne, 1), jnp.int32)), F32)
    gt = aff > thr
    eq = aff == thr
    n_gt = jnp.sum(gt.astype(F32), axis=1, keepdims=True)
    ri = lax.broadcasted_iota(jnp.int32, (LANES, LANES), 0)
    ci = lax.broadcasted_iota(jnp.int32, (LANES, LANES), 1)
    tri = (ri <= ci).astype(BF16)
    eq_f = eq.astype(F32)
    eq_rank = _lane_cumsum(eq_f, tri) - eq_f
    sel = gt | (eq & (eq_rank < cap - n_gt))
    sel_f = sel.astype(F32)
    slot = _lane_cumsum(sel_f, tri) - sel_f
    key_t = jnp.where(sel, slot, -1.0).T
    tok = lax.broadcasted_iota(jnp.int32, (SUBLANES, seq), 1)
    piece = lax.broadcasted_iota(jnp.int32, (SUBLANES, seq), 0)
    tok_rows = jnp.where(piece == 0, tok // 64, jnp.where(piece == 1, tok % 64, 0)).astype(F32)
    p_iota = lax.broadcasted_iota(jnp.int32, (1, cap), 1).astype(F32)
    for e in range(ne):
        onehot = (key_t[:, e:e + 1] == p_iota).astype(BF16)
        g = aff[e:e + 1, :]
        g_hi = g.astype(BF16).astype(F32)
        g_mid = (g - g_hi).astype(BF16).astype(F32)
        g_lo = g - g_hi - g_mid
        lhs = jnp.where(piece == 2, g_hi, jnp.where(piece == 3, g_mid, jnp.where(piece == 4, g_lo, tok_rows)))
        res = jnp.dot(lhs.astype(BF16), onehot, preferred_element_type=F32)
        tile_row = (res[0:1] * 64.0 + res[1:2]).astype(jnp.int32) * SUBLANES
        idx_l_ref[e] = tile_row + (b % 2) * (seq * SUBLANES)
        idx_e_ref[e] = tile_row + b * (seq * SUBLANES)
        gate_ref[e] = res[2:3] + res[3:4] + res[4:5]


def _route(aff_t, cap):
    batch, ne, seq = aff_t.shape
    assert seq % LANES == 0 and seq <= 64 * 256
    return pl.pallas_call(
        functools.partial(_route_kernel, cap=cap, seq=seq),
        grid=(batch,),
        in_specs=[pl.BlockSpec((None, ne, seq), lambda b: (b, 0, 0))],
        out_specs=[pl.BlockSpec((ne, 1, cap), lambda b: (0, 0, b)), pl.BlockSpec((ne, 1, cap), lambda b: (b, 0, 0)),
                   pl.BlockSpec((ne, 1, cap), lambda b: (b, 0, 0))],
        out_shape=[jax.ShapeDtypeStruct((ne, 1, batch * cap), jnp.int32),
                   jax.ShapeDtypeStruct((batch * ne, 1, cap), jnp.int32),
                   jax.ShapeDtypeStruct((batch * ne, 1, cap), F32)],
        compiler_params=_cparams(("parallel",), VMEM_LIMIT),
        name="route",
    )(aff_t)


def _experts_kernel(idx_ref, idx_next_ref, h_hbm, wg_ref, wu_ref, wd_ref, y_ref, rows_ref, *rest, ne, nf):
    xe_refs, sems = rest[:-1], rest[-1]
    e, f = pl.program_id(0), pl.program_id(1)
    chunks = len(xe_refs)
    d = xe_refs[0].shape[1]
    ntok = chunks * FFN_TM
    slot = e % 2
    per_chunk = ntok // (nf * chunks)

    def row_copy(idx_smem, i, buf):
        src = pl.multiple_of(idx_smem[0, 0, i], SUBLANES)
        dst = pl.multiple_of(i * SUBLANES, SUBLANES)
        return pltpu.make_async_copy(h_hbm.at[pl.ds(src, SUBLANES), :],
                                     rows_ref.at[buf, pl.ds(dst, SUBLANES), :], sems.at[buf])

    def wait_rows(buf):
        pltpu.make_async_copy(h_hbm.at[pl.ds(0, ntok * SUBLANES), :], rows_ref.at[buf], sems.at[buf]).wait()

    @pl.when((e == 0) & (f == 0))
    def _first_gather():
        def issue(i, carry):
            row_copy(idx_ref, i, 0).start()
            return carry

        lax.fori_loop(0, ntok, issue, 0, unroll=8)

    def body(first):
        wg = wg_ref[...].astype(BF16)
        wu = wu_ref[...].astype(BF16)
        wd = wd_ref[...].astype(BF16)
        for m in range(chunks):
            if first:
                xe_refs[m][...] = jnp.concatenate(
                    [rows_ref[slot, pl.ds(m * FFN_TM * SUBLANES + c, FFN_TM, stride=SUBLANES), :]
                     for c in range(d // LANES)], axis=1).astype(BF16)
            xe = xe_refs[m][...]
            hmid = []
            for n in range(wg.shape[1] // MXU_COLS):
                cols = slice(n * MXU_COLS, (n + 1) * MXU_COLS)
                a = jnp.dot(xe, wg[:, cols], preferred_element_type=F32)
                u = jnp.dot(xe, wu[:, cols], preferred_element_type=F32)
                hmid.append((a * jax.nn.sigmoid(a) * u).astype(BF16))
            part = jnp.dot(jnp.concatenate(hmid, axis=1), wd, preferred_element_type=F32)
            rows = slice(m * FFN_TM, (m + 1) * FFN_TM)
            y_ref[rows, :] = part if first else y_ref[rows, :] + part
            for i in range(per_chunk):
                row_copy(idx_next_ref, (f * chunks + m) * per_chunk + i, 1 - slot).start()

    @pl.when(f == 0)
    def _first_step():
        wait_rows(slot)
        body(True)

    @pl.when(f > 0)
    def _later_steps():
        body(False)

    @pl.when((e == ne - 1) & (f == nf - 1))
    def _drain():
        wait_rows(1 - slot)


def _experts(idx_e, h2_rows, w_gate, w_up, w_down):
    ne, d, ff = w_gate.shape
    ntok = idx_e.shape[-1]
    tf = FFN_TF
    nf = ff // tf
    assert ff % tf == 0 and ntok % FFN_TM == 0 and ntok % (nf * (ntok // FFN_TM)) == 0 and d // LANES == SUBLANES
    return pl.pallas_call(
        functools.partial(_experts_kernel, ne=ne, nf=nf),
        grid=(ne, nf),
        in_specs=[pl.BlockSpec((1, 1, ntok), lambda e, f: (e, 0, 0), memory_space=pltpu.SMEM),
                  pl.BlockSpec((1, 1, ntok), lambda e, f: (jnp.minimum(e + 1, ne - 1), 0, 0),
                               memory_space=pltpu.SMEM),
                  pl.BlockSpec(memory_space=pl.ANY),
                  pl.BlockSpec((None, d, tf), lambda e, f: (e, 0, f)),
                  pl.BlockSpec((None, d, tf), lambda e, f: (e, 0, f)),
                  pl.BlockSpec((None, tf, d), lambda e, f: (e, f, 0))],
        out_specs=pl.BlockSpec((None, ntok, d), lambda e, f: (e, 0, 0)),
        out_shape=jax.ShapeDtypeStruct((ne, ntok, d), F32),
        scratch_shapes=[pltpu.VMEM((2, ntok * SUBLANES, LANES), F32)]
        + [pltpu.VMEM((FFN_TM, d), BF16)] * (ntok // FFN_TM) + [pltpu.SemaphoreType.DMA((2,))],
        compiler_params=_cparams(("arbitrary", "arbitrary"), VMEM_LIMIT),
        name="experts",
    )(idx_e, idx_e, h2_rows, w_gate, w_up, w_down)


def _combine_kernel(*refs, batch, per_step):
    idx_refs, gate_refs, y_refs = (refs[i * per_step:(i + 1) * per_step] for i in range(3))
    x1_ref, g_ref, o_ref, acc_ref, ybuf_ref = refs[3 * per_step:]
    b, j = pl.program_id(0), pl.program_id(1)
    cap, d = y_refs[0].shape
    tm = x1_ref.shape[0]
    half_rows = acc_ref.shape[0] // 2
    cur = pl.multiple_of((b % 2) * half_rows, SUBLANES)
    prev = pl.multiple_of(((b + 1) % 2) * half_rows, SUBLANES)

    @pl.when(b < batch)
    def _scatter():
        @pl.when(j == 0)
        def _init():
            acc_ref[pl.ds(cur, half_rows), :] = jnp.zeros((half_rows, LANES), F32)

        for idx_ref, gate_ref, y_ref in zip(idx_refs, gate_refs, y_refs):
            gate_col = jnp.broadcast_to(gate_ref[0], (LANES, cap)).T
            for c in range(d // LANES):
                ybuf_ref[pl.ds(c, cap, stride=SUBLANES), :] = y_ref[:, c * LANES:(c + 1) * LANES] * gate_col

            def group(k, carry, idx_ref=idx_ref):
                rows, sums = [], []
                for u in range(RMW_UNROLL):
                    i = k * RMW_UNROLL + u
                    t = pl.ds(pl.multiple_of(idx_ref[0, 0, i], SUBLANES), SUBLANES)
                    yi = ybuf_ref[pl.ds(pl.multiple_of(i * SUBLANES, SUBLANES), SUBLANES), :]
                    rows.append(t)
                    sums.append(acc_ref[t, :] + yi)
                for t, s in zip(rows, sums):
                    acc_ref[t, :] = s
                return carry

            lax.fori_loop(0, cap // RMW_UNROLL, group, 0, unroll=True)

    @pl.when(b > 0)
    def _emit():
        base = prev + pl.multiple_of(j * tm * SUBLANES, SUBLANES)
        moe = jnp.concatenate([acc_ref[pl.ds(base + c, tm, stride=SUBLANES), :]
                               for c in range(d // LANES)], axis=1)
        o_ref[...] = _rms(x1_ref[...] + moe, g_ref[...])


def _combine(idx_l, gates, y, x1, g_final, batch, seq):
    ne = y.shape[0]
    t, d = x1.shape
    cap = idx_l.shape[-1]
    per_step = COMBINE_EXPERTS
    steps = ne // per_step
    tm = seq // steps
    assert cap % RMW_UNROLL == 0 and d // LANES == SUBLANES and tm * steps == seq and ne % per_step == 0
    scattered = lambda b: jnp.minimum(b, batch - 1)
    rows = lambda b, j: (jnp.maximum(b - 1, 0) * steps + jnp.where(b > 0, j, 0), 0)
    slot_specs = [pl.BlockSpec((1, 1, cap), functools.partial(
        lambda b, j, x: (scattered(b) * ne + j * per_step + x, 0, 0), x=x), memory_space=ms)
        for ms in (pltpu.SMEM, None) for x in range(per_step)]
    y_specs = [pl.BlockSpec((None, cap, d), functools.partial(
        lambda b, j, x: (j * per_step + x, scattered(b), 0), x=x)) for x in range(per_step)]
    return pl.pallas_call(
        functools.partial(_combine_kernel, batch=batch, per_step=per_step),
        grid=(batch + 1, steps),
        in_specs=slot_specs + y_specs + [pl.BlockSpec((tm, d), rows), pl.BlockSpec((1, d), lambda b, j: (0, 0))],
        out_specs=pl.BlockSpec((tm, d), rows),
        out_shape=jax.ShapeDtypeStruct((t, d), F32),
        scratch_shapes=[pltpu.VMEM((2 * seq * SUBLANES, LANES), F32), pltpu.VMEM((cap * SUBLANES, LANES), F32)],
        compiler_params=_cparams(("arbitrary", "arbitrary"), VMEM_LIMIT),
        name="combine",
    )(*([idx_l] * per_step + [gates] * per_step + [y] * per_step), x1, g_final.reshape(1, d))


def kernel(x, g_mix, w_in, a_sink, g_out_a, g_out_b, w_out, g_ffn, w_router, w_gate, w_up, w_down, g_final):
    batch, seq, d = x.shape
    depth = w_in.shape[0]
    ne = w_router.shape[-1]
    cap = CAPACITY_FACTOR * seq // ne
    assert depth == 1, "the last kernel fuses the MoE residual with the output norm of a single layer"
    l = 0
    x2 = x.reshape(batch * seq, d)
    qa, ka, va, qb, kb, vb = _proj(x2, g_mix[l], w_in[l], seq)
    oa = _attn_a(qa, ka, va, a_sink[l], batch, seq)
    ob = _attn_b(qb, kb, vb, batch, seq)
    x1, h2_rows, aff_t = _mix(x2, oa, ob, g_out_a[l], g_out_b[l], w_out[l], g_ffn[l], w_router[l], batch, seq)
    idx_e, idx_l, gates = _route(aff_t, cap)
    y = _experts(idx_e, h2_rows, w_gate[l], w_up[l], w_down[l])
    out = _combine(idx_l, gates, y, x1, g_final, batch, seq)
    return out.reshape(batch, seq, d)
```

```python
import functools

import jax
import jax.numpy as jnp
from jax import lax
from jax.experimental import pallas as pl
from jax.experimental.pallas import tpu as pltpu

F32 = jnp.float32
BF16 = jnp.bfloat16

LANES = 128
MXU_COLS = 256
SUBLANES = 8
BF16_ROWS = 16
VMEM_LIMIT = 56 * 1024 * 1024

HEAD_DIM = 64
A_HEADS = 8
A_KV_HEADS = 2
A_GROUP = A_HEADS // A_KV_HEADS
A_HALF_WINDOW = 128
B_HEADS = 8
DILATED_PATTERNS = ((128, 1), (512, 4), (2048, 16))
A_Q = A_HEADS * HEAD_DIM
A_KV = A_KV_HEADS * HEAD_DIM
B_W = B_HEADS * HEAD_DIM
ROPE_THETA = 500000.0
ROT_DIM = HEAD_DIM // 4
CAPACITY_FACTOR = 2
NORM_EPS = 1e-6
NEG_INF = -1e30
LOG2_E = 1.4426950408889634

PROJ_TM = 1024
ATT_QB = 128
ATT_A_SLOTS = 2
ATT_B_SLOTS = 3
MIX_TM = 1024
FFN_TF = 512
FFN_TM = 1024
RMW_UNROLL = 16
COMBINE_EXPERTS = 2


def _cparams(sem, vmem=None):
    return pltpu.CompilerParams(dimension_semantics=sem, vmem_limit_bytes=vmem)


def _rope_tables(seq_len):
    half = ROT_DIM // 2
    inv_freq = ROPE_THETA ** (-jnp.arange(0, ROT_DIM, 2, dtype=F32) / ROT_DIM)
    ang = jnp.arange(seq_len, dtype=F32)[:, None] * inv_freq[None, :]
    cos = jnp.tile(jnp.cos(ang), (1, LANES // half))
    sin = jnp.tile(jnp.sin(ang), (1, LANES // half))
    j = jnp.arange(LANES) % HEAD_DIM
    cos_t = jnp.where(j < ROT_DIM, cos, 1.0)
    sin_a = jnp.where(j < half, -sin, 0.0)
    sin_b = jnp.where((j >= half) & (j < ROT_DIM), sin, 0.0)
    return cos_t, sin_a, sin_b


def _rms(x, g):
    return x * lax.rsqrt(jnp.mean(x * x, axis=-1, keepdims=True) + NORM_EPS) * g


def _proj_kernel(x_ref, g_ref, w_ref, cos_ref, sa_ref, sb_ref,
                 qa_ref, ka_ref, va_ref, qb_ref, kb_ref, vb_ref, wbf_ref):
    @pl.when(pl.program_id(0) == 0)
    def _cast_weights():
        wbf_ref[...] = w_ref[...].astype(BF16)

    h = _rms(x_ref[...], g_ref[...]).astype(BF16)
    cos, sa, sb = cos_ref[...], sa_ref[...], sb_ref[...]
    half = ROT_DIM // 2

    def rope(t):
        return t * cos + pltpu.roll(t, LANES - half, 1) * sa + pltpu.roll(t, half, 1) * sb

    def blocks(col0, width):
        for n in range(width // MXU_COLS):
            c0 = col0 + n * MXU_COLS
            yield 2 * n, jnp.dot(h, wbf_ref[:, c0:c0 + MXU_COLS], preferred_element_type=F32)

    scale = HEAD_DIM ** -0.5 * LOG2_E
    for c, p in blocks(0, A_Q):
        for i in range(2):
            qa_ref[c + i] = (rope(p[:, i * LANES:(i + 1) * LANES]) * scale).astype(qa_ref.dtype)
    (_, kv), = blocks(A_Q, 2 * A_KV)
    first = lax.broadcasted_iota(jnp.int32, (kv.shape[0], LANES), 1) < HEAD_DIM
    for ref, t in ((ka_ref, rope(kv[:, :A_KV])), (va_ref, kv[:, A_KV:])):
        swapped = pltpu.roll(t, HEAD_DIM, 1)
        ref[0] = jnp.where(first, t, swapped).astype(ref.dtype)
        ref[1] = jnp.where(first, swapped, t).astype(ref.dtype)
    for c, p in blocks(A_Q + 2 * A_KV, B_W):
        for i in range(2):
            qb_ref[c + i] = rope(p[:, i * LANES:(i + 1) * LANES]) * scale
    for c, p in blocks(A_Q + 2 * A_KV + B_W, B_W):
        for i in range(2):
            kb_ref[c + i] = rope(p[:, i * LANES:(i + 1) * LANES])
    for c, p in blocks(A_Q + 2 * A_KV + 2 * B_W, B_W):
        for i in range(2):
            vb_ref[c + i] = p[:, i * LANES:(i + 1) * LANES]


def _proj(x2, g_mix, w_in, seq):
    t, d = x2.shape
    tm = PROJ_TM
    assert t % tm == 0 and seq % tm == 0 and A_KV == LANES and A_KV_HEADS == 2
    nseq = seq // tm
    tables = _rope_tables(seq)
    slab = lambda i: (0, i, 0)
    const = lambda i: (0, 0)
    tab = lambda i: (i % nseq, 0)
    widths = (A_Q, A_KV_HEADS * LANES, A_KV_HEADS * LANES, B_W, B_W, B_W)
    return pl.pallas_call(
        _proj_kernel,
        grid=(t // tm,),
        in_specs=[pl.BlockSpec((tm, d), lambda i: (i, 0)), pl.BlockSpec((1, d), const),
                  pl.BlockSpec(w_in.shape, const)] + [pl.BlockSpec((tm, LANES), tab)] * 3,
        out_specs=[pl.BlockSpec((w // LANES, tm, LANES), slab) for w in widths],
        out_shape=[jax.ShapeDtypeStruct((w // LANES, t, LANES), dt)
                   for w, dt in zip(widths, (BF16, BF16, BF16, F32, F32, F32))],
        scratch_shapes=[pltpu.VMEM(w_in.shape, BF16)],
        compiler_params=_cparams(("arbitrary",), VMEM_LIMIT),
        name="proj",
    )(x2, g_mix.reshape(1, d), w_in, *tables)


def _band_bias(bias_ref, qb, kw, hw):
    kk = lax.broadcasted_iota(jnp.int32, (kw, qb), 0)
    qq = lax.broadcasted_iota(jnp.int32, (kw, qb), 1)
    for t, delta in enumerate((0, -hw, -2 * hw)):
        bias_ref[t] = jnp.where(jnp.abs(kk + delta - qq) <= hw, 0.0, NEG_INF)


def _band_scratch(qb, kw, slots):
    n = 2 * slots
    return ([pltpu.VMEM((kw, 2 * qb), F32)] * n + [pltpu.VMEM((kw, 2 * qb), BF16)] * n
            + [pltpu.VMEM((SUBLANES, 2 * qb), F32)] * n + [pltpu.VMEM((LANES, kw), BF16)] * n
            + [pltpu.VMEM((SUBLANES, 2 * qb), F32)] * n)


def _band_pipeline(q_ref, k_ref, v_ref, bias_ref, o_ref, lse_ref, bufs, *, nblocks, geometry, stage_vt,
                   sink_row=None):
    qb = ATT_QB
    n = len(bufs) // 5
    slots = n // 2
    s_bufs, p_bufs, ml_bufs, vt_bufs, mx_bufs = (bufs[i * n:(i + 1) * n] for i in range(5))
    kw = s_bufs[0].shape[0]

    def rows(block):
        q0, k0, kind = geometry(block)
        return pl.ds(q0, qb), pl.ds(k0, kw), kind

    def scores(block, s_buf, mx_buf):
        q_rows, k_rows, kind = rows(block)
        q = q_ref[q_rows, :]
        lo = lax.broadcasted_iota(jnp.int32, q.shape, 1) < HEAD_DIM
        q2 = jnp.concatenate([jnp.where(lo, q, 0.0), jnp.where(lo, 0.0, q)], axis=0).astype(BF16)
        s = lax.dot_general(k_ref[k_rows, :].astype(BF16), q2, (((1,), (1,)), ((), ())),
                            preferred_element_type=F32)
        bias = bias_ref[kind]
        for hh in range(2):
            s_h = s[:, hh * qb:(hh + 1) * qb] + bias
            s_buf[:, hh * qb:(hh + 1) * qb] = s_h
            mx_buf[0:1, hh * qb:(hh + 1) * qb] = jnp.max(s_h, axis=0, keepdims=True)

    def softmax(block, s_buf, mx_buf, p_buf, ml_buf, vt_buf):
        if stage_vt:
            _, k_rows, _ = rows(block)
            vt_buf[...] = v_ref[k_rows, :].astype(F32).T.astype(BF16)
        m = mx_buf[0:1, :]
        if sink_row is not None:
            m = jnp.maximum(m, sink_row)
        p = jnp.exp2(s_buf[...] - m)
        l = jnp.sum(p, axis=0, keepdims=True)
        if sink_row is not None:
            l = l + jnp.exp2(sink_row - m)
        p_buf[...] = p.astype(BF16)
        ml_buf[0:1, :] = m
        ml_buf[1:2, :] = l

    def output(block, p_buf, ml_buf, vt_buf):
        q_rows, k_rows, _ = rows(block)
        if stage_vt:
            o_t = jnp.dot(vt_buf[...], p_buf[...], preferred_element_type=F32)
        else:
            o_t = lax.dot_general(v_ref[k_rows, :].astype(BF16), p_buf[...], (((0,), (0,)), ((), ())),
                                  preferred_element_type=F32)
        o_tc = jnp.concatenate([o_t[:HEAD_DIM, :qb], o_t[HEAD_DIM:, qb:]], axis=0)
        top = lax.broadcasted_iota(jnp.int32, (LANES, qb), 0) < HEAD_DIM
        m, l = ml_buf[0:1, :], ml_buf[1:2, :]
        inv = 1.0 / l
        o_ref[q_rows, :] = (o_tc * jnp.where(top, inv[:, :qb], inv[:, qb:])).T
        if lse_ref is not None:
            lse = m + jnp.log2(l)
            lse_ref[q_rows, :] = jnp.where(top, lse[:, :qb], lse[:, qb:]).T

    assert nblocks % slots == 0
    for trip in range(nblocks // slots + 2):
        for slot in range(slots):
            t = trip * slots + slot
            rd, wr = (1 - trip % 2) * slots + slot, (trip % 2) * slots + slot
            if 0 <= t - 2 * slots:
                output(t - 2 * slots, p_bufs[rd], ml_bufs[rd], vt_bufs[rd])
            if 0 <= t - slots < nblocks:
                softmax(t - slots, s_bufs[rd], mx_bufs[rd], p_bufs[wr], ml_bufs[wr], vt_bufs[wr])
            if t < nblocks:
                scores(t, s_bufs[wr], mx_bufs[wr])


def _segment_geometry(block, n, hw, base=0):
    qb = ATT_QB
    nblk = n // qb
    seg, jb = divmod(block, nblk)
    qs = jb * qb
    ks = min(max(qs - hw, 0), n - (qb + 2 * hw))
    kind = 0 if jb == 0 else 2 if jb == nblk - 1 else 1
    return base + seg * n + qs, base + seg * n + ks, kind


def _attn_a_kernel(sink_ref, q_ref, k_ref, v_ref, o_ref, bias_ref, *bufs, seq):
    j = pl.program_id(1)
    qb, hw = ATT_QB, A_HALF_WINDOW
    _band_bias(bias_ref, qb, qb + 2 * hw, hw)
    col = lax.broadcasted_iota(jnp.int32, (1, 2 * qb), 1)
    sink_row = jnp.where(col < qb, sink_ref[2 * j], sink_ref[2 * j + 1]) * LOG2_E
    _band_pipeline(q_ref, k_ref, v_ref, bias_ref, o_ref, None, bufs, nblocks=seq // qb,
                   geometry=functools.partial(_segment_geometry, n=seq, hw=hw), stage_vt=False,
                   sink_row=sink_row)


def _attn_a(qa, ka, va, sink, batch, seq):
    qb, hw = ATT_QB, A_HALF_WINDOW
    assert seq % qb == 0 and seq >= qb + 2 * hw and A_KV == LANES
    qspec = pl.BlockSpec((None, seq, LANES), lambda b, j: (j, b, 0))
    per_kv = A_GROUP * HEAD_DIM // LANES
    kvspec = pl.BlockSpec((None, seq, LANES), lambda b, j: (j // per_kv, b, 0))
    return pl.pallas_call(
        functools.partial(_attn_a_kernel, seq=seq),
        grid=(batch, A_Q // LANES),
        in_specs=[pl.BlockSpec(memory_space=pltpu.SMEM), qspec, kvspec, kvspec],
        out_specs=qspec,
        out_shape=jax.ShapeDtypeStruct(qa.shape, F32),
        scratch_shapes=[pltpu.VMEM((3, qb + 2 * hw, qb), F32)] + _band_scratch(qb, qb + 2 * hw, ATT_A_SLOTS),
        compiler_params=_cparams(("parallel", "parallel"), VMEM_LIMIT),
        name="attn_a",
    )(sink, qa, ka, va)


def _attn_b_kernel(q_ref, k_ref, v_ref, out_ref, bias_ref, tmp_ref, qs_ref, ks_ref, vs_ref, os_ref, ls_ref,
                   *bufs, seq):
    (w1, d1), (w4, d4), (w16, d16) = DILATED_PATTERNS
    hw = w1 // (2 * d1)
    qb = ATT_QB
    _band_bias(bias_ref, qb, qb + 2 * hw, hw)
    n4, n16 = seq // d4, seq // d16
    step = d16 // d4
    chunk = 512
    for src, dst in ((q_ref, qs_ref), (k_ref, ks_ref), (v_ref, vs_ref)):
        for c in range(seq // chunk):
            dst[c * chunk:(c + 1) * chunk, :] = src[c * chunk:(c + 1) * chunk, :].astype(dst.dtype)
        for r in range(d4):
            tmp_ref[r * n4:(r + 1) * n4, :] = src[pl.ds(r, n4, stride=d4), :]
        for c in range(seq // chunk):
            dst[seq + c * chunk:seq + (c + 1) * chunk, :] = tmp_ref[c * chunk:(c + 1) * chunk, :].astype(dst.dtype)
        for r in range(d16):
            dst[2 * seq + r * n16:2 * seq + (r + 1) * n16, :] = tmp_ref[
                pl.ds((r % d4) * n4 + r // d4, n16, stride=step), :].astype(dst.dtype)

    bps = seq // qb

    def geometry(block):
        pat, within = divmod(block, bps)
        return _segment_geometry(within, seq // DILATED_PATTERNS[pat][1], hw, base=pat * seq)

    _band_pipeline(qs_ref, ks_ref, vs_ref, bias_ref, os_ref, ls_ref, bufs,
                   nblocks=len(DILATED_PATTERNS) * bps, geometry=geometry, stage_vt=True)

    def merge(r, carry):
        rows16 = pl.ds(pl.multiple_of(2 * seq + r * n16, SUBLANES), n16)
        rows4 = pl.ds(seq + (r % d4) * n4 + r // d4, n16, stride=step)
        rows1 = pl.ds(r, n16, stride=d16)
        la, lb, lc = ls_ref[rows1, :], ls_ref[rows4, :], ls_ref[rows16, :]
        top = jnp.maximum(jnp.maximum(la, lb), lc)
        wa, wb, wc = jnp.exp2(la - top), jnp.exp2(lb - top), jnp.exp2(lc - top)
        num = wa * os_ref[rows1, :] + wb * os_ref[rows4, :] + wc * os_ref[rows16, :]
        out_ref[rows1, :] = num * (1.0 / (wa + wb + wc))
        return carry

    lax.fori_loop(0, d16, merge, 0, unroll=4)


def _attn_b(qb_, kb, vb, batch, seq):
    (w1, d1), (w4, d4), (w16, d16) = DILATED_PATTERNS
    hw = w1 // (2 * d1)
    qb = ATT_QB
    assert d1 == 1 and d16 % d4 == 0 and w4 // (2 * d4) == hw and w16 // (2 * d16) == hw
    assert seq % 512 == 0 and seq // d16 >= qb + 2 * hw
    for (_, d) in DILATED_PATTERNS:
        nblk = seq // d // qb
        assert nblk * d * qb == seq and nblk & (nblk - 1) == 0
    spec = pl.BlockSpec((None, seq, LANES), lambda b, p: (p, b, 0))
    npat = len(DILATED_PATTERNS)
    return pl.pallas_call(
        functools.partial(_attn_b_kernel, seq=seq),
        grid=(batch, B_W // LANES),
        in_specs=[spec] * 3,
        out_specs=spec,
        out_shape=jax.ShapeDtypeStruct(qb_.shape, F32),
        scratch_shapes=[pltpu.VMEM((3, qb + 2 * hw, qb), F32), pltpu.VMEM((seq, LANES), F32),
                        pltpu.VMEM((npat * seq, LANES), F32), pltpu.VMEM((npat * seq, LANES), BF16),
                        pltpu.VMEM((npat * seq, LANES), BF16), pltpu.VMEM((npat * seq, LANES), F32),
                        pltpu.VMEM((npat * seq, LANES), F32)] + _band_scratch(qb, qb + 2 * hw, ATT_B_SLOTS),
        compiler_params=_cparams(("parallel", "parallel"), VMEM_LIMIT),
        name="attn_b",
    )(qb_, kb, vb)


def _mix_kernel(x_ref, oa_ref, ob_ref, ga_ref, gb_ref, wo_f32_ref, gf_ref, wr_ref,
                x1_ref, h2_ref, aff_ref, wo_ref):
    @pl.when(pl.program_id(0) == 0)
    def _cast_weights():
        wo_ref[...] = wo_f32_ref[...].astype(BF16)

    oa = jnp.concatenate([oa_ref[c] for c in range(oa_ref.shape[0])], axis=1)
    ob = jnp.concatenate([ob_ref[c] for c in range(ob_ref.shape[0])], axis=1)
    mixed = jnp.concatenate([_rms(oa, ga_ref[...]), _rms(ob, gb_ref[...])], axis=1).astype(BF16)
    for n in range(x_ref.shape[1] // MXU_COLS):
        cols = slice(n * MXU_COLS, (n + 1) * MXU_COLS)
        x1_ref[:, cols] = x_ref[:, cols] + jnp.dot(mixed, wo_ref[:, cols], preferred_element_type=F32)
    x1 = x1_ref[...]
    h2 = _rms(x1, gf_ref[...])
    tm = h2.shape[0]
    for c in range(h2.shape[1] // LANES):
        h2_ref[pl.ds(c, tm, stride=SUBLANES), :] = h2[:, c * LANES:(c + 1) * LANES]
    ne = aff_ref.shape[0]
    dn = (((1,), (1,)), ((), ()))
    h_hi = h2.astype(BF16)
    h_mid = (h2 - h_hi.astype(F32)).astype(BF16)
    r = lax.dot_general(wr_ref[...], h_hi, dn, preferred_element_type=F32)
    logits = r[:ne] + r[ne:] + lax.dot_general(wr_ref[:ne, :], h_mid, dn, preferred_element_type=F32)
    e = jnp.exp(logits - jnp.max(logits, axis=0, keepdims=True))
    aff_ref[...] = e / jnp.sum(e, axis=0, keepdims=True)


def _mix(x2, oa, ob, g_out_a, g_out_b, w_out, g_ffn, w_router, batch, seq):
    t, d = x2.shape
    tm = MIX_TM
    assert seq % tm == 0 and d % LANES == 0 and d // LANES == SUBLANES
    ns = seq // tm
    row = lambda i: (i, 0)
    const = lambda i: (0, 0)
    ne = w_router.shape[1]
    w_hi = w_router.T.astype(BF16)
    w_mid = (w_router.T - w_hi.astype(F32)).astype(BF16)
    wr = jnp.concatenate([w_hi, w_mid], axis=0)
    return pl.pallas_call(
        _mix_kernel,
        grid=(t // tm,),
        in_specs=[pl.BlockSpec((tm, d), row), pl.BlockSpec((A_Q // LANES, tm, LANES), lambda i: (0, i, 0)),
                  pl.BlockSpec((B_W // LANES, tm, LANES), lambda i: (0, i, 0)),
                  pl.BlockSpec((1, A_Q), const), pl.BlockSpec((1, B_W), const),
                  pl.BlockSpec(w_out.shape, const), pl.BlockSpec((1, d), const),
                  pl.BlockSpec((2 * ne, d), const)],
        out_specs=[pl.BlockSpec((tm, d), row), pl.BlockSpec((tm * SUBLANES, LANES), row),
                   pl.BlockSpec((None, ne, tm), lambda i: (i // ns, 0, i % ns))],
        out_shape=[jax.ShapeDtypeStruct((t, d), F32),
                   jax.ShapeDtypeStruct((t * SUBLANES, LANES), F32),
                   jax.ShapeDtypeStruct((batch, ne, seq), F32)],
        scratch_shapes=[pltpu.VMEM(w_out.shape, BF16)],
        compiler_params=_cparams(("arbitrary",), VMEM_LIMIT),
        name="mix",
    )(x2, oa, ob, g_out_a.reshape(1, -1), g_out_b.reshape(1, -1), w_out, g_ffn.reshape(1, d), wr)


def _lane_cumsum(mask_f, tri):
    rows, s = mask_f.shape
    carry = jnp.zeros((rows, 1), F32)
    out = []
    for c in range(s // LANES):
        inc = jnp.dot(mask_f[:, c * LANES:(c + 1) * LANES].astype(BF16), tri,
                      preferred_element_type=F32) + carry
        out.append(inc)
        carry = inc[:, LANES - 1:LANES]
    return jnp.concatenate(out, axis=1)


def _route_kernel(aff_ref, idx_e_ref, idx_l_ref, gate_ref, *, cap, seq):
    b = pl.program_id(0)
    aff = aff_ref[...]
    ne = aff.shape[0]

    def search(i, thr):
        lo = 28 - 2 * i
        for digit in (1, 2, 3):
            cand = thr | (jnp.int32(digit) << lo)
            cnt = jnp.sum((aff >= lax.bitcast_convert_type(cand, F32)).astype(F32), axis=1, keepdims=True)
            best = jnp.where(cnt >= cap, cand, thr if digit == 1 else best)
        return best

    thr = lax.bitcast_convert_type(lax.fori_loop(0, 15, search, jnp.zeros((ne, 1), jnp.int32)), F32)
    gt = aff > thr
    eq = aff == thr
    n_gt = jnp.sum(gt.astype(F32), axis=1, keepdims=True)
    ri = lax.broadcasted_iota(jnp.int32, (LANES, LANES), 0)
    ci = lax.broadcasted_iota(jnp.int32, (LANES, LANES), 1)
    tri = (ri <= ci).astype(BF16)
    eq_f = eq.astype(F32)
    eq_rank = _lane_cumsum(eq_f, tri) - eq_f
    sel = gt | (eq & (eq_rank < cap - n_gt))
    sel_f = sel.astype(F32)
    slot = _lane_cumsum(sel_f, tri) - sel_f
    key_t = jnp.where(sel, slot, -1.0).T
    tok = lax.broadcasted_iota(jnp.int32, (SUBLANES, seq), 1)
    piece = lax.broadcasted_iota(jnp.int32, (SUBLANES, seq), 0)
    tok_rows = jnp.where(piece == 0, tok // 64, jnp.where(piece == 1, tok % 64, 0)).astype(F32)
    p_iota = lax.broadcasted_iota(jnp.int32, (1, cap), 1).astype(F32)
    for e in range(ne):
        onehot = (key_t[:, e:e + 1] == p_iota).astype(BF16)
        g = aff[e:e + 1, :]
        g_hi = g.astype(BF16).astype(F32)
        g_mid = (g - g_hi).astype(BF16).astype(F32)
        g_lo = g - g_hi - g_mid
        lhs = jnp.where(piece == 2, g_hi, jnp.where(piece == 3, g_mid, jnp.where(piece == 4, g_lo, tok_rows)))
        res = jnp.dot(lhs.astype(BF16), onehot, preferred_element_type=F32)
        tile_row = (res[0:1] * 64.0 + res[1:2]).astype(jnp.int32) * SUBLANES
        idx_l_ref[e] = tile_row + (b % 2) * (seq * SUBLANES)
        idx_e_ref[e] = tile_row + b * (seq * SUBLANES)
        gate_ref[e] = res[2:3] + res[3:4] + res[4:5]


def _route(aff_t, cap):
    batch, ne, seq = aff_t.shape
    assert seq % LANES == 0 and seq <= 64 * 256
    return pl.pallas_call(
        functools.partial(_route_kernel, cap=cap, seq=seq),
        grid=(batch,),
        in_specs=[pl.BlockSpec((None, ne, seq), lambda b: (b, 0, 0))],
        out_specs=[pl.BlockSpec((ne, 1, cap), lambda b: (0, 0, b)), pl.BlockSpec((ne, 1, cap), lambda b: (b, 0, 0)),
                   pl.BlockSpec((ne, 1, cap), lambda b: (b, 0, 0))],
        out_shape=[jax.ShapeDtypeStruct((ne, 1, batch * cap), jnp.int32),
                   jax.ShapeDtypeStruct((batch * ne, 1, cap), jnp.int32),
                   jax.ShapeDtypeStruct((batch * ne, 1, cap), F32)],
        compiler_params=_cparams(("parallel",), VMEM_LIMIT),
        name="route",
    )(aff_t)


def _experts_kernel(idx_ref, idx_next_ref, h_hbm, wg_ref, wu_ref, wd_ref, y_ref, rows_ref, *rest, ne, nf):
    xe_refs, sems = rest[:-1], rest[-1]
    e, f = pl.program_id(0), pl.program_id(1)
    chunks = len(xe_refs)
    d = xe_refs[0].shape[1]
    ntok = chunks * FFN_TM
    slot = e % 2
    per_chunk = ntok // (nf * chunks)

    def row_copy(idx_smem, i, buf):
        src = pl.multiple_of(idx_smem[0, 0, i], SUBLANES)
        dst = pl.multiple_of(i * SUBLANES, SUBLANES)
        return pltpu.make_async_copy(h_hbm.at[pl.ds(src, SUBLANES), :],
                                     rows_ref.at[buf, pl.ds(dst, SUBLANES), :], sems.at[buf])

    def wait_rows(buf):
        pltpu.make_async_copy(h_hbm.at[pl.ds(0, ntok * SUBLANES), :], rows_ref.at[buf], sems.at[buf]).wait()

    @pl.when((e == 0) & (f == 0))
    def _first_gather():
        def issue(i, carry):
            row_copy(idx_ref, i, 0).start()
            return carry

        lax.fori_loop(0, ntok, issue, 0, unroll=8)

    def body(first):
        wg = wg_ref[...].astype(BF16)
        wu = wu_ref[...].astype(BF16)
        wd = wd_ref[...].astype(BF16)
        for m in range(chunks):
            if first:
                xe_refs[m][...] = jnp.concatenate(
                    [rows_ref[slot, pl.ds(m * FFN_TM * SUBLANES + c, FFN_TM, stride=SUBLANES), :]
                     for c in range(d // LANES)], axis=1).astype(BF16)
            xe = xe_refs[m][...]
            hmid = []
            for n in range(wg.shape[1] // MXU_COLS):
                cols = slice(n * MXU_COLS, (n + 1) * MXU_COLS)
                a = jnp.dot(xe, wg[:, cols], preferred_element_type=F32)
                u = jnp.dot(xe, wu[:, cols], preferred_element_type=F32)
                hmid.append((a * jax.nn.sigmoid(a) * u).astype(BF16))
            part = jnp.dot(jnp.concatenate(hmid, axis=1), wd, preferred_element_type=F32)
            rows = slice(m * FFN_TM, (m + 1) * FFN_TM)
            y_ref[rows, :] = part if first else y_ref[rows, :] + part
            for i in range(per_chunk):
                row_copy(idx_next_ref, (f * chunks + m) * per_chunk + i, 1 - slot).start()

    @pl.when(f == 0)
    def _first_step():
        wait_rows(slot)
        body(True)

    @pl.when(f > 0)
    def _later_steps():
        body(False)

    @pl.when((e == ne - 1) & (f == nf - 1))
    def _drain():
        wait_rows(1 - slot)


def _experts(idx_e, h2_rows, w_gate, w_up, w_down):
    ne, d, ff = w_gate.shape
    ntok = idx_e.shape[-1]
    tf = FFN_TF
    nf = ff // tf
    assert ff % tf == 0 and ntok % FFN_TM == 0 and ntok % (nf * (ntok // FFN_TM)) == 0 and d // LANES == SUBLANES
    return pl.pallas_call(
        functools.partial(_experts_kernel, ne=ne, nf=nf),
        grid=(ne, nf),
        in_specs=[pl.BlockSpec((1, 1, ntok), lambda e, f: (e, 0, 0), memory_space=pltpu.SMEM),
                  pl.BlockSpec((1, 1, ntok), lambda e, f: (jnp.minimum(e + 1, ne - 1), 0, 0),
                               memory_space=pltpu.SMEM),
                  pl.BlockSpec(memory_space=pl.ANY),
                  pl.BlockSpec((None, d, tf), lambda e, f: (e, 0, f)),
                  pl.BlockSpec((None, d, tf), lambda e, f: (e, 0, f)),
                  pl.BlockSpec((None, tf, d), lambda e, f: (e, f, 0))],
        out_specs=pl.BlockSpec((None, ntok, d), lambda e, f: (e, 0, 0)),
        out_shape=jax.ShapeDtypeStruct((ne, ntok, d), F32),
        scratch_shapes=[pltpu.VMEM((2, ntok * SUBLANES, LANES), F32)]
        + [pltpu.VMEM((FFN_TM, d), BF16)] * (ntok // FFN_TM) + [pltpu.SemaphoreType.DMA((2,))],
        compiler_params=_cparams(("arbitrary", "arbitrary"), VMEM_LIMIT),
        name="experts",
    )(idx_e, idx_e, h2_rows, w_gate, w_up, w_down)


def _combine_kernel(*refs, batch, per_step):
    idx_refs, gate_refs, y_refs = (refs[i * per_step:(i + 1) * per_step] for i in range(3))
    x1_ref, g_ref, o_ref, acc_ref, ybuf_ref = refs[3 * per_step:]
    b, j = pl.program_id(0), pl.program_id(1)
    cap, d = y_refs[0].shape
    tm = x1_ref.shape[0]
    half_rows = acc_ref.shape[0] // 2
    cur = pl.multiple_of((b % 2) * half_rows, SUBLANES)
    prev = pl.multiple_of(((b + 1) % 2) * half_rows, SUBLANES)

    @pl.when(b < batch)
    def _scatter():
        @pl.when(j == 0)
        def _init():
            acc_ref[pl.ds(cur, half_rows), :] = jnp.zeros((half_rows, LANES), F32)

        for idx_ref, gate_ref, y_ref in zip(idx_refs, gate_refs, y_refs):
            gate_col = jnp.broadcast_to(gate_ref[0], (LANES, cap)).T
            for c in range(d // LANES):
                ybuf_ref[pl.ds(c, cap, stride=SUBLANES), :] = y_ref[:, c * LANES:(c + 1) * LANES] * gate_col

            def group(k, carry, idx_ref=idx_ref):
                rows, sums = [], []
                for u in range(RMW_UNROLL):
                    i = k * RMW_UNROLL + u
                    t = pl.ds(pl.multiple_of(idx_ref[0, 0, i], SUBLANES), SUBLANES)
                    yi = ybuf_ref[pl.ds(pl.multiple_of(i * SUBLANES, SUBLANES), SUBLANES), :]
                    rows.append(t)
                    sums.append(acc_ref[t, :] + yi)
                for t, s in zip(rows, sums):
                    acc_ref[t, :] = s
                return carry

            lax.fori_loop(0, cap // RMW_UNROLL, group, 0, unroll=True)

    @pl.when(b > 0)
    def _emit():
        base = prev + pl.multiple_of(j * tm * SUBLANES, SUBLANES)
        moe = jnp.concatenate([acc_ref[pl.ds(base + c, tm, stride=SUBLANES), :]
                               for c in range(d // LANES)], axis=1)
        o_ref[...] = _rms(x1_ref[...] + moe, g_ref[...])


def _combine(idx_l, gates, y, x1, g_final, batch, seq):
    ne = y.shape[0]
    t, d = x1.shape
    cap = idx_l.shape[-1]
    per_step = COMBINE_EXPERTS
    steps = ne // per_step
    tm = seq // steps
    assert cap % RMW_UNROLL == 0 and d // LANES == SUBLANES and tm * steps == seq and ne % per_step == 0
    scattered = lambda b: jnp.minimum(b, batch - 1)
    rows = lambda b, j: (jnp.maximum(b - 1, 0) * steps + jnp.where(b > 0, j, 0), 0)
    slot_specs = [pl.BlockSpec((1, 1, cap), functools.partial(
        lambda b, j, x: (scattered(b) * ne + j * per_step + x, 0, 0), x=x), memory_space=ms)
        for ms in (pltpu.SMEM, None) for x in range(per_step)]
    y_specs = [pl.BlockSpec((None, cap, d), functools.partial(
        lambda b, j, x: (j * per_step + x, scattered(b), 0), x=x)) for x in range(per_step)]
    return pl.pallas_call(
        functools.partial(_combine_kernel, batch=batch, per_step=per_step),
        grid=(batch + 1, steps),
        in_specs=slot_specs + y_specs + [pl.BlockSpec((tm, d), rows), pl.BlockSpec((1, d), lambda b, j: (0, 0))],
        out_specs=pl.BlockSpec((tm, d), rows),
        out_shape=jax.ShapeDtypeStruct((t, d), F32),
        scratch_shapes=[pltpu.VMEM((2 * seq * SUBLANES, LANES), F32), pltpu.VMEM((cap * SUBLANES, LANES), F32)],
        compiler_params=_cparams(("arbitrary", "arbitrary"), VMEM_LIMIT),
        name="combine",
    )(*([idx_l] * per_step + [gates] * per_step + [y] * per_step), x1, g_final.reshape(1, d))


def kernel(x, g_mix, w_in, a_sink, g_out_a, g_out_b, w_out, g_ffn, w_router, w_gate, w_up, w_down, g_final):
    batch, seq, d = x.shape
    depth = w_in.shape[0]
    ne = w_router.shape[-1]
    cap = CAPACITY_FACTOR * seq // ne
    assert depth == 1, "the last kernel fuses the MoE residual with the output norm of a single layer"
    l = 0
    x2 = x.reshape(batch * seq, d)
    qa, ka, va, qb, kb, vb = _proj(x2, g_mix[l], w_in[l], seq)
    oa = _attn_a(qa, ka, va, a_sink[l], batch, seq)
    ob = _attn_b(qb, kb, vb, batch, seq)
    x1, h2_rows, aff_t = _mix(x2, oa, ob, g_out_a[l], g_out_b[l], w_out[l], g_ffn[l], w_router[l], batch, seq)
    idx_e, idx_l, gates = _route(aff_t, cap)
    y = _experts(idx_e, h2_rows, w_gate[l], w_up[l], w_down[l])
    out = _combine(idx_l, gates, y, x1, g_final, batch, seq)
    return out.reshape(batch, seq, d)
```
